```python
import math
import jax, jax.numpy as jnp
from jax import lax
import numpy as np

D_MODEL = 1024
BATCH = 8
SEQ = 8192
DEPTH = 2
DEC_BATCH = 2
DEC_SEQ = 8192
PAST_LEN = 128

RMS_EPS = 1e-6
LRU_WIDTH = D_MODEL // 2
LRU_BLOCKS = 8
LRU_BLOCK = LRU_WIDTH // LRU_BLOCKS
LRU_CONV = 4
LRU_CONV_LEFT = 2
LRU_C = 8.0
RWKV_WIDTH = D_MODEL // 2
RWKV_HEAD = 64
RWKV_HEADS = RWKV_WIDTH // RWKV_HEAD
RWKV_W_RANK = 64
RWKV_A_RANK = 64
RWKV_G_RANK = 128
RWKV_COLS = 3 * RWKV_WIDTH + RWKV_W_RANK + RWKV_A_RANK + RWKV_G_RANK
EVEN_IN = 2 * LRU_WIDTH + RWKV_COLS
RWKV_DECAY_SCALE = math.exp(-0.5)
RWKV_GN_EPS = 64e-5
MLSTM_INNER = 2 * D_MODEL
MLSTM_HEADS = 4
MLSTM_HEAD = MLSTM_INNER // MLSTM_HEADS
MLSTM_QK_BLOCK = 4
MLSTM_NBLK = MLSTM_INNER // MLSTM_QK_BLOCK
MLSTM_CONV = 4
MLSTM_CONV_LEFT = 2
MLSTM_CHUNK = 128
MLSTM_LN_EPS = 1e-5
ODD_IN = 2 * MLSTM_INNER
D_FF = 2816
FFN_CONV = 3
N_EVEN = (DEPTH + 1) // 2
N_ODD = DEPTH // 2

kernel_name = 'hybrid_rglru_rwkv7_mlstm_encoder'


def _f32(t):
    return t.astype(jnp.float32)


def rms_norm(x, g):
    xf = _f32(x)
    y = xf * lax.rsqrt(jnp.mean(xf * xf, axis=-1, keepdims=True) + RMS_EPS)
    return (y * _f32(g)).astype(x.dtype)


def depthwise_conv(x, w, b, pad_left):
    k = w.shape[0]
    t = x.shape[1]
    xp = jnp.pad(x, ((0, 0), (pad_left, k - 1 - pad_left), (0, 0)))
    y = xp[:, 0:t] * w[0]
    for j in range(1, k):
        y = y + xp[:, j:j + t] * w[j]
    return y + b


def block_diag(x, w):
    b, t, _ = x.shape
    nb, bi, bo = w.shape
    return jnp.einsum('btni,nio->btno', x.reshape(b, t, nb, bi), w).reshape(b, t, nb * bo)


def centred_shift(x):
    zero = jnp.zeros_like(x[:, :1])
    prev = jnp.concatenate([zero, x[:, :-1]], axis=1)
    nxt = jnp.concatenate([x[:, 1:], zero], axis=1)
    return 0.5 * (prev + nxt)


def rglru_scan(xc, w_a, b_a, w_x, b_x, lam, reverse):
    xf = _f32(xc)
    r = jax.nn.sigmoid(_f32(block_diag(xc, w_a) + b_a))
    i = jax.nn.sigmoid(_f32(block_diag(xc, w_x) + b_x))
    log_a = -LRU_C * r * jax.nn.softplus(-_f32(lam))
    a = jnp.exp(log_a)
    u = jnp.sqrt(-jnp.expm1(2.0 * log_a)) * (i * xf)

    def combine(left, right):
        a_l, h_l = left
        a_r, h_r = right
        return a_l * a_r, a_r * h_l + h_r

    _, h = lax.associative_scan(combine, (a, u), reverse=reverse, axis=1)
    return h


def rwkv7_scan(r, w, k, v, kk, akk, reverse):
    b, t, h, n = r.shape
    tm = lambda z: jnp.moveaxis(z, 1, 0)

    def step(s, inp):
        r_t, w_t, k_t, v_t, kk_t, akk_t = inp
        s = (s * w_t[:, :, None, :]
             - jnp.einsum('bhvk,bhk->bhv', s, kk_t)[..., None] * akk_t[:, :, None, :]
             + v_t[..., None] * k_t[:, :, None, :])
        return s, jnp.einsum('bhvk,bhk->bhv', s, r_t)

    s0 = jnp.zeros((b, h, n, n), jnp.float32)
    _, y = lax.scan(step, s0, (tm(r), tm(w), tm(k), tm(v), tm(kk), tm(akk)), reverse=reverse)
    return jnp.moveaxis(y, 0, 1)


def rwkv7_mixer(rw, mu, w0, w_up, a0, a_up, g_up, k_k, k_a, r_k, ln_w, ln_b):
    bsz, t, _ = rw.shape
    rw = _f32(rw)
    rw = rw + _f32(mu) * (centred_shift(rw) - rw)
    c = RWKV_WIDTH
    r = rw[..., 0:c]
    k = rw[..., c:2 * c]
    v = rw[..., 2 * c:3 * c]
    wd = rw[..., 3 * c:3 * c + RWKV_W_RANK]
    ad = rw[..., 3 * c + RWKV_W_RANK:3 * c + RWKV_W_RANK + RWKV_A_RANK]
    gd = rw[..., 3 * c + RWKV_W_RANK + RWKV_A_RANK:]
    heads = lambda z: z.reshape(bsz, t, RWKV_HEADS, RWKV_HEAD)
    kk = heads(k * _f32(k_k))
    kk = kk / jnp.maximum(jnp.sqrt(jnp.sum(kk * kk, axis=-1, keepdims=True)), 1e-12)
    tw = jnp.tanh(wd)
    ys = []
    for d, rev in enumerate((False, True)):
        w = jnp.exp(-RWKV_DECAY_SCALE * jax.nn.sigmoid(_f32(w0[d]) + tw @ _f32(w_up[d])))
        a = jax.nn.sigmoid(_f32(a0[d]) + ad @ _f32(a_up[d]))
        kt = k * (1.0 + (a - 1.0) * _f32(k_a))
        ys.append(rwkv7_scan(heads(r), heads(w), heads(kt), heads(v), kk, heads(a) * kk, rev))
    y = ys[0] + ys[1]
    mean = jnp.mean(y, axis=-1, keepdims=True)
    var = jnp.mean(jnp.square(y - mean), axis=-1, keepdims=True)
    y = ((y - mean) * lax.rsqrt(var + RWKV_GN_EPS)).reshape(bsz, t, c) * _f32(ln_w) + _f32(ln_b)
    bonus = (jnp.sum(heads(r) * heads(k) * _f32(r_k), axis=-1, keepdims=True) * heads(v)).reshape(bsz, t, c)
    g = jax.nn.sigmoid(gd) @ _f32(g_up)
    return (y + bonus) * g


def even_mixer(h, w_in, lru_conv_w, lru_conv_b, lru_wa, lru_ba, lru_wx, lru_bx, lru_lambda,
               rw_mu, rw_w0, rw_w_up, rw_a0, rw_a_up, rw_g_up, rw_k_k, rw_k_a, rw_r_k,
               rw_ln_w, rw_ln_b, w_out):
    p = h @ w_in
    lru_x = p[..., :LRU_WIDTH]
    lru_g = p[..., LRU_WIDTH:2 * LRU_WIDTH]
    rw = p[..., 2 * LRU_WIDTH:]
    xc = depthwise_conv(lru_x, lru_conv_w, lru_conv_b, LRU_CONV_LEFT)
    hl = (rglru_scan(xc, lru_wa[0], lru_ba[0], lru_wx[0], lru_bx[0], lru_lambda[0], False)
          + rglru_scan(xc, lru_wa[1], lru_ba[1], lru_wx[1], lru_bx[1], lru_lambda[1], True))
    lru_out = hl * jax.nn.gelu(_f32(lru_g))
    rw_out = rwkv7_mixer(rw, rw_mu, rw_w0, rw_w_up, rw_a0, rw_a_up, rw_g_up, rw_k_k, rw_k_a,
                         rw_r_k, rw_ln_w, rw_ln_b)
    mixed = jnp.concatenate([lru_out, rw_out], axis=-1).astype(h.dtype)
    return mixed @ w_out


def mlstm_chunkwise(q, k, v, li, lf, reverse):
    if reverse:
        q, k, v, li, lf = (jnp.flip(z, axis=1) for z in (q, k, v, li, lf))
    bsz, t, _ = q.shape
    nc = t // MLSTM_CHUNK

    def to_chunks(z):
        return _f32(z).reshape(bsz, nc, MLSTM_CHUNK, MLSTM_HEADS, -1).transpose(1, 0, 3, 2, 4)

    qc = to_chunks(q) * (MLSTM_HEAD ** -0.5)
    kc = to_chunks(k)
    vc = to_chunks(v)
    lic = to_chunks(li)[..., 0]
    lfc = to_chunks(lf)[..., 0]
    tri = jnp.tril(jnp.ones((MLSTM_CHUNK, MLSTM_CHUNK), dtype=bool))

    def step(carry, inp):
        c_mat, n_vec, m = carry
        q_, k_, v_, li_, lf_ = inp
        bcum = jnp.cumsum(lf_, axis=-1)
        dmat = jnp.where(tri, bcum[..., :, None] - bcum[..., None, :] + li_[..., None, :], -jnp.inf)
        inter = bcum + m[..., None]
        m_t = jnp.maximum(inter, jnp.max(dmat, axis=-1))
        s = jnp.einsum('bhld,bhsd->bhls', q_, k_) * jnp.exp(dmat - m_t[..., None])
        scale = jnp.exp(inter - m_t)
        num = (jnp.einsum('bhls,bhsd->bhld', s, v_)
               + scale[..., None] * jnp.einsum('bhvk,bhlk->bhlv', c_mat, q_))
        den = jnp.sum(s, axis=-1) + scale * jnp.einsum('bhk,bhlk->bhl', n_vec, q_)
        h = num / jnp.maximum(jnp.abs(den), jnp.exp(-m_t))[..., None]
        g = bcum[..., -1]
        wlog = g[..., None] - bcum + li_
        m_new = jnp.maximum(g + m, jnp.max(wlog, axis=-1))
        wexp = jnp.exp(wlog - m_new[..., None])
        decay = jnp.exp(g + m - m_new)
        c_mat = decay[..., None, None] * c_mat + jnp.einsum('bhsv,bhsk->bhvk', v_ * wexp[..., None], k_)
        n_vec = decay[..., None] * n_vec + jnp.einsum('bhs,bhsk->bhk', wexp, k_)
        return (c_mat, n_vec, m_new), h

    init = (jnp.zeros((bsz, MLSTM_HEADS, MLSTM_HEAD, MLSTM_HEAD), jnp.float32),
            jnp.zeros((bsz, MLSTM_HEADS, MLSTM_HEAD), jnp.float32),
            jnp.zeros((bsz, MLSTM_HEADS), jnp.float32))
    _, hc = lax.scan(step, init, (qc, kc, vc, lic, lfc))
    h = hc.transpose(1, 0, 3, 2, 4).reshape(bsz, t, MLSTM_HEADS, MLSTM_HEAD)
    if reverse:
        h = jnp.flip(h, axis=1)
    return h


def odd_mixer(h, w_in, conv_w, conv_b, w_q, w_k, w_v, w_ig, b_ig, w_fg, b_fg, skip, norm_w, w_out):
    bsz, t, _ = h.shape
    p = h @ w_in
    xm = p[..., :MLSTM_INNER]
    z = p[..., MLSTM_INNER:]
    xc = jax.nn.silu(depthwise_conv(xm, conv_w, conv_b, MLSTM_CONV_LEFT))
    q = block_diag(xc, w_q)
    k = block_diag(xc, w_k)
    v = block_diag(xm, w_v)
    qkv = _f32(jnp.concatenate([q, k, v], axis=-1))
    hs = []
    for d, rev in enumerate((False, True)):
        li = qkv @ _f32(w_ig[d]) + _f32(b_ig[d])
        lf = jax.nn.log_sigmoid(qkv @ _f32(w_fg[d]) + _f32(b_fg[d]))
        hs.append(mlstm_chunkwise(q, k, v, li, lf, rev))
    hsum = hs[0] + hs[1]
    mean = jnp.mean(hsum, axis=-1, keepdims=True)
    var = jnp.mean(jnp.square(hsum - mean), axis=-1, keepdims=True)
    hn = ((hsum - mean) * lax.rsqrt(var + MLSTM_LN_EPS)).reshape(bsz, t, MLSTM_INNER) * _f32(norm_w)
    out = (hn + _f32(skip) * _f32(xc)) * jax.nn.silu(_f32(z))
    return out.astype(h.dtype) @ w_out


def conv_ffn(h, w_up, conv_w, conv_b, w_down):
    u = h @ w_up
    val = u[..., :D_FF]
    gate = depthwise_conv(u[..., D_FF:], conv_w, conv_b, FFN_CONV // 2)
    return (jax.nn.gelu(gate) * val) @ w_down


def trunk(x, even_params, odd_params, ffn_params, norm_mix, norm_ffn, norm_final):
    for layer in range(DEPTH):
        h = rms_norm(x, norm_mix[layer])
        if layer % 2 == 0:
            x = x + even_mixer(h, *[p[layer // 2] for p in even_params])
        else:
            x = x + odd_mixer(h, *[p[layer // 2] for p in odd_params])
        h = rms_norm(x, norm_ffn[layer])
        x = x + conv_ffn(h, *[p[layer] for p in ffn_params])
    return rms_norm(x, norm_final)


def setup_inputs(seed: int = 0) -> dict:
    key = jax.random.key(seed)
    keys = iter(jax.random.split(key, 64))

    def normal(shape, scale):
        return scale * jax.random.normal(next(keys), shape, jnp.float32)

    def uniform(shape, lo, hi):
        return jax.random.uniform(next(keys), shape, jnp.float32, lo, hi)

    e, o, l = N_EVEN, N_ODD, DEPTH
    u_lam = uniform((e, 2, LRU_WIDTH), 0.9, 0.999) ** (1.0 / LRU_C)
    lam = jnp.log(u_lam) - jnp.log1p(-u_lam)
    fg_bias = jnp.linspace(3.0, 6.0, MLSTM_HEADS, dtype=jnp.float32) + normal((o, 2, MLSTM_HEADS), 0.1)
    return {
        'x_prompt': normal((BATCH, SEQ, D_MODEL), 1.0),
        'x_sample': normal((DEC_BATCH, DEC_SEQ, D_MODEL), 1.0),
        'ev_w_in': normal((e, D_MODEL, EVEN_IN), D_MODEL ** -0.5),
        'ev_lru_conv_w': normal((e, LRU_CONV, LRU_WIDTH), LRU_CONV ** -0.5),
        'ev_lru_conv_b': normal((e, LRU_WIDTH), 0.01),
        'ev_lru_wa': normal((e, 2, LRU_BLOCKS, LRU_BLOCK, LRU_BLOCK), LRU_BLOCK ** -0.5),
        'ev_lru_ba': normal((e, 2, LRU_WIDTH), 0.01),
        'ev_lru_wx': normal((e, 2, LRU_BLOCKS, LRU_BLOCK, LRU_BLOCK), LRU_BLOCK ** -0.5),
        'ev_lru_bx': normal((e, 2, LRU_WIDTH), 0.01),
        'ev_lru_lambda': lam,
        'ev_rw_mu': uniform((e, RWKV_COLS), 0.0, 1.0),
        'ev_rw_w0': uniform((e, 2, RWKV_WIDTH), -4.0, 2.0),
        'ev_rw_w_up': normal((e, 2, RWKV_W_RANK, RWKV_WIDTH), 0.1 * RWKV_W_RANK ** -0.5),
        'ev_rw_a0': normal((e, 2, RWKV_WIDTH), 0.1),
        'ev_rw_a_up': normal((e, 2, RWKV_A_RANK, RWKV_WIDTH), 0.1 * RWKV_A_RANK ** -0.5),
        'ev_rw_g_up': normal((e, RWKV_G_RANK, RWKV_WIDTH), RWKV_G_RANK ** -0.5),
        'ev_rw_k_k': 0.85 + normal((e, RWKV_WIDTH), 0.02),
        'ev_rw_k_a': 1.0 + normal((e, RWKV_WIDTH), 0.02),
        'ev_rw_r_k': normal((e, RWKV_HEADS, RWKV_HEAD), 0.1),
        'ev_rw_ln_w': 1.0 + normal((e, RWKV_WIDTH), 0.01),
        'ev_rw_ln_b': normal((e, RWKV_WIDTH), 0.01),
        'ev_w_out': normal((e, D_MODEL, D_MODEL), D_MODEL ** -0.5),
        'od_w_in': normal((o, D_MODEL, ODD_IN), D_MODEL ** -0.5),
        'od_conv_w': normal((o, MLSTM_CONV, MLSTM_INNER), MLSTM_CONV ** -0.5),
        'od_conv_b': normal((o, MLSTM_INNER), 0.01),
        'od_w_q': normal((o, MLSTM_NBLK, MLSTM_QK_BLOCK, MLSTM_QK_BLOCK), MLSTM_QK_BLOCK ** -0.5),
        'od_w_k': normal((o, MLSTM_NBLK, MLSTM_QK_BLOCK, MLSTM_QK_BLOCK), MLSTM_QK_BLOCK ** -0.5),
        'od_w_v': normal((o, MLSTM_NBLK, MLSTM_QK_BLOCK, MLSTM_QK_BLOCK), MLSTM_QK_BLOCK ** -0.5),
        'od_w_ig': normal((o, 2, 3 * MLSTM_INNER, MLSTM_HEADS), 0.1 * (3 * MLSTM_INNER) ** -0.5),
        'od_b_ig': normal((o, 2, MLSTM_HEADS), 0.1),
        'od_w_fg': normal((o, 2, 3 * MLSTM_INNER, MLSTM_HEADS), 0.1 * (3 * MLSTM_INNER) ** -0.5),
        'od_b_fg': fg_bias,
        'od_skip': 1.0 + normal((o, MLSTM_INNER), 0.01),
        'od_norm_w': 1.0 + normal((o, MLSTM_INNER), 0.01),
        'od_w_out': normal((o, MLSTM_INNER, D_MODEL), MLSTM_INNER ** -0.5),
        'ffn_w_up': normal((l, D_MODEL, 2 * D_FF), D_MODEL ** -0.5),
        'ffn_conv_w': normal((l, FFN_CONV, D_FF), FFN_CONV ** -0.5),
        'ffn_conv_b': normal((l, D_FF), 0.01),
        'ffn_w_down': normal((l, D_FF, D_MODEL), D_FF ** -0.5),
        'norm_mix': 1.0 + normal((l, D_MODEL), 0.01),
        'norm_ffn': 1.0 + normal((l, D_MODEL), 0.01),
        'norm_final': 1.0 + normal((D_MODEL,), 0.01),
    }


def reference(x_prompt, x_sample, ev_w_in, ev_lru_conv_w, ev_lru_conv_b, ev_lru_wa, ev_lru_ba,
              ev_lru_wx, ev_lru_bx, ev_lru_lambda, ev_rw_mu, ev_rw_w0, ev_rw_w_up, ev_rw_a0,
              ev_rw_a_up, ev_rw_g_up, ev_rw_k_k, ev_rw_k_a, ev_rw_r_k, ev_rw_ln_w, ev_rw_ln_b,
              ev_w_out, od_w_in, od_conv_w, od_conv_b, od_w_q, od_w_k, od_w_v, od_w_ig, od_b_ig,
              od_w_fg, od_b_fg, od_skip, od_norm_w, od_w_out, ffn_w_up, ffn_conv_w, ffn_conv_b,
              ffn_w_down, norm_mix, norm_ffn, norm_final):
    even_params = (ev_w_in, ev_lru_conv_w, ev_lru_conv_b, ev_lru_wa, ev_lru_ba, ev_lru_wx, ev_lru_bx,
                   ev_lru_lambda, ev_rw_mu, ev_rw_w0, ev_rw_w_up, ev_rw_a0, ev_rw_a_up, ev_rw_g_up,
                   ev_rw_k_k, ev_rw_k_a, ev_rw_r_k, ev_rw_ln_w, ev_rw_ln_b, ev_w_out)
    odd_params = (od_w_in, od_conv_w, od_conv_b, od_w_q, od_w_k, od_w_v, od_w_ig, od_b_ig,
                  od_w_fg, od_b_fg, od_skip, od_norm_w, od_w_out)
    ffn_params = (ffn_w_up, ffn_conv_w, ffn_conv_b, ffn_w_down)
    y_prompt = trunk(x_prompt, even_params, odd_params, ffn_params, norm_mix, norm_ffn, norm_final)
    y_sample = trunk(x_sample, even_params, odd_params, ffn_params, norm_mix, norm_ffn, norm_final)
    return (y_prompt, y_sample)
```

```python
import functools
import math

import jax
import jax.numpy as jnp
from jax import lax
from jax.experimental import pallas as pl
from jax.experimental.pallas import tpu as pltpu

_F32 = jnp.float32
_BF16 = jnp.bfloat16

_RMS_EPS = 1e-6
_LRU_C = 8.0
_RW_HEAD = 64
_RW_DECAY_SCALE = math.exp(-0.5)
_RW_GN_EPS = 64e-5
_ML_HEADS = 4
_ML_CHUNK = 128
_ML_LN_EPS = 1e-5
_ML_BLK = 256

_SUBLANES = 8
_VMEM_LIMIT = 56 * 1024 * 1024
_ROW_TILE = 512
_LRU_TILE = 512
_RW_TILE = 256
_RW_CHUNK = 64


def _cparams(n_axes):
    return pltpu.CompilerParams(dimension_semantics=("arbitrary",) * n_axes,
                                vmem_limit_bytes=_VMEM_LIMIT)


def _dot(a, b):
    return jnp.dot(a.astype(_BF16), b.astype(_BF16), preferred_element_type=_F32)


def _dot_nt(a, b):
    return lax.dot_general(a.astype(_BF16), b.astype(_BF16), (((1,), (1,)), ((), ())),
                           preferred_element_type=_F32)


def _dot_tn(a, b):
    return lax.dot_general(a.astype(_BF16), b.astype(_BF16), (((0,), (0,)), ((), ())),
                           preferred_element_type=_F32)


def _split3(x):
    hi = x.astype(_BF16)
    r1 = x - hi.astype(_F32)
    mid = r1.astype(_BF16)
    lo = (r1 - mid.astype(_F32)).astype(_BF16)
    return hi, mid, lo


def _dot01_left(m01, x):
    hi, mid, lo = _split3(x)
    f = lambda p: jnp.dot(m01, p, preferred_element_type=_F32)
    return (f(lo) + f(mid)) + f(hi)


def _dot01_right(x, m01):
    hi, mid, lo = _split3(x)
    f = lambda p: jnp.dot(p, m01, preferred_element_type=_F32)
    return (f(lo) + f(mid)) + f(hi)


def _sigmoid(x):
    return jax.nn.sigmoid(x)


def _softplus(x):
    return jnp.maximum(x, 0.0) + jnp.log1p(jnp.exp(-jnp.abs(x)))


def _gelu_tanh(x):
    c = math.sqrt(2.0 / math.pi)
    return x * (0.5 * (1.0 + jnp.tanh(c * (x + 0.044715 * (x * x * x)))))


def _rms(x, g):
    ms = jnp.mean(x * x, axis=-1, keepdims=True)
    return x * lax.rsqrt(ms + _RMS_EPS) * g


def _ext_rows(main_ref, prev_ref, next_ref, ti, nt):
    prev = jnp.where(ti == 0, 0.0, prev_ref[...])
    nxt = jnp.where(ti == nt - 1, 0.0, next_ref[...])
    return jnp.concatenate([prev, main_ref[...], nxt], axis=0)


def _shifted(ext, k, rows):
    if k == 0:
        return ext[_SUBLANES:_SUBLANES + rows]
    return pltpu.roll(ext, k % ext.shape[0], axis=0)[_SUBLANES:_SUBLANES + rows]


def _time_specs(width, rows, nt, t_len, reverse, col=0):
    per = rows // _SUBLANES
    last = t_len // _SUBLANES - 1

    def tidx(i):
        return (nt - 1 - i) if reverse else i

    main = pl.BlockSpec((None, rows, width), lambda b, i: (b, tidx(i), col))
    prev = pl.BlockSpec((None, _SUBLANES, width),
                        lambda b, i: (b, jnp.maximum(tidx(i) * per - 1, 0), col))
    nxt = pl.BlockSpec((None, _SUBLANES, width),
                       lambda b, i: (b, jnp.minimum((tidx(i) + 1) * per, last), col))
    return main, prev, nxt


def _const_spec(shape):
    nd = len(shape)
    return pl.BlockSpec(shape, lambda *_: (0,) * nd)


def _norm_proj_kernel(x_ref, g_ref, w_ref, *o_refs):
    h = _rms(x_ref[...], g_ref[...])
    res = _dot(h, w_ref[...])
    off = 0
    for o in o_refs:
        wdt = o.shape[-1]
        o[...] = res[:, off:off + wdt]
        off += wdt


def _norm_proj(x, g, w, splits, tm):
    b, t, d = x.shape
    n = w.shape[1]
    assert sum(splits) == n
    nt = t // tm
    row = lambda wd: pl.BlockSpec((None, tm, wd), lambda bi, i: (bi, i, 0))
    return pl.pallas_call(
        _norm_proj_kernel,
        grid=(b, nt),
        in_specs=[row(d), _const_spec((1, d)), _const_spec((d, n))],
        out_specs=[row(s) for s in splits],
        out_shape=[jax.ShapeDtypeStruct((b, t, s), _F32) for s in splits],
        compiler_params=_cparams(2),
        name="norm_proj",
    )(x, g, w)


def _lru_kernel(*refs, reverse, finalize, tt, nt):
    x_ref, xp_ref, xn_ref, cw_ref, cb_ref, wg_ref, bg_ref, lam_ref = refs[:8]
    if finalize:
        hf_ref, gate_ref, o_ref, a_s, u_s, h_s, carry = refs[8:]
    else:
        o_ref, a_s, u_s, carry = refs[8:]
        h_s = o_ref
    i = pl.program_id(1)
    ti = (nt - 1 - i) if reverse else i
    width = x_ref.shape[-1]

    ext = _ext_rows(x_ref, xp_ref, xn_ref, ti, nt)
    cw = cw_ref[...]
    xc = cb_ref[...] + _shifted(ext, 2, tt) * cw[0:1]
    xc = xc + _shifted(ext, 1, tt) * cw[1:2]
    xc = xc + _shifted(ext, 0, tt) * cw[2:3]
    xc = xc + _shifted(ext, -1, tt) * cw[3:4]

    gates = _dot(xc, wg_ref[...]) + bg_ref[...]
    r = _sigmoid(gates[:, :width])
    ig = _sigmoid(gates[:, width:])
    log_a = (-_LRU_C) * r * _softplus(-lam_ref[...])
    th = jnp.tanh(log_a)
    one_minus_a2 = (-2.0 * th) / (1.0 - th)
    a_s[...] = jnp.exp(log_a)
    u_s[...] = jnp.sqrt(one_minus_a2) * (ig * xc)

    @pl.when(i == 0)
    def _():
        carry[...] = jnp.zeros_like(carry)

    ng = tt // _SUBLANES
    sub = lax.broadcasted_iota(jnp.int32, (_SUBLANES, width), 0)

    def body(s, h):
        g = (ng - 1 - s) if reverse else s
        off = pl.multiple_of(g * _SUBLANES, _SUBLANES)
        out8 = jnp.zeros((_SUBLANES, width), _F32)
        order = range(_SUBLANES - 1, -1, -1) if reverse else range(_SUBLANES)
        for j in order:
            h = a_s[pl.ds(off + j, 1), :] * h + u_s[pl.ds(off + j, 1), :]
            out8 = jnp.where(sub == j, h, out8)
        h_s[pl.ds(off, _SUBLANES), :] = out8
        return h

    carry[...] = lax.fori_loop(0, ng, body, carry[...])

    if finalize:
        o_ref[...] = (hf_ref[...] + h_s[...]) * _gelu_tanh(gate_ref[...])


def _lru_pass(p_lru, conv_w, conv_b, wg, bg, lam, hf, reverse, tt):
    b, t, w2 = p_lru.shape
    w = w2 // 2
    nt = t // tt
    finalize = hf is not None
    main, prev, nxt = _time_specs(w, tt, nt, t, reverse)
    tidx = (lambda i: nt - 1 - i) if reverse else (lambda i: i)
    tile = lambda col: pl.BlockSpec((None, tt, w), lambda bi, i: (bi, tidx(i), col))
    in_specs = [main, prev, nxt, _const_spec(conv_w.shape), _const_spec(conv_b.shape),
                _const_spec(wg.shape), _const_spec(bg.shape), _const_spec(lam.shape)]
    args = [p_lru, p_lru, p_lru, conv_w, conv_b, wg, bg, lam]
    scratch = [pltpu.VMEM((tt, w), _F32), pltpu.VMEM((tt, w), _F32)]
    if finalize:
        in_specs += [tile(0), tile(1)]
        args += [hf, p_lru]
        scratch += [pltpu.VMEM((tt, w), _F32)]
    scratch += [pltpu.VMEM((1, w), _F32)]
    return pl.pallas_call(
        functools.partial(_lru_kernel, reverse=reverse, finalize=finalize, tt=tt, nt=nt),
        grid=(b, nt),
        in_specs=in_specs,
        out_specs=tile(0),
        out_shape=jax.ShapeDtypeStruct((b, t, w), _F32),
        scratch_shapes=scratch,
        compiler_params=_cparams(2),
        name="lru_bwd" if reverse else "lru_fwd",
    )(*args)


def _rwkv_kernel(*refs, reverse, finalize, tt, nt):
    (rw_ref, rwp_ref, rwn_ref, mu_ref, kk_ref, ka_ref, w0_ref, a0_ref, wup_ref, aup_ref,
     ones_ref) = refs[:11]
    if finalize:
        yf_ref, rk_ref, lnw_ref, lnb_ref, gup_ref, o_ref, ht_s = refs[11:]
    else:
        o_ref, ht_s = refs[11:]
    i = pl.program_id(1)
    ti = (nt - 1 - i) if reverse else i
    c = ht_s.shape[0]
    lc = _RW_CHUNK
    nheads = c // _RW_HEAD
    ones_bd = ones_ref[...]

    def segsum(x):
        return _dot01_right(x, ones_bd)

    ext = _ext_rows(rw_ref, rwp_ref, rwn_ref, ti, nt)
    x = _shifted(ext, 0, tt)
    rws = x + mu_ref[...] * (0.5 * (_shifted(ext, 1, tt) + _shifted(ext, -1, tt)) - x)
    r = rws[:, 0:c]
    k = rws[:, c:2 * c]
    v = rws[:, 2 * c:3 * c]
    wad = rws[:, 3 * c:3 * c + 128]
    kk = k * kk_ref[...]
    kk = kk / jnp.maximum(jnp.sqrt(segsum(kk * kk)), 1e-12)
    lw = (-_RW_DECAY_SCALE) * _sigmoid(w0_ref[...] + _dot(jnp.tanh(wad), wup_ref[...]))
    a = _sigmoid(a0_ref[...] + _dot(wad, aup_ref[...]))
    kt = k * (1.0 + (a - 1.0) * ka_ref[...])
    bvec = a * kk

    row = lax.broadcasted_iota(jnp.int32, (c, c), 0)
    col = lax.broadcasted_iota(jnp.int32, (c, c), 1)
    shift = _RW_HEAD.bit_length() - 1
    same = (row >> shift) == (col >> shift)
    trow = lax.broadcasted_iota(jnp.int32, (lc, c), 0)
    tcol = lax.broadcasted_iota(jnp.int32, (lc, c), 1) & (lc - 1)
    m_incl = (tcol >= trow) if reverse else (tcol <= trow)
    m_strict = (tcol > trow) if reverse else (tcol < trow)
    srow = lax.broadcasted_iota(jnp.int32, (lc, lc), 0)
    scol = lax.broadcasted_iota(jnp.int32, (lc, lc), 1)
    tri01 = ((scol >= srow) if reverse else (scol <= srow)).astype(_BF16)

    def stack(xv):
        xb = xv.astype(_BF16)
        return jnp.where(same, jnp.concatenate([xb] * nheads, axis=0), jnp.zeros((), _BF16))

    @pl.when(i == 0)
    def _():
        ht_s[...] = jnp.zeros_like(ht_s)

    ht = ht_s[...]
    nchunk = tt // lc
    ys = [None] * nchunk
    for ci in (range(nchunk - 1, -1, -1) if reverse else range(nchunk)):
        sl = slice(ci * lc, (ci + 1) * lc)
        r_c, lw_c, k_c, v_c, kk_c, b_c = r[sl], lw[sl], kt[sl], v[sl], kk[sl], bvec[sl]
        cum = _dot01_left(tri01, lw_c)
        clast = cum[0:1] if reverse else cum[lc - 1:lc]
        rg = r_c * jnp.exp(cum)
        kkg = kk_c * jnp.exp(cum - lw_c)
        einv = jnp.exp(-cum)
        eend = jnp.exp(clast - cum)
        lhs = jnp.concatenate([kkg, rg], axis=0)
        rhs = jnp.concatenate([stack(k_c * einv), stack(b_c * einv)], axis=0)
        amat = _dot_nt(lhs, rhs)
        hl = nheads * lc
        a_kk_k = jnp.where(m_strict, amat[:lc, :hl], 0.0)
        a_kk_b = jnp.where(m_strict, amat[:lc, hl:], 0.0)
        a_r_k = jnp.where(m_incl, amat[lc:, :hl], 0.0)
        a_r_b = jnp.where(m_incl, amat[lc:, hl:], 0.0)
        p0 = _dot_nt(lhs, ht)
        av = _dot(jnp.concatenate([a_kk_k, a_r_k], axis=0), stack(v_c))
        rhs_u = p0[:lc] + av[:lc]
        aj = -a_kk_b
        u = rhs_u + _dot(aj, stack(rhs_u))
        for _ in range(lc.bit_length() - 2):
            aj = _dot(aj, stack(aj))
            u = u + _dot(aj, stack(u))
        ys[ci] = p0[lc:] + av[lc:] - _dot(a_r_b, stack(u))
        upd = _dot_tn(jnp.concatenate([v_c, u], axis=0),
                      jnp.concatenate([k_c * eend, -(b_c * eend)], axis=0))
        ht = ht * jnp.exp(clast) + jnp.where(same, upd, 0.0)
    ht_s[...] = ht
    y = jnp.concatenate(ys, axis=0)

    if finalize:
        y = yf_ref[...] + y
        inv_n = 1.0 / _RW_HEAD
        mean = segsum(y) * inv_n
        d = y - mean
        var = segsum(d * d) * inv_n
        yn = d * lax.rsqrt(var + _RW_GN_EPS) * lnw_ref[...] + lnb_ref[...]
        bonus = segsum(r * k * rk_ref[...]) * v
        gd = rws[:, 3 * c + 128:3 * c + 256]
        g = _dot(_sigmoid(gd), gup_ref[...])
        o_ref[...] = (yn + bonus) * g
    else:
        o_ref[...] = y


def _rwkv_pass(p_rw, prm, d, yf, reverse, tt):
    b, t, cols = p_rw.shape
    c = prm["k_k"].shape[-1]
    nt = t // tt
    finalize = yf is not None
    main, prev, nxt = _time_specs(cols, tt, nt, t, reverse)
    tidx = (lambda i: nt - 1 - i) if reverse else (lambda i: i)
    tile = pl.BlockSpec((None, tt, c), lambda bi, i: (bi, tidx(i), 0))
    consts = [prm["mu"], prm["k_k"], prm["k_a"], prm["w0"][d], prm["a0"][d], prm["w_up"][d],
              prm["a_up"][d], prm["ones_bd"]]
    args = [p_rw, p_rw, p_rw] + consts
    in_specs = [main, prev, nxt] + [_const_spec(z.shape) for z in consts]
    if finalize:
        extra = [prm["r_k"], prm["ln_w"], prm["ln_b"], prm["g_up"]]
        args += [yf] + extra
        in_specs += [tile] + [_const_spec(z.shape) for z in extra]
    return pl.pallas_call(
        functools.partial(_rwkv_kernel, reverse=reverse, finalize=finalize, tt=tt, nt=nt),
        grid=(b, nt),
        in_specs=in_specs,
        out_specs=tile,
        out_shape=jax.ShapeDtypeStruct((b, t, c), _F32),
        scratch_shapes=[pltpu.VMEM((c, c), _F32)],
        compiler_params=_cparams(2),
        name="rwkv_bwd" if reverse else "rwkv_fwd",
    )(*args)


def _out_proj2_kernel(x_ref, a_ref, b_ref, wa_ref, wb_ref, o_ref):
    o_ref[...] = x_ref[...] + _dot(a_ref[...], wa_ref[...]) + _dot(b_ref[...], wb_ref[...])


def _out_proj2(x, a, bm, wa, wb, tm):
    b, t, d = x.shape
    nt = t // tm
    row = lambda wd: pl.BlockSpec((None, tm, wd), lambda bi, i: (bi, i, 0))
    return pl.pallas_call(
        _out_proj2_kernel,
        grid=(b, nt),
        in_specs=[row(d), row(a.shape[-1]), row(bm.shape[-1]), _const_spec(wa.shape),
                  _const_spec(wb.shape)],
        out_specs=row(d),
        out_shape=jax.ShapeDtypeStruct((b, t, d), _F32),
        compiler_params=_cparams(2),
        name="even_out_proj",
    )(x, a, bm, wa, wb)


def _ffn_kernel(*refs, tm, nt, nff, final_norm):
    x_ref, xp_ref, xn_ref, g_ref, wv_ref, wgt_ref, cw_ref, cb_ref, wd_ref = refs[:9]
    if final_norm:
        gf_ref, o_ref, h_s, acc_s = refs[9:]
    else:
        o_ref, h_s, acc_s = refs[9:]
    i = pl.program_id(1)
    cidx = pl.program_id(2)

    @pl.when(cidx == 0)
    def _():
        g = g_ref[...]
        prev = jnp.where(i == 0, 0.0, _rms(xp_ref[...], g))
        nxt = jnp.where(i == nt - 1, 0.0, _rms(xn_ref[...], g))
        h_s[...] = jnp.concatenate([prev, _rms(x_ref[...], g), nxt], axis=0)
        acc_s[...] = jnp.zeros_like(acc_s)

    ug = _dot(h_s[...], wgt_ref[...])
    cw = cw_ref[...]
    gate = cb_ref[...] + _shifted(ug, 1, tm) * cw[0:1]
    gate = gate + _shifted(ug, 0, tm) * cw[1:2]
    gate = gate + _shifted(ug, -1, tm) * cw[2:3]
    val = _dot(h_s[pl.ds(_SUBLANES, tm), :], wv_ref[...])
    acc_s[...] += _dot(_gelu_tanh(gate) * val, wd_ref[...])

    @pl.when(cidx == nff - 1)
    def _():
        y = x_ref[...] + acc_s[...]
        if final_norm:
            y = _rms(y, gf_ref[...])
        o_ref[...] = y


def _ffn(x, g, w_up, conv_w, conv_b, w_down, g_final, tm, ck):
    b, t, d = x.shape
    dff = w_down.shape[0]
    nt = t // tm
    nff = dff // ck
    final_norm = g_final is not None
    per = tm // _SUBLANES
    last = t // _SUBLANES - 1
    main = pl.BlockSpec((None, tm, d), lambda bi, i, c: (bi, i, 0))
    prev = pl.BlockSpec((None, _SUBLANES, d), lambda bi, i, c: (bi, jnp.maximum(i * per - 1, 0), 0))
    nxt = pl.BlockSpec((None, _SUBLANES, d),
                       lambda bi, i, c: (bi, jnp.minimum((i + 1) * per, last), 0))
    in_specs = [main, prev, nxt, _const_spec((1, d)),
                pl.BlockSpec((d, ck), lambda bi, i, c: (0, c)),
                pl.BlockSpec((d, ck), lambda bi, i, c: (0, nff + c)),
                pl.BlockSpec((conv_w.shape[0], ck), lambda bi, i, c: (0, c)),
                pl.BlockSpec((1, ck), lambda bi, i, c: (0, c)),
                pl.BlockSpec((ck, d), lambda bi, i, c: (c, 0))]
    args = [x, x, x, g, w_up, w_up, conv_w, conv_b, w_down]
    if final_norm:
        in_specs.append(_const_spec((1, d)))
        args.append(g_final)
    return pl.pallas_call(
        functools.partial(_ffn_kernel, tm=tm, nt=nt, nff=nff, final_norm=final_norm),
        grid=(b, nt, nff),
        in_specs=in_specs,
        out_specs=main,
        out_shape=jax.ShapeDtypeStruct((b, t, d), _F32),
        scratch_shapes=[pltpu.VMEM((tm + 2 * _SUBLANES, d), _F32), pltpu.VMEM((tm, d), _F32)],
        compiler_params=_cparams(3),
        name="conv_ffn",
    )(*args)


def _mlstm_prep_kernel(xm_ref, xp_ref, xn_ref, cw_ref, cb_ref, wq_ref, wk_ref, wv_ref, wg_ref,
                       bg_ref, q_ref, k_ref, v_ref, xc_ref, gt_ref, *, tt, nt):
    i = pl.program_id(1)
    inner = xm_ref.shape[-1]
    ext = _ext_rows(xm_ref, xp_ref, xn_ref, i, nt)
    cw = cw_ref[...]
    xm = _shifted(ext, 0, tt)
    z = cb_ref[...] + _shifted(ext, 2, tt) * cw[0:1]
    z = z + _shifted(ext, 1, tt) * cw[1:2]
    z = z + xm * cw[2:3]
    z = z + _shifted(ext, -1, tt) * cw[3:4]
    xc = z * _sigmoid(z)
    xc_ref[...] = xc
    for j in range(inner // _ML_BLK):
        sl = slice(j * _ML_BLK, (j + 1) * _ML_BLK)
        q_ref[:, sl] = _dot(xc[:, sl], wq_ref[j])
        k_ref[:, sl] = _dot(xc[:, sl], wk_ref[j])
        v_ref[:, sl] = _dot(xm[:, sl], wv_ref[j])
    gt_ref[...] = (_dot(q_ref[...], wg_ref[0:inner]) + _dot(k_ref[...], wg_ref[inner:2 * inner])
                   + _dot(v_ref[...], wg_ref[2 * inner:3 * inner]) + bg_ref[...])


def _mlstm_prep(xm, prm, tt):
    b, t, inner = xm.shape
    nt = t // tt
    main, prev, nxt = _time_specs(inner, tt, nt, t, False)
    consts = [prm["conv_w"], prm["conv_b"], prm["wq"], prm["wk"], prm["wv"], prm["wg"], prm["bg"]]
    row = lambda wd: pl.BlockSpec((None, tt, wd), lambda bi, i: (bi, i, 0))
    big = jax.ShapeDtypeStruct((b, t, inner), _F32)
    return pl.pallas_call(
        functools.partial(_mlstm_prep_kernel, tt=tt, nt=nt),
        grid=(b, nt),
        in_specs=[main, prev, nxt] + [_const_spec(z.shape) for z in consts],
        out_specs=[row(inner)] * 4 + [row(128)],
        out_shape=[big, big, big, big, jax.ShapeDtypeStruct((b, t, 128), _F32)],
        compiler_params=_cparams(2),
        name="mlstm_prep",
    )(xm, xm, xm, *consts)


def _mlstm_kernel(q_ref, k_ref, v_ref, gt_ref, o_ref, c_s, n_s, m_s, *, reverse, d):
    i = pl.program_id(1)
    lc = _ML_CHUNK
    dh = c_s.shape[-1]

    @pl.when(i == 0)
    def _():
        c_s[...] = jnp.zeros_like(c_s)
        n_s[...] = jnp.zeros_like(n_s)
        m_s[...] = jnp.zeros_like(m_s)

    gates = gt_ref[...]
    lane = lax.broadcasted_iota(jnp.int32, (lc, 128), 1)
    row = lax.broadcasted_iota(jnp.int32, (lc, lc), 0)
    col = lax.broadcasted_iota(jnp.int32, (lc, lc), 1)
    tri = (col >= row) if reverse else (col <= row)
    tri01 = tri.astype(_BF16)
    qscale = dh ** -0.5

    def lane_col(j):
        cval = jnp.sum(jnp.where(lane == j, gates, 0.0), axis=-1, keepdims=True)
        return jnp.broadcast_to(cval, (lc, 128))

    for h in range(_ML_HEADS):
        sl = slice(h * dh, (h + 1) * dh)
        li = lane_col(d * _ML_HEADS + h)
        lf = -_softplus(-lane_col(2 * _ML_HEADS + d * _ML_HEADS + h))
        bcum = _dot01_left(tri01, lf)
        bcum_row = bcum.T
        li_row = li.T
        m_prev = m_s[h]
        dmat = jnp.where(tri, bcum - bcum_row + li_row, -jnp.inf)
        inter = bcum + m_prev
        m_t = jnp.maximum(inter, jnp.max(dmat, axis=-1, keepdims=True))
        q = q_ref[:, sl] * qscale
        k = k_ref[:, sl]
        v = v_ref[:, sl]
        s = _dot_nt(q, k) * jnp.exp(dmat - m_t)
        scale = jnp.exp(inter - m_t)[:, 0:1]
        cmat = c_s[h]
        nvec = n_s[h]
        num = _dot(s, v) + scale * _dot_nt(q, cmat)
        den = jnp.sum(s, axis=-1, keepdims=True) + scale * jnp.sum(q * nvec, axis=-1, keepdims=True)
        o_ref[:, sl] = num / jnp.maximum(jnp.abs(den), jnp.exp(-m_t[:, 0:1]))
        g = bcum[0:1] if reverse else bcum[lc - 1:lc]
        wlog = g - bcum + li
        m_new = jnp.maximum(g + m_prev, jnp.max(wlog, axis=0, keepdims=True))
        wexp = jnp.exp(wlog - m_new)[:, 0:1]
        decay = jnp.exp(g + m_prev - m_new)[:, 0:1]
        c_s[h] = decay * cmat + _dot_tn(v * wexp, k)
        n_s[h] = decay * nvec + jnp.sum(k * wexp, axis=0, keepdims=True)
        m_s[h] = m_new


def _mlstm_pass(q, k, v, gt, d, reverse):
    b, t, inner = q.shape
    lc = _ML_CHUNK
    nc = t // lc
    dh = inner // _ML_HEADS
    tidx = (lambda i: nc - 1 - i) if reverse else (lambda i: i)
    row = lambda wd: pl.BlockSpec((None, lc, wd), lambda bi, i: (bi, tidx(i), 0))
    return pl.pallas_call(
        functools.partial(_mlstm_kernel, reverse=reverse, d=d),
        grid=(b, nc),
        in_specs=[row(inner)] * 3 + [row(128)],
        out_specs=row(inner),
        out_shape=jax.ShapeDtypeStruct((b, t, inner), _F32),
        scratch_shapes=[pltpu.VMEM((_ML_HEADS, dh, dh), _F32), pltpu.VMEM((_ML_HEADS, 1, dh), _F32),
                        pltpu.VMEM((_ML_HEADS, 1, 128), _F32)],
        compiler_params=_cparams(2),
        name="mlstm_bwd" if reverse else "mlstm_fwd",
    )(q, k, v, gt)


def _mlstm_out_kernel(x_ref, hf_ref, hb_ref, xc_ref, z_ref, nw_ref, skip_ref, wo_ref, o_ref):
    hs = hf_ref[...] + hb_ref[...]
    inner = hs.shape[-1]
    dh = inner // _ML_HEADS
    parts = []
    for h in range(_ML_HEADS):
        hh = hs[:, h * dh:(h + 1) * dh]
        mean = jnp.mean(hh, axis=-1, keepdims=True)
        dlt = hh - mean
        var = jnp.mean(dlt * dlt, axis=-1, keepdims=True)
        parts.append(dlt * lax.rsqrt(var + _ML_LN_EPS))
    hn = jnp.concatenate(parts, axis=-1) * nw_ref[...]
    z = z_ref[...]
    out = (hn + skip_ref[...] * xc_ref[...]) * (z * _sigmoid(z))
    o_ref[...] = x_ref[...] + _dot(out, wo_ref[...])


def _mlstm_out(x, hf, hb, xc, z, norm_w, skip, w_out, tm):
    b, t, d = x.shape
    inner = hf.shape[-1]
    nt = t // tm
    row = lambda wd: pl.BlockSpec((None, tm, wd), lambda bi, i: (bi, i, 0))
    return pl.pallas_call(
        _mlstm_out_kernel,
        grid=(b, nt),
        in_specs=[row(d)] + [row(inner)] * 4 + [_const_spec(norm_w.shape), _const_spec(skip.shape),
                                                _const_spec(w_out.shape)],
        out_specs=row(d),
        out_shape=jax.ShapeDtypeStruct((b, t, d), _F32),
        compiler_params=_cparams(2),
        name="mlstm_out_proj",
    )(x, hf, hb, xc, z, norm_w, skip, w_out)


def _block_diag_dense(w):
    nb, bi, bo = w.shape
    eye = jnp.eye(nb, dtype=w.dtype)
    return jnp.einsum("nio,nm->nimo", w, eye).reshape(nb * bi, nb * bo)


def _block_diag_blocks(w, blk):
    nb, bi, bo = w.shape
    per = blk // bi
    return jax.vmap(_block_diag_dense)(w.reshape(nb // per, per, bi, bo))


def _row(v):
    return v.reshape(1, -1).astype(_F32)


def _tile(t_len, want):
    return min(want, t_len)


def _even_layer(x, li, p):
    (w_in, conv_w, conv_b, wa, ba, wx, bx, lam, mu, w0, w_up, a0, a_up, g_up, k_k, k_a, r_k,
     ln_w, ln_b, w_out) = [z[li] for z in p["even"]]
    t = x.shape[1]
    width = lam.shape[-1]
    c = k_k.shape[-1]
    p_lru, p_rw = _norm_proj(x, _row(p["norm_mix"][2 * li]), w_in.astype(_BF16),
                             (2 * width, w_in.shape[1] - 2 * width), _tile(t, 256))

    tt = _tile(t, _LRU_TILE)
    h_dir = None
    for d, rev in enumerate((False, True)):
        wg = jnp.concatenate([_block_diag_dense(wa[d]), _block_diag_dense(wx[d])], axis=1).astype(_BF16)
        bg = jnp.concatenate([ba[d], bx[d]]).reshape(1, -1)
        h_dir = _lru_pass(p_lru, conv_w, _row(conv_b), wg, bg, _row(lam[d]), h_dir, rev, tt)
    lru_out = h_dir

    zeros = jnp.zeros((_RW_HEAD, c), _F32)
    head = jnp.arange(c) // _RW_HEAD
    prm = {
        "mu": _row(mu), "k_k": _row(k_k), "k_a": _row(k_a),
        "w0": [_row(w0[d]) for d in range(2)], "a0": [_row(a0[d]) for d in range(2)],
        "w_up": [jnp.concatenate([w_up[d], zeros], axis=0).astype(_BF16) for d in range(2)],
        "a_up": [jnp.concatenate([zeros, a_up[d]], axis=0).astype(_BF16) for d in range(2)],
        "ones_bd": (head[:, None] == head[None, :]).astype(_BF16),
        "r_k": _row(r_k), "ln_w": _row(ln_w), "ln_b": _row(ln_b), "g_up": g_up.astype(_BF16),
    }
    tt = _tile(t, _RW_TILE)
    y_dir = None
    for d, rev in enumerate((False, True)):
        y_dir = _rwkv_pass(p_rw, prm, d, y_dir, rev, tt)
    rw_out = y_dir

    wo = w_out.astype(_BF16)
    return _out_proj2(x, lru_out, rw_out, wo[:width], wo[width:], _tile(t, _ROW_TILE))


def _odd_layer(x, li, p):
    (w_in, conv_w, conv_b, w_q, w_k, w_v, w_ig, b_ig, w_fg, b_fg, skip, norm_w,
     w_out) = [z[li] for z in p["odd"]]
    t = x.shape[1]
    inner = skip.shape[-1]
    xm, z = _norm_proj(x, _row(p["norm_mix"][2 * li + 1]), w_in.astype(_BF16), (inner, inner),
                       _tile(t, 256))
    nh = b_ig.shape[-1]
    wg = jnp.concatenate([w_ig[0], w_ig[1], w_fg[0], w_fg[1],
                          jnp.zeros((3 * inner, 128 - 4 * nh), _F32)], axis=1).astype(_BF16)
    bg = jnp.concatenate([b_ig[0], b_ig[1], b_fg[0], b_fg[1], jnp.zeros((128 - 4 * nh,), _F32)])
    prm = {
        "conv_w": conv_w, "conv_b": _row(conv_b),
        "wq": _block_diag_blocks(w_q, _ML_BLK).astype(_BF16),
        "wk": _block_diag_blocks(w_k, _ML_BLK).astype(_BF16),
        "wv": _block_diag_blocks(w_v, _ML_BLK).astype(_BF16),
        "wg": wg, "bg": _row(bg),
    }
    q, k, v, xc, gt = _mlstm_prep(xm, prm, _tile(t, 256))
    hf = _mlstm_pass(q, k, v, gt, 0, False)
    hb = _mlstm_pass(q, k, v, gt, 1, True)
    return _mlstm_out(x, hf, hb, xc, z, _row(norm_w), _row(skip), w_out.astype(_BF16),
                      _tile(t, 256))


def _trunk(x, p):
    depth = p["norm_mix"].shape[0]
    t = x.shape[1]
    for layer in range(depth):
        if layer % 2 == 0:
            x = _even_layer(x, layer // 2, p)
        else:
            x = _odd_layer(x, layer // 2, p)
        w_up, conv_w, conv_b, w_down = [z[layer] for z in p["ffn"]]
        dff = w_down.shape[0]
        ck = 256 if dff % 256 == 0 else dff
        g_final = _row(p["norm_final"]) if layer == depth - 1 else None
        x = _ffn(x, _row(p["norm_ffn"][layer]), w_up.astype(_BF16), conv_w, _row(conv_b),
                 w_down.astype(_BF16), g_final, _tile(t, _ROW_TILE), ck)
    return x


def kernel(x_prompt, x_sample, ev_w_in, ev_lru_conv_w, ev_lru_conv_b, ev_lru_wa, ev_lru_ba, ev_lru_wx, ev_lru_bx, ev_lru_lambda, ev_rw_mu, ev_rw_w0, ev_rw_w_up, ev_rw_a0, ev_rw_a_up, ev_rw_g_up, ev_rw_k_k, ev_rw_k_a, ev_rw_r_k, ev_rw_ln_w, ev_rw_ln_b, ev_w_out, od_w_in, od_conv_w, od_conv_b, od_w_q, od_w_k, od_w_v, od_w_ig, od_b_ig, od_w_fg, od_b_fg, od_skip, od_norm_w, od_w_out, ffn_w_up, ffn_conv_w, ffn_conv_b, ffn_w_down, norm_mix, norm_ffn, norm_final):
    p = {
        "even": (ev_w_in, ev_lru_conv_w, ev_lru_conv_b, ev_lru_wa, ev_lru_ba, ev_lru_wx, ev_lru_bx,
                 ev_lru_lambda, ev_rw_mu, ev_rw_w0, ev_rw_w_up, ev_rw_a0, ev_rw_a_up, ev_rw_g_up,
                 ev_rw_k_k, ev_rw_k_a, ev_rw_r_k, ev_rw_ln_w, ev_rw_ln_b, ev_w_out),
        "odd": (od_w_in, od_conv_w, od_conv_b, od_w_q, od_w_k, od_w_v, od_w_ig, od_b_ig, od_w_fg,
                od_b_fg, od_skip, od_norm_w, od_w_out),
        "ffn": (ffn_w_up, ffn_conv_w, ffn_conv_b, ffn_w_down),
        "norm_mix": norm_mix, "norm_ffn": norm_ffn, "norm_final": norm_final,
    }
    nb = x_prompt.shape[0]
    if x_prompt.shape[1:] == x_sample.shape[1:]:
        y = _trunk(jnp.concatenate([x_prompt, x_sample], axis=0), p)
        return (y[:nb], y[nb:])
    return (_trunk(x_prompt, p), _trunk(x_sample, p))
```

```python
import functools
import math

import jax
import jax.numpy as jnp
from jax import lax
from jax.experimental import pallas as pl
from jax.experimental.pallas import tpu as pltpu

_F32 = jnp.float32
_BF16 = jnp.bfloat16

_RMS_EPS = 1e-6
_LRU_C = 8.0
_RW_HEAD = 64
_RW_DECAY_SCALE = math.exp(-0.5)
_RW_GN_EPS = 64e-5
_ML_HEADS = 4
_ML_LN_EPS = 1e-5
_ML_BLK = 256

_SUBLANES = 8
_LANES = 128
_VMEM_LIMIT = 56 * 1024 * 1024
_ROW_TILE = 512
_NORM_TILE = 256
_LRU_TILE = 512
_RW_TILE = 256
_RW_CHUNK = 64
_RW_GROUP = 256
_ML_ROW_TILE = 256
_ML_TILE = 256
_FFN_COLS = 256


def _cparams(n_axes):
    return pltpu.CompilerParams(dimension_semantics=("arbitrary",) * n_axes,
                                vmem_limit_bytes=_VMEM_LIMIT)


def _dot(a, b):
    return jnp.dot(a.astype(_BF16), b.astype(_BF16), preferred_element_type=_F32)


def _dot_nt(a, b):
    return lax.dot_general(a.astype(_BF16), b.astype(_BF16), (((1,), (1,)), ((), ())),
                           preferred_element_type=_F32)


def _dot_tn(a, b):
    return lax.dot_general(a.astype(_BF16), b.astype(_BF16), (((0,), (0,)), ((), ())),
                           preferred_element_type=_F32)


def _split3(x):
    hi = x.astype(_BF16)
    r1 = x - hi.astype(_F32)
    mid = r1.astype(_BF16)
    lo = (r1 - mid.astype(_F32)).astype(_BF16)
    return hi, mid, lo


def _dot01_left(m01, x):
    hi, mid, lo = _split3(x)
    f = lambda p: jnp.dot(m01, p, preferred_element_type=_F32)
    return (f(lo) + f(mid)) + f(hi)


def _dot01_right(x, m01):
    hi, mid, lo = _split3(x)
    f = lambda p: jnp.dot(p, m01, preferred_element_type=_F32)
    return (f(lo) + f(mid)) + f(hi)


def _sigmoid(x):
    return jax.nn.sigmoid(x)


def _softplus(x):
    return jnp.maximum(x, 0.0) + jnp.log1p(jnp.exp(-jnp.abs(x)))


def _gelu_tanh(x):
    c = math.sqrt(2.0 / math.pi)
    return x * (0.5 * (1.0 + jnp.tanh(c * (x + 0.044715 * (x * x * x)))))


def _rms(x, g):
    ms = jnp.mean(x * x, axis=-1, keepdims=True)
    return x * lax.rsqrt(ms + _RMS_EPS) * g


def _ext_rows(main_ref, prev_ref, next_ref, ti, nt):
    prev = jnp.where(ti == 0, 0.0, prev_ref[...])
    nxt = jnp.where(ti == nt - 1, 0.0, next_ref[...])
    return jnp.concatenate([prev, main_ref[...], nxt], axis=0)


def _shifted(ext, k, rows, halo=_SUBLANES):
    if k == 0:
        return ext[halo:halo + rows]
    return pltpu.roll(ext, k % ext.shape[0], axis=0)[halo:halo + rows]


def _time_specs(width, rows, nt, t_len, reverse, col=0):
    per = rows // _SUBLANES
    last = t_len // _SUBLANES - 1

    def tidx(i):
        return (nt - 1 - i) if reverse else i

    main = pl.BlockSpec((None, rows, width), lambda b, i: (b, tidx(i), col))
    prev = pl.BlockSpec((None, _SUBLANES, width),
                        lambda b, i: (b, jnp.maximum(tidx(i) * per - 1, 0), col))
    nxt = pl.BlockSpec((None, _SUBLANES, width),
                       lambda b, i: (b, jnp.minimum((tidx(i) + 1) * per, last), col))
    return main, prev, nxt


def _const_spec(shape):
    nd = len(shape)
    return pl.BlockSpec(shape, lambda *_: (0,) * nd)


def _resident_spec(shape):
    nd = len(shape)
    return pl.BlockSpec(shape, lambda *_: (0,) * nd, pipeline_mode=pl.Buffered(1))


def _norm_proj_kernel(x_ref, g_ref, w_ref, *o_refs):
    h = _rms(x_ref[...], g_ref[...])
    res = _dot(h, w_ref[...])
    off = 0
    for o in o_refs:
        wdt = o.shape[-1]
        o[...] = res[:, off:off + wdt].astype(o.dtype)
        off += wdt


def _norm_proj(x, g, w, splits, dtypes, tm):
    b, t, d = x.shape
    n = w.shape[1]
    assert sum(splits) == n
    nt = t // tm
    row = lambda wd: pl.BlockSpec((None, tm, wd), lambda bi, i: (bi, i, 0))
    return pl.pallas_call(
        _norm_proj_kernel,
        grid=(b, nt),
        in_specs=[row(d), _const_spec((1, d)), _const_spec((d, n))],
        out_specs=[row(s) for s in splits],
        out_shape=[jax.ShapeDtypeStruct((b, t, s), dt) for s, dt in zip(splits, dtypes)],
        compiler_params=_cparams(2),
        name="norm_proj",
    )(x, g, w)


def _lru_kernel(*refs, reverse, finalize, tt, nt):
    x_ref, xp_ref, xn_ref, cw_ref, cb_ref, wg_ref, bg_ref, lam_ref = refs[:8]
    if finalize:
        hf_ref, gate_ref, o_ref, a_s, u_s, h_s, carry = refs[8:]
    else:
        o_ref, a_s, u_s, carry = refs[8:]
        h_s = o_ref
    i = pl.program_id(1)
    ti = (nt - 1 - i) if reverse else i
    width = x_ref.shape[-1]

    ext = _ext_rows(x_ref, xp_ref, xn_ref, ti, nt)
    cw = cw_ref[...]
    xc = cb_ref[...] + _shifted(ext, 2, tt) * cw[0:1]
    xc = xc + _shifted(ext, 1, tt) * cw[1:2]
    xc = xc + _shifted(ext, 0, tt) * cw[2:3]
    xc = xc + _shifted(ext, -1, tt) * cw[3:4]

    gates = _dot(xc, wg_ref[...]) + bg_ref[...]
    r = _sigmoid(gates[:, :width])
    ig = _sigmoid(gates[:, width:])
    log_a = (-_LRU_C) * r * _softplus(-lam_ref[...])
    th = jnp.tanh(log_a)
    one_minus_a2 = (-2.0 * th) / (1.0 - th)
    a_s[...] = jnp.exp(log_a)
    u_s[...] = jnp.sqrt(one_minus_a2) * (ig * xc)

    @pl.when(i == 0)
    def _():
        carry[...] = jnp.zeros_like(carry)

    ng = tt // _SUBLANES
    sub = lax.broadcasted_iota(jnp.int32, (_SUBLANES, width), 0)

    def body(s, h):
        g = (ng - 1 - s) if reverse else s
        off = pl.multiple_of(g * _SUBLANES, _SUBLANES)
        out8 = jnp.zeros((_SUBLANES, width), _F32)
        order = range(_SUBLANES - 1, -1, -1) if reverse else range(_SUBLANES)
        for j in order:
            h = a_s[pl.ds(off + j, 1), :] * h + u_s[pl.ds(off + j, 1), :]
            out8 = jnp.where(sub == j, h, out8)
        h_s[pl.ds(off, _SUBLANES), :] = out8
        return h

    carry[...] = lax.fori_loop(0, ng, body, carry[...])

    if finalize:
        o_ref[...] = (hf_ref[...] + h_s[...]) * _gelu_tanh(gate_ref[...])


def _lru_pass(p_lru, conv_w, conv_b, wg, bg, lam, hf, reverse, tt):
    b, t, w2 = p_lru.shape
    w = w2 // 2
    nt = t // tt
    finalize = hf is not None
    main, prev, nxt = _time_specs(w, tt, nt, t, reverse)
    tidx = (lambda i: nt - 1 - i) if reverse else (lambda i: i)
    tile = lambda col: pl.BlockSpec((None, tt, w), lambda bi, i: (bi, tidx(i), col))
    in_specs = [main, prev, nxt, _const_spec(conv_w.shape), _const_spec(conv_b.shape),
                _const_spec(wg.shape), _const_spec(bg.shape), _const_spec(lam.shape)]
    args = [p_lru, p_lru, p_lru, conv_w, conv_b, wg, bg, lam]
    scratch = [pltpu.VMEM((tt, w), _F32), pltpu.VMEM((tt, w), _F32)]
    if finalize:
        in_specs += [tile(0), tile(1)]
        args += [hf, p_lru]
        scratch += [pltpu.VMEM((tt, w), _F32)]
    scratch += [pltpu.VMEM((1, w), _F32)]
    return pl.pallas_call(
        functools.partial(_lru_kernel, reverse=reverse, finalize=finalize, tt=tt, nt=nt),
        grid=(b, nt),
        in_specs=in_specs,
        out_specs=tile(0),
        out_shape=jax.ShapeDtypeStruct((b, t, w), _F32),
        scratch_shapes=scratch,
        compiler_params=_cparams(2),
        name="lru_bwd" if reverse else "lru_fwd",
    )(*args)


def _rwkv_kernel(*refs, reverse, finalize, tt, nt):
    (rw_ref, rwp_ref, rwn_ref, mu_ref, kk_ref, ka_ref, w0_ref, a0_ref, wup_ref, aup_ref,
     onesr_ref, onese_ref) = refs[:12]
    if finalize:
        yf_ref, rk_ref, lnw_ref, lnb_ref, gup_ref, o_ref, ht_s = refs[12:]
    else:
        o_ref, ht_s = refs[12:]
    i = pl.program_id(1)
    ti = (nt - 1 - i) if reverse else i
    ngrp, gw, _ = ht_s.shape
    c = ngrp * gw
    lc = _RW_CHUNK
    hpg = gw // _RW_HEAD
    ones_r = onesr_ref[...]
    ones_e = onese_ref[...]

    def segsum(x):
        return _dot01_right(_dot01_right(x, ones_r), ones_e)

    ext = _ext_rows(rw_ref, rwp_ref, rwn_ref, ti, nt)
    x = _shifted(ext, 0, tt)
    rws = x + mu_ref[...] * (0.5 * (_shifted(ext, 1, tt) + _shifted(ext, -1, tt)) - x)
    r = rws[:, 0:c]
    k = rws[:, c:2 * c]
    v = rws[:, 2 * c:3 * c]
    wad = rws[:, 3 * c:3 * c + _LANES]
    kk = k * kk_ref[...]
    kk = kk / jnp.maximum(jnp.sqrt(segsum(kk * kk)), 1e-12)
    lw = (-_RW_DECAY_SCALE) * _sigmoid(w0_ref[...] + _dot(jnp.tanh(wad), wup_ref[...]))
    a = _sigmoid(a0_ref[...] + _dot(wad, aup_ref[...]))
    kt = k * (1.0 + (a - 1.0) * ka_ref[...])
    bvec = a * kk

    shift = _RW_HEAD.bit_length() - 1
    row = lax.broadcasted_iota(jnp.int32, (gw, gw), 0)
    col = lax.broadcasted_iota(jnp.int32, (gw, gw), 1)
    same = (row >> shift) == (col >> shift)
    trow = lax.broadcasted_iota(jnp.int32, (lc, gw), 0)
    tcol = lax.broadcasted_iota(jnp.int32, (lc, gw), 1) & (lc - 1)
    m_incl = (tcol >= trow) if reverse else (tcol <= trow)
    m_strict = (tcol > trow) if reverse else (tcol < trow)
    eye_cat = jnp.where(tcol == trow, 1.0, 0.0)
    srow = lax.broadcasted_iota(jnp.int32, (lc, lc), 0)
    scol = lax.broadcasted_iota(jnp.int32, (lc, lc), 1)
    tri01 = ((scol >= srow) if reverse else (scol <= srow)).astype(_BF16)

    def stack(xv):
        xb = xv.astype(_BF16)
        return jnp.where(same, jnp.concatenate([xb] * hpg, axis=0), jnp.zeros((), _BF16))

    @pl.when(i == 0)
    def _():
        ht_s[...] = jnp.zeros_like(ht_s)

    nchunk = tt // lc
    chunks = list(range(nchunk - 1, -1, -1) if reverse else range(nchunk))
    units = [(ci, g) for ci in chunks for g in range(ngrp)]

    pre = {}
    for ci in chunks:
        sl = slice(ci * lc, (ci + 1) * lc)
        r_c, lw_c, k_c, v_c, kk_c, b_c = r[sl], lw[sl], kt[sl], v[sl], kk[sl], bvec[sl]
        cum = _dot01_left(tri01, lw_c)
        clast = cum[0:1] if reverse else cum[lc - 1:lc]
        einv = jnp.exp(-cum)
        eend = jnp.exp(clast - cum)
        pre[ci] = dict(gamma=jnp.exp(clast), rg=r_c * jnp.exp(cum), kkg=kk_c * jnp.exp(cum - lw_c),
                       kd=k_c * einv, bd=b_c * einv, kend=k_c * eend, bend=b_c * eend, v=v_c)

    st = {}
    for ci, g in units:
        ls = slice(g * gw, (g + 1) * gw)
        pc = pre[ci]
        lhs = jnp.concatenate([pc["kkg"][:, ls], pc["rg"][:, ls]], axis=0)
        rhs = jnp.concatenate([stack(pc["kd"][:, ls]), stack(pc["bd"][:, ls])], axis=0)
        amat = _dot_nt(lhs, rhs)
        nj = -jnp.where(m_strict, amat[:lc, gw:], 0.0)
        st[ci, g] = dict(a_kk_k=jnp.where(m_strict, amat[:lc, :gw], 0.0),
                         a_r_k=jnp.where(m_incl, amat[lc:, :gw], 0.0),
                         a_r_b=jnp.where(m_incl, amat[lc:, gw:], 0.0),
                         nj=nj, tinv=eye_cat + nj)
    for key in units:
        s = st[key]
        s["nj"] = _dot(s["nj"], stack(s["nj"]))
    for _ in range(lc.bit_length() - 3):
        for key in units:
            s = st[key]
            pn = _dot(jnp.concatenate([s["tinv"], s["nj"]], axis=0), stack(s["nj"]))
            s["tinv"] = s["tinv"] + pn[:lc]
            s["nj"] = pn[lc:]
    for key in units:
        s = st[key]
        s["tinv"] = s["tinv"] + _dot(s["tinv"], stack(s["nj"]))
    for ci, g in units:
        ls = slice(g * gw, (g + 1) * gw)
        s = st[ci, g]
        s["av"] = _dot(jnp.concatenate([s["a_kk_k"], s["a_r_k"]], axis=0), stack(pre[ci]["v"][:, ls]))
    for ci, g in units:
        ls = slice(g * gw, (g + 1) * gw)
        s = st[ci, g]
        tk = _dot(s["tinv"], jnp.concatenate([stack(pre[ci]["kkg"][:, ls]), stack(s["av"][:lc])], axis=1))
        s["kkgp"], s["uv"] = tk[:, :gw], tk[:, gw:]
    for ci, g in units:
        ls = slice(g * gw, (g + 1) * gw)
        s = st[ci, g]
        pc = pre[ci]
        s["kb"] = jnp.where(same, _dot_tn(s["kkgp"], pc["bend"][:, ls]), 0.0)
        s["gmat"] = jnp.where(same, _dot_tn(jnp.concatenate([pc["v"][:, ls], -s["uv"]], axis=0),
                                            jnp.concatenate([pc["kend"][:, ls], pc["bend"][:, ls]], axis=0)),
                              0.0)

    hts = [ht_s[g] for g in range(ngrp)]
    ys = [None] * nchunk
    for ci in chunks:
        y_parts = []
        for g in range(ngrp):
            ls = slice(g * gw, (g + 1) * gw)
            s = st[ci, g]
            ht = hts[g]
            pu = _dot_nt(jnp.concatenate([s["kkgp"], pre[ci]["rg"][:, ls]], axis=0), ht)
            u = pu[:lc] + s["uv"]
            y_parts.append(pu[lc:] + s["av"][lc:] - _dot(s["a_r_b"], stack(u)))
            hts[g] = ht * pre[ci]["gamma"][:, ls] - _dot(ht, s["kb"]) + s["gmat"]
        ys[ci] = jnp.concatenate(y_parts, axis=1)
    for g in range(ngrp):
        ht_s[g] = hts[g]
    y = jnp.concatenate(ys, axis=0)

    if finalize:
        y = yf_ref[...] + y
        inv_n = 1.0 / _RW_HEAD
        mean = segsum(y) * inv_n
        d = y - mean
        var = segsum(d * d) * inv_n
        yn = d * lax.rsqrt(var + _RW_GN_EPS) * lnw_ref[...] + lnb_ref[...]
        bonus = segsum(r * k * rk_ref[...]) * v
        gd = rws[:, 3 * c + _LANES:3 * c + 2 * _LANES]
        g = _dot(_sigmoid(gd), gup_ref[...])
        o_ref[...] = (yn + bonus) * g
    else:
        o_ref[...] = y


def _rwkv_pass(p_rw, prm, d, yf, reverse, tt):
    b, t, cols = p_rw.shape
    c = prm["k_k"].shape[-1]
    nt = t // tt
    finalize = yf is not None
    main, prev, nxt = _time_specs(cols, tt, nt, t, reverse)
    tidx = (lambda i: nt - 1 - i) if reverse else (lambda i: i)
    tile = pl.BlockSpec((None, tt, c), lambda bi, i: (bi, tidx(i), 0))
    consts = [prm["mu"], prm["k_k"], prm["k_a"], prm["w0"][d], prm["a0"][d], prm["w_up"][d],
              prm["a_up"][d], prm["ones_r"], prm["ones_e"]]
    args = [p_rw, p_rw, p_rw] + consts
    in_specs = [main, prev, nxt] + [_const_spec(z.shape) for z in consts]
    if finalize:
        extra = [prm["r_k"], prm["ln_w"], prm["ln_b"], prm["g_up"]]
        args += [yf] + extra
        in_specs += [tile] + [_const_spec(z.shape) for z in extra]
    return pl.pallas_call(
        functools.partial(_rwkv_kernel, reverse=reverse, finalize=finalize, tt=tt, nt=nt),
        grid=(b, nt),
        in_specs=in_specs,
        out_specs=tile,
        out_shape=jax.ShapeDtypeStruct((b, t, c), _F32),
        scratch_shapes=[pltpu.VMEM((c // _RW_GROUP, _RW_GROUP, _RW_GROUP), _F32)],
        compiler_params=_cparams(2),
        name="rwkv_bwd" if reverse else "rwkv_fwd",
    )(*args)


def _out_proj2_kernel(x_ref, a_ref, b_ref, wa_ref, wb_ref, o_ref):
    o_ref[...] = x_ref[...] + _dot(a_ref[...], wa_ref[...]) + _dot(b_ref[...], wb_ref[...])


def _out_proj2(x, a, bm, wa, wb, tm):
    b, t, d = x.shape
    nt = t // tm
    row = lambda wd: pl.BlockSpec((None, tm, wd), lambda bi, i: (bi, i, 0))
    return pl.pallas_call(
        _out_proj2_kernel,
        grid=(b, nt),
        in_specs=[row(d), row(a.shape[-1]), row(bm.shape[-1]), _const_spec(wa.shape),
                  _const_spec(wb.shape)],
        out_specs=row(d),
        out_shape=jax.ShapeDtypeStruct((b, t, d), _F32),
        compiler_params=_cparams(2),
        name="even_out_proj",
    )(x, a, bm, wa, wb)


def _ffn_kernel(*refs, tm, nt, ck, final_norm):
    x_ref, xp_ref, xn_ref, g_ref, wup_ref, cw_ref, cb_ref, wd_ref = refs[:8]
    if final_norm:
        gf_ref, o_ref, h_s, act_s = refs[8:]
    else:
        o_ref, h_s, act_s = refs[8:]
    i = pl.program_id(1)
    dff = wd_ref.shape[0]
    halo = 2 * _SUBLANES
    g = g_ref[...]
    zeros = jnp.zeros((_SUBLANES, x_ref.shape[-1]), _F32)
    prev = jnp.where(i == 0, 0.0, _rms(xp_ref[...], g))
    nxt = jnp.where(i == nt - 1, 0.0, _rms(xn_ref[...], g))
    h_s[...] = jnp.concatenate([zeros, prev, _rms(x_ref[...], g), nxt, zeros], axis=0).astype(_BF16)
    hext = h_s[...]
    hmain = h_s[pl.ds(halo, tm), :]
    cw = cw_ref[...]
    cb = cb_ref[...]
    for c in range(dff // ck):
        cs = slice(c * ck, (c + 1) * ck)
        ug = jnp.dot(hext, wup_ref[:, dff + c * ck:dff + (c + 1) * ck], preferred_element_type=_F32)
        gate = cb[:, cs] + _shifted(ug, 1, tm, halo) * cw[0:1, cs]
        gate = gate + _shifted(ug, 0, tm, halo) * cw[1:2, cs]
        gate = gate + _shifted(ug, -1, tm, halo) * cw[2:3, cs]
        val = jnp.dot(hmain, wup_ref[:, cs], preferred_element_type=_F32)
        act_s[:, cs] = (_gelu_tanh(gate) * val).astype(_BF16)
    y = x_ref[...] + jnp.dot(act_s[...], wd_ref[...], preferred_element_type=_F32)
    if final_norm:
        y = _rms(y, gf_ref[...])
    o_ref[...] = y


def _ffn(x, g, w_up, conv_w, conv_b, w_down, g_final, tm, ck):
    b, t, d = x.shape
    dff = w_down.shape[0]
    nt = t // tm
    final_norm = g_final is not None
    main, prev, nxt = _time_specs(d, tm, nt, t, False)
    consts = [g, w_up, conv_w, conv_b, w_down] + ([g_final] if final_norm else [])
    return pl.pallas_call(
        functools.partial(_ffn_kernel, tm=tm, nt=nt, ck=ck, final_norm=final_norm),
        grid=(b, nt),
        in_specs=[main, prev, nxt] + [_resident_spec(z.shape) for z in consts],
        out_specs=main,
        out_shape=jax.ShapeDtypeStruct((b, t, d), _F32),
        scratch_shapes=[pltpu.VMEM((tm + 4 * _SUBLANES, d), _BF16), pltpu.VMEM((tm, dff), _BF16)],
        compiler_params=_cparams(2),
        name="conv_ffn",
    )(x, x, x, *consts)


def _mlstm_prep_kernel(xm_ref, xp_ref, xn_ref, cw_ref, cb_ref, wq_ref, wk_ref, wv_ref, wg_ref,
                       bg_ref, q_ref, k_ref, v_ref, xc_ref, gt_ref, *, tt, nt):
    i = pl.program_id(1)
    inner = xm_ref.shape[-1]
    ext = _ext_rows(xm_ref, xp_ref, xn_ref, i, nt)
    cw = cw_ref[...]
    xm = _shifted(ext, 0, tt)
    z = cb_ref[...] + _shifted(ext, 2, tt) * cw[0:1]
    z = z + _shifted(ext, 1, tt) * cw[1:2]
    z = z + xm * cw[2:3]
    z = z + _shifted(ext, -1, tt) * cw[3:4]
    xc = z * _sigmoid(z)
    xc_ref[...] = xc.astype(xc_ref.dtype)
    qscale = (inner // _ML_HEADS) ** -0.5
    gates = bg_ref[...]
    for j in range(inner // _ML_BLK):
        sl = slice(j * _ML_BLK, (j + 1) * _ML_BLK)
        q = _dot(xc[:, sl], wq_ref[j])
        k = _dot(xc[:, sl], wk_ref[j])
        v = _dot(xm[:, sl], wv_ref[j])
        gates = (gates + _dot(q, wg_ref[sl]) + _dot(k, wg_ref[inner + j * _ML_BLK:inner + (j + 1) * _ML_BLK])
                 + _dot(v, wg_ref[2 * inner + j * _ML_BLK:2 * inner + (j + 1) * _ML_BLK]))
        q_ref[:, sl] = (q * qscale).astype(q_ref.dtype)
        k_ref[:, sl] = k.astype(k_ref.dtype)
        v_ref[:, sl] = v.astype(v_ref.dtype)
    gt_ref[...] = gates


def _mlstm_prep(xm, prm, tt):
    b, t, inner = xm.shape
    nt = t // tt
    main, prev, nxt = _time_specs(inner, tt, nt, t, False)
    consts = [prm["conv_w"], prm["conv_b"], prm["wq"], prm["wk"], prm["wv"], prm["wg"], prm["bg"]]
    row = lambda wd: pl.BlockSpec((None, tt, wd), lambda bi, i: (bi, i, 0))
    big = jax.ShapeDtypeStruct((b, t, inner), _BF16)
    return pl.pallas_call(
        functools.partial(_mlstm_prep_kernel, tt=tt, nt=nt),
        grid=(b, nt),
        in_specs=[main, prev, nxt] + [_const_spec(z.shape) for z in consts],
        out_specs=[row(inner)] * 4 + [row(_LANES)],
        out_shape=[big, big, big, big, jax.ShapeDtypeStruct((b, t, _LANES), _F32)],
        compiler_params=_cparams(2),
        name="mlstm_prep",
    )(xm, xm, xm, *consts)


def _mlstm_kernel(q_ref, k_ref, v_ref, gt_ref, o_ref, c_s, n_s, m_s, *, reverse, d, lc):
    i = pl.program_id(1)
    dh = c_s.shape[-1]

    @pl.when(i == 0)
    def _():
        c_s[...] = jnp.zeros_like(c_s)
        n_s[...] = jnp.zeros_like(n_s)
        m_s[...] = jnp.zeros_like(m_s)

    gates = gt_ref[...]
    lane = lax.broadcasted_iota(jnp.int32, gates.shape, 1)
    row = lax.broadcasted_iota(jnp.int32, (lc, lc), 0)
    col = lax.broadcasted_iota(jnp.int32, (lc, lc), 1)
    tri = (col >= row) if reverse else (col <= row)
    bcum_all = _dot01_left(tri.astype(_BF16), -_softplus(-gates))

    def lane_col(mat, j):
        return jnp.sum(jnp.where(lane == j, mat, 0.0), axis=-1, keepdims=True)

    for h in range(_ML_HEADS):
        sl = slice(h * dh, (h + 1) * dh)
        li = lane_col(gates, d * _ML_HEADS + h)
        bcum = lane_col(bcum_all, 2 * _ML_HEADS + d * _ML_HEADS + h)
        m_prev = m_s[h][:, 0:1]
        src_row = jnp.broadcast_to(li - bcum, (lc, lc)).T
        dmat = jnp.where(tri, bcum + src_row, -jnp.inf)
        inter = bcum + m_prev
        m_t = jnp.maximum(inter, jnp.max(dmat, axis=-1, keepdims=True))
        q = q_ref[:, sl]
        k = k_ref[:, sl]
        v = v_ref[:, sl]
        s = _dot_nt(q, k) * jnp.exp(dmat - m_t)
        scale = jnp.exp(inter - m_t)
        cmat = c_s[h]
        nvec = n_s[h]
        num = _dot(s, v) + scale * _dot_nt(q, cmat)
        den = (jnp.sum(s, axis=-1, keepdims=True)
               + scale * jnp.sum(q.astype(_F32) * nvec, axis=-1, keepdims=True))
        o_ref[:, sl] = (num / jnp.maximum(jnp.abs(den), jnp.exp(-m_t))).astype(o_ref.dtype)
        g = bcum[0:1] if reverse else bcum[lc - 1:lc]
        wlog = g - bcum + li
        m_new = jnp.maximum(g + m_prev, jnp.max(wlog, axis=0, keepdims=True))
        wexp = jnp.exp(wlog - m_new)
        decay = jnp.exp(g + m_prev - m_new)
        c_s[h] = decay * cmat + _dot_tn(v * wexp.astype(v.dtype), k)
        n_s[h] = decay * nvec + jnp.sum(k.astype(_F32) * wexp, axis=0, keepdims=True)
        m_s[h] = jnp.broadcast_to(m_new, m_s.shape[1:])


def _mlstm_pass(q, k, v, gt, d, reverse, lc):
    b, t, inner = q.shape
    nc = t // lc
    dh = inner // _ML_HEADS
    tidx = (lambda i: nc - 1 - i) if reverse else (lambda i: i)
    row = lambda wd: pl.BlockSpec((None, lc, wd), lambda bi, i: (bi, tidx(i), 0))
    return pl.pallas_call(
        functools.partial(_mlstm_kernel, reverse=reverse, d=d, lc=lc),
        grid=(b, nc),
        in_specs=[row(inner)] * 3 + [row(_LANES)],
        out_specs=row(inner),
        out_shape=jax.ShapeDtypeStruct((b, t, inner), _BF16),
        scratch_shapes=[pltpu.VMEM((_ML_HEADS, dh, dh), _F32), pltpu.VMEM((_ML_HEADS, 1, dh), _F32),
                        pltpu.VMEM((_ML_HEADS, 1, _LANES), _F32)],
        compiler_params=_cparams(2),
        name="mlstm_bwd" if reverse else "mlstm_fwd",
    )(q, k, v, gt)


def _mlstm_out_kernel(x_ref, hf_ref, hb_ref, xc_ref, z_ref, nw_ref, skip_ref, wo_ref, o_ref):
    hs = hf_ref[...].astype(_F32) + hb_ref[...].astype(_F32)
    inner = hs.shape[-1]
    dh = inner // _ML_HEADS
    parts = []
    for h in range(_ML_HEADS):
        hh = hs[:, h * dh:(h + 1) * dh]
        mean = jnp.mean(hh, axis=-1, keepdims=True)
        dlt = hh - mean
        var = jnp.mean(dlt * dlt, axis=-1, keepdims=True)
        parts.append(dlt * lax.rsqrt(var + _ML_LN_EPS))
    hn = jnp.concatenate(parts, axis=-1) * nw_ref[...]
    z = z_ref[...].astype(_F32)
    out = (hn + skip_ref[...] * xc_ref[...].astype(_F32)) * (z * _sigmoid(z))
    o_ref[...] = x_ref[...] + _dot(out, wo_ref[...])


def _mlstm_out(x, hf, hb, xc, z, norm_w, skip, w_out, tm):
    b, t, d = x.shape
    inner = hf.shape[-1]
    nt = t // tm
    row = lambda wd: pl.BlockSpec((None, tm, wd), lambda bi, i: (bi, i, 0))
    return pl.pallas_call(
        _mlstm_out_kernel,
        grid=(b, nt),
        in_specs=[row(d)] + [row(inner)] * 4 + [_const_spec(norm_w.shape), _const_spec(skip.shape),
                                                _const_spec(w_out.shape)],
        out_specs=row(d),
        out_shape=jax.ShapeDtypeStruct((b, t, d), _F32),
        compiler_params=_cparams(2),
        name="mlstm_out_proj",
    )(x, hf, hb, xc, z, norm_w, skip, w_out)


def _block_diag_dense(w):
    nb, bi, bo = w.shape
    eye = jnp.eye(nb, dtype=w.dtype)
    return jnp.einsum("nio,nm->nimo", w, eye).reshape(nb * bi, nb * bo)


def _block_diag_blocks(w, blk):
    nb, bi, bo = w.shape
    per = blk // bi
    return jax.vmap(_block_diag_dense)(w.reshape(nb // per, per, bi, bo))


def _row(v):
    return v.reshape(1, -1).astype(_F32)


def _tile(t_len, want):
    return min(want, t_len)


def _even_layer(x, li, p):
    (w_in, conv_w, conv_b, wa, ba, wx, bx, lam, mu, w0, w_up, a0, a_up, g_up, k_k, k_a, r_k,
     ln_w, ln_b, w_out) = [z[li] for z in p["even"]]
    t = x.shape[1]
    width = lam.shape[-1]
    c = k_k.shape[-1]
    p_lru, p_rw = _norm_proj(x, _row(p["norm_mix"][2 * li]), w_in.astype(_BF16),
                             (2 * width, w_in.shape[1] - 2 * width), (_F32, _F32),
                             _tile(t, _NORM_TILE))

    tt = _tile(t, _LRU_TILE)
    h_dir = None
    for d, rev in enumerate((False, True)):
        wg = jnp.concatenate([_block_diag_dense(wa[d]), _block_diag_dense(wx[d])], axis=1).astype(_BF16)
        bg = jnp.concatenate([ba[d], bx[d]]).reshape(1, -1)
        h_dir = _lru_pass(p_lru, conv_w, _row(conv_b), wg, bg, _row(lam[d]), h_dir, rev, tt)
    lru_out = h_dir

    zeros = jnp.zeros((_RW_HEAD, c), _F32)
    head = jnp.arange(c) // _RW_HEAD
    prm = {
        "mu": _row(mu), "k_k": _row(k_k), "k_a": _row(k_a),
        "w0": [_row(w0[d]) for d in range(2)], "a0": [_row(a0[d]) for d in range(2)],
        "w_up": [jnp.concatenate([w_up[d], zeros], axis=0).astype(_BF16) for d in range(2)],
        "a_up": [jnp.concatenate([zeros, a_up[d]], axis=0).astype(_BF16) for d in range(2)],
        "ones_r": (head[:, None] == jnp.arange(_LANES)[None, :]).astype(_BF16),
        "ones_e": (jnp.arange(_LANES)[:, None] == head[None, :]).astype(_BF16),
        "r_k": _row(r_k), "ln_w": _row(ln_w), "ln_b": _row(ln_b), "g_up": g_up.astype(_BF16),
    }
    tt = _tile(t, _RW_TILE)
    y_dir = None
    for d, rev in enumerate((False, True)):
        y_dir = _rwkv_pass(p_rw, prm, d, y_dir, rev, tt)
    rw_out = y_dir

    wo = w_out.astype(_BF16)
    return _out_proj2(x, lru_out, rw_out, wo[:width], wo[width:], _tile(t, _ROW_TILE))


def _odd_layer(x, li, p):
    (w_in, conv_w, conv_b, w_q, w_k, w_v, w_ig, b_ig, w_fg, b_fg, skip, norm_w,
     w_out) = [z[li] for z in p["odd"]]
    t = x.shape[1]
    inner = skip.shape[-1]
    xm, z = _norm_proj(x, _row(p["norm_mix"][2 * li + 1]), w_in.astype(_BF16), (inner, inner),
                       (_F32, _BF16), _tile(t, _NORM_TILE))
    nh = b_ig.shape[-1]
    wg = jnp.concatenate([w_ig[0], w_ig[1], w_fg[0], w_fg[1],
                          jnp.zeros((3 * inner, _LANES - 4 * nh), _F32)], axis=1).astype(_BF16)
    bg = jnp.concatenate([b_ig[0], b_ig[1], b_fg[0], b_fg[1], jnp.zeros((_LANES - 4 * nh,), _F32)])
    prm = {
        "conv_w": conv_w, "conv_b": _row(conv_b),
        "wq": _block_diag_blocks(w_q, _ML_BLK).astype(_BF16),
        "wk": _block_diag_blocks(w_k, _ML_BLK).astype(_BF16),
        "wv": _block_diag_blocks(w_v, _ML_BLK).astype(_BF16),
        "wg": wg, "bg": _row(bg),
    }
    q, k, v, xc, gt = _mlstm_prep(xm, prm, _tile(t, _ML_ROW_TILE))
    lc = _tile(t, _ML_TILE)
    hf = _mlstm_pass(q, k, v, gt, 0, False, lc)
    hb = _mlstm_pass(q, k, v, gt, 1, True, lc)
    return _mlstm_out(x, hf, hb, xc, z, _row(norm_w), _row(skip), w_out.astype(_BF16),
                      _tile(t, _ML_ROW_TILE))


def _trunk(x, p):
    depth = p["norm_mix"].shape[0]
    t = x.shape[1]
    for layer in range(depth):
        if layer % 2 == 0:
            x = _even_layer(x, layer // 2, p)
        else:
            x = _odd_layer(x, layer // 2, p)
        w_up, conv_w, conv_b, w_down = [z[layer] for z in p["ffn"]]
        dff = w_down.shape[0]
        ck = _FFN_COLS if dff % _FFN_COLS == 0 else dff
        g_final = _row(p["norm_final"]) if layer == depth - 1 else None
        x = _ffn(x, _row(p["norm_ffn"][layer]), w_up.astype(_BF16), conv_w, _row(conv_b),
                 w_down.astype(_BF16), g_final, _tile(t, _ROW_TILE), ck)
    return x


def kernel(x_prompt, x_sample, ev_w_in, ev_lru_conv_w, ev_lru_conv_b, ev_lru_wa, ev_lru_ba, ev_lru_wx, ev_lru_bx, ev_lru_lambda, ev_rw_mu, ev_rw_w0, ev_rw_w_up, ev_rw_a0, ev_rw_a_up, ev_rw_g_up, ev_rw_k_k, ev_rw_k_a, ev_rw_r_k, ev_rw_ln_w, ev_rw_ln_b, ev_w_out, od_w_in, od_conv_w, od_conv_b, od_w_q, od_w_k, od_w_v, od_w_ig, od_b_ig, od_w_fg, od_b_fg, od_skip, od_norm_w, od_w_out, ffn_w_up, ffn_conv_w, ffn_conv_b, ffn_w_down, norm_mix, norm_ffn, norm_final):
    p = {
        "even": (ev_w_in, ev_lru_conv_w, ev_lru_conv_b, ev_lru_wa, ev_lru_ba, ev_lru_wx, ev_lru_bx,
                 ev_lru_lambda, ev_rw_mu, ev_rw_w0, ev_rw_w_up, ev_rw_a0, ev_rw_a_up, ev_rw_g_up,
                 ev_rw_k_k, ev_rw_k_a, ev_rw_r_k, ev_rw_ln_w, ev_rw_ln_b, ev_w_out),
        "odd": (od_w_in, od_conv_w, od_conv_b, od_w_q, od_w_k, od_w_v, od_w_ig, od_b_ig, od_w_fg,
                od_b_fg, od_skip, od_norm_w, od_w_out),
        "ffn": (ffn_w_up, ffn_conv_w, ffn_conv_b, ffn_w_down),
        "norm_mix": norm_mix, "norm_ffn": norm_ffn, "norm_final": norm_final,
    }
    return (_trunk(x_prompt, p), _trunk(x_sample, p))
```

```python
import functools
import math

import jax
import jax.numpy as jnp
from jax import lax
from jax.experimental import pallas as pl
from jax.experimental.pallas import tpu as pltpu

_F32 = jnp.float32
_BF16 = jnp.bfloat16

_RMS_EPS = 1e-6
_LRU_C = 8.0
_RW_HEAD = 64
_RW_DECAY_SCALE = math.exp(-0.5)
_RW_GN_EPS = 64e-5
_ML_HEADS = 4
_ML_LN_EPS = 1e-5
_ML_BLK = 256

_SUBLANES = 8
_LANES = 128
_VMEM_LIMIT = 56 * 1024 * 1024
_ROW_TILE = 512
_NORM_TILE = 512
_LRU_TILE = 512
_LRU_SEQS = 2
_RW_TILE = 256
_RW_CHUNK = 64
_RW_GROUP = 256
_ML_ROW_TILE = 256
_ML_OUT_TILE = 512
_ML_TILE = 256
_FFN_COLS = 256


def _cparams(n_axes):
    return pltpu.CompilerParams(dimension_semantics=("arbitrary",) * n_axes,
                                vmem_limit_bytes=_VMEM_LIMIT)


def _dot(a, b):
    return jnp.dot(a.astype(_BF16), b.astype(_BF16), preferred_element_type=_F32)


def _dot_nt(a, b):
    return lax.dot_general(a.astype(_BF16), b.astype(_BF16), (((1,), (1,)), ((), ())),
                           preferred_element_type=_F32)


def _dot_tn(a, b):
    return lax.dot_general(a.astype(_BF16), b.astype(_BF16), (((0,), (0,)), ((), ())),
                           preferred_element_type=_F32)


def _split3(x):
    hi = x.astype(_BF16)
    r1 = x - hi.astype(_F32)
    mid = r1.astype(_BF16)
    lo = (r1 - mid.astype(_F32)).astype(_BF16)
    return hi, mid, lo


def _dot01_left(m01, x):
    hi, mid, lo = _split3(x)
    f = lambda p: jnp.dot(m01, p, preferred_element_type=_F32)
    return (f(lo) + f(mid)) + f(hi)


def _dot01_right(x, m01):
    hi = x.astype(_BF16)
    lo = (x - hi.astype(_F32)).astype(_BF16)
    f = lambda p: jnp.dot(p, m01, preferred_element_type=_F32)
    return f(lo) + f(hi)


def _sigmoid(x):
    return 0.5 * jnp.tanh(0.5 * x) + 0.5


def _softplus(x):
    return jnp.maximum(x, 0.0) + jnp.log1p(jnp.exp(-jnp.abs(x)))


def _gelu_tanh(x):
    c = math.sqrt(2.0 / math.pi)
    return x * (0.5 * (1.0 + jnp.tanh(c * (x + 0.044715 * (x * x * x)))))


def _rms(x, g):
    ms = jnp.mean(x * x, axis=-1, keepdims=True)
    return x * lax.rsqrt(ms + _RMS_EPS) * g


def _halo_shifter(main, prev, nxt, ti, nt):
    rows = main.shape[0]
    prev = jnp.where(ti == 0, 0.0, prev)
    nxt = jnp.where(ti == nt - 1, 0.0, nxt)
    ridx = lax.broadcasted_iota(jnp.int32, main.shape, 0)
    first, second, last = ridx == 0, ridx == 1, ridx == rows - 1

    def shift(k):
        rolled = pltpu.roll(main, k % rows, axis=0)
        if k == 1:
            return jnp.where(first, prev[_SUBLANES - 1:_SUBLANES], rolled)
        if k == 2:
            return jnp.where(first, prev[_SUBLANES - 2:_SUBLANES - 1],
                             jnp.where(second, prev[_SUBLANES - 1:_SUBLANES], rolled))
        assert k == -1
        return jnp.where(last, nxt[0:1], rolled)

    return shift


def _shifted(ext, k, rows, halo=_SUBLANES):
    if k == 0:
        return ext[halo:halo + rows]
    return pltpu.roll(ext, k % ext.shape[0], axis=0)[halo:halo + rows]


def _time_specs(width, rows, nt, t_len, reverse, col=0, bb=None):
    per = rows // _SUBLANES
    last = t_len // _SUBLANES - 1

    def tidx(i):
        return (nt - 1 - i) if reverse else i

    main = pl.BlockSpec((bb, rows, width), lambda b, i: (b, tidx(i), col))
    prev = pl.BlockSpec((bb, _SUBLANES, width),
                        lambda b, i: (b, jnp.maximum(tidx(i) * per - 1, 0), col))
    nxt = pl.BlockSpec((bb, _SUBLANES, width),
                       lambda b, i: (b, jnp.minimum((tidx(i) + 1) * per, last), col))
    return main, prev, nxt


def _const_spec(shape):
    nd = len(shape)
    return pl.BlockSpec(shape, lambda *_: (0,) * nd)


def _resident_spec(shape):
    nd = len(shape)
    return pl.BlockSpec(shape, lambda *_: (0,) * nd, pipeline_mode=pl.Buffered(1))


def _norm_proj_kernel(x_ref, g_ref, w_ref, *o_refs):
    h = _rms(x_ref[...], g_ref[...])
    res = _dot(h, w_ref[...])
    off = 0
    for o in o_refs:
        wdt = o.shape[-1]
        o[...] = res[:, off:off + wdt].astype(o.dtype)
        off += wdt


def _norm_proj(x, g, w, splits, dtypes, tm):
    b, t, d = x.shape
    n = w.shape[1]
    assert sum(splits) == n
    nt = t // tm
    row = lambda wd: pl.BlockSpec((None, tm, wd), lambda bi, i: (bi, i, 0))
    return pl.pallas_call(
        _norm_proj_kernel,
        grid=(b, nt),
        in_specs=[row(d), _const_spec((1, d)), _resident_spec((d, n))],
        out_specs=[row(s) for s in splits],
        out_shape=[jax.ShapeDtypeStruct((b, t, s), dt) for s, dt in zip(splits, dtypes)],
        compiler_params=_cparams(2),
        name="norm_proj",
    )(x, g, w)


def _lru_kernel(*refs, reverse, finalize, tt, nt):
    x_ref, xp_ref, xn_ref, cw_ref, cb_ref, wg_ref, bg_ref, lam_ref = refs[:8]
    if finalize:
        hf_ref, gate_ref, o_ref, a_s, u_s, h_s, carry = refs[8:]
    else:
        o_ref, a_s, u_s, carry = refs[8:]
        h_s = o_ref
    i = pl.program_id(1)
    ti = (nt - 1 - i) if reverse else i
    nseq, _, width = x_ref.shape

    cw = cw_ref[...]
    neg_c_sp = (-_LRU_C) * _softplus(-lam_ref[...])
    for b in range(nseq):
        xb = x_ref[b]
        shift = _halo_shifter(xb, xp_ref[b], xn_ref[b], ti, nt)
        xc = cb_ref[...] + shift(2) * cw[0:1]
        xc = xc + shift(1) * cw[1:2]
        xc = xc + xb * cw[2:3]
        xc = xc + shift(-1) * cw[3:4]
        gates = _dot(xc, wg_ref[...]) + bg_ref[...]
        r = _sigmoid(gates[:, :width])
        ig = _sigmoid(gates[:, width:])
        log_a = r * neg_c_sp
        th = jnp.tanh(log_a)
        a_s[b] = jnp.exp(log_a)
        u_s[b] = jnp.sqrt(-2.0 * th) * lax.rsqrt(1.0 - th) * (ig * xc)

    @pl.when(i == 0)
    def _():
        carry[...] = jnp.zeros_like(carry)

    ng = tt // _SUBLANES
    sub = lax.broadcasted_iota(jnp.int32, (_SUBLANES, width), 0)

    def body(s, hs):
        g = (ng - 1 - s) if reverse else s
        off = pl.multiple_of(g * _SUBLANES, _SUBLANES)
        hs = list(hs)
        outs = [jnp.zeros((_SUBLANES, width), _F32)] * nseq
        order = range(_SUBLANES - 1, -1, -1) if reverse else range(_SUBLANES)
        for j in order:
            for b in range(nseq):
                hs[b] = a_s[b, pl.ds(off + j, 1), :] * hs[b] + u_s[b, pl.ds(off + j, 1), :]
                outs[b] = jnp.where(sub == j, hs[b], outs[b])
        for b in range(nseq):
            h_s[b, pl.ds(off, _SUBLANES), :] = outs[b]
        return tuple(hs)

    hs = lax.fori_loop(0, ng, body, tuple(carry[b] for b in range(nseq)))
    for b in range(nseq):
        carry[b] = hs[b]

    if finalize:
        o_ref[...] = (hf_ref[...] + h_s[...]) * _gelu_tanh(gate_ref[...])


def _lru_pass(p_lru, conv_w, conv_b, wg, bg, lam, hf, reverse, tt):
    b, t, w2 = p_lru.shape
    w = w2 // 2
    nt = t // tt
    nseq = _LRU_SEQS if b % _LRU_SEQS == 0 else 1
    finalize = hf is not None
    main, prev, nxt = _time_specs(w, tt, nt, t, reverse, bb=nseq)
    tidx = (lambda i: nt - 1 - i) if reverse else (lambda i: i)
    tile = lambda col: pl.BlockSpec((nseq, tt, w), lambda bi, i: (bi, tidx(i), col))
    in_specs = [main, prev, nxt, _const_spec(conv_w.shape), _const_spec(conv_b.shape),
                _const_spec(wg.shape), _const_spec(bg.shape), _const_spec(lam.shape)]
    args = [p_lru, p_lru, p_lru, conv_w, conv_b, wg, bg, lam]
    scratch = [pltpu.VMEM((nseq, tt, w), _F32), pltpu.VMEM((nseq, tt, w), _F32)]
    if finalize:
        in_specs += [tile(0), tile(1)]
        args += [hf, p_lru]
        scratch += [pltpu.VMEM((nseq, tt, w), _F32)]
    scratch += [pltpu.VMEM((nseq, 1, w), _F32)]
    return pl.pallas_call(
        functools.partial(_lru_kernel, reverse=reverse, finalize=finalize, tt=tt, nt=nt),
        grid=(b // nseq, nt),
        in_specs=in_specs,
        out_specs=tile(0),
        out_shape=jax.ShapeDtypeStruct((b, t, w), _F32),
        scratch_shapes=scratch,
        compiler_params=_cparams(2),
        name="lru_bwd" if reverse else "lru_fwd",
    )(*args)


def _rwkv_kernel(*refs, reverse, finalize, tt, nt):
    (rw_ref, rwp_ref, rwn_ref, mu_ref, kk_ref, ka_ref, w0_ref, a0_ref, wup_ref, aup_ref,
     onesr_ref, onese_ref) = refs[:12]
    if finalize:
        yf_ref, rk_ref, lnw_ref, lnb_ref, gup_ref, o_ref, ht_s = refs[12:]
    else:
        o_ref, ht_s = refs[12:]
    i = pl.program_id(1)
    ti = (nt - 1 - i) if reverse else i
    ngrp, gw, _ = ht_s.shape
    c = ngrp * gw
    lc = _RW_CHUNK
    hpg = gw // _RW_HEAD
    ones_r = onesr_ref[...]
    ones_e = onese_ref[...]

    def segsum(x):
        return _dot01_right(_dot01_right(x, ones_r), ones_e)

    x = rw_ref[...]
    shift = _halo_shifter(x, rwp_ref[...], rwn_ref[...], ti, nt)
    rws = x + mu_ref[...] * (0.5 * (shift(1) + shift(-1)) - x)
    r = rws[:, 0:c]
    k = rws[:, c:2 * c]
    v = rws[:, 2 * c:3 * c]
    wad = rws[:, 3 * c:3 * c + _LANES]
    kk = k * kk_ref[...]
    kk = kk * jnp.minimum(lax.rsqrt(segsum(kk * kk)), 1e12)
    lw = (-_RW_DECAY_SCALE) * _sigmoid(w0_ref[...] + _dot(jnp.tanh(wad), wup_ref[...]))
    a = _sigmoid(a0_ref[...] + _dot(wad, aup_ref[...]))
    kt = k * (1.0 + (a - 1.0) * ka_ref[...])
    bvec = a * kk

    shift = _RW_HEAD.bit_length() - 1
    row = lax.broadcasted_iota(jnp.int32, (gw, gw), 0)
    col = lax.broadcasted_iota(jnp.int32, (gw, gw), 1)
    same = (row >> shift) == (col >> shift)
    trow = lax.broadcasted_iota(jnp.int32, (lc, gw), 0)
    tcol = lax.broadcasted_iota(jnp.int32, (lc, gw), 1) & (lc - 1)
    m_incl = (tcol >= trow) if reverse else (tcol <= trow)
    m_strict = (tcol > trow) if reverse else (tcol < trow)
    eye_cat = jnp.where(tcol == trow, 1.0, 0.0)
    srow = lax.broadcasted_iota(jnp.int32, (lc, lc), 0)
    scol = lax.broadcasted_iota(jnp.int32, (lc, lc), 1)
    tri01 = ((scol >= srow) if reverse else (scol <= srow)).astype(_BF16)

    def stack(xv):
        xb = xv.astype(_BF16)
        return jnp.where(same, jnp.concatenate([xb] * hpg, axis=0), jnp.zeros((), _BF16))

    @pl.when(i == 0)
    def _():
        ht_s[...] = jnp.zeros_like(ht_s)

    nchunk = tt // lc
    chunks = list(range(nchunk - 1, -1, -1) if reverse else range(nchunk))
    units = [(ci, g) for ci in chunks for g in range(ngrp)]

    pre = {}
    for ci in chunks:
        sl = slice(ci * lc, (ci + 1) * lc)
        r_c, lw_c, k_c, v_c, kk_c, b_c = r[sl], lw[sl], kt[sl], v[sl], kk[sl], bvec[sl]
        cum = _dot01_left(tri01, lw_c)
        clast = cum[0:1] if reverse else cum[lc - 1:lc]
        einv = jnp.exp(-cum)
        eend = jnp.exp(clast - cum)
        pre[ci] = dict(gamma=jnp.exp(clast), rg=r_c * jnp.exp(cum), kkg=kk_c * jnp.exp(cum - lw_c),
                       kd=k_c * einv, bd=b_c * einv, kend=k_c * eend, bend=b_c * eend, v=v_c)

    st = {}
    for ci, g in units:
        ls = slice(g * gw, (g + 1) * gw)
        pc = pre[ci]
        lhs = jnp.concatenate([pc["kkg"][:, ls], pc["rg"][:, ls]], axis=0)
        rhs = jnp.concatenate([stack(pc["kd"][:, ls]), stack(pc["bd"][:, ls])], axis=0)
        amat = _dot_nt(lhs, rhs)
        nj = -jnp.where(m_strict, amat[:lc, gw:], 0.0)
        st[ci, g] = dict(a_kk_k=jnp.where(m_strict, amat[:lc, :gw], 0.0),
                         a_r_k=jnp.where(m_incl, amat[lc:, :gw], 0.0),
                         a_r_b=jnp.where(m_incl, amat[lc:, gw:], 0.0),
                         nj=nj, tinv=eye_cat + nj)
    for key in units:
        s = st[key]
        s["nj"] = _dot(s["nj"], stack(s["nj"]))
    for _ in range(lc.bit_length() - 3):
        for key in units:
            s = st[key]
            pn = _dot(jnp.concatenate([s["tinv"], s["nj"]], axis=0), stack(s["nj"]))
            s["tinv"] = s["tinv"] + pn[:lc]
            s["nj"] = pn[lc:]
    for key in units:
        s = st[key]
        s["tinv"] = s["tinv"] + _dot(s["tinv"], stack(s["nj"]))
    for ci, g in units:
        ls = slice(g * gw, (g + 1) * gw)
        s = st[ci, g]
        s["av"] = _dot(jnp.concatenate([s["a_kk_k"], s["a_r_k"]], axis=0), stack(pre[ci]["v"][:, ls]))
    for ci, g in units:
        ls = slice(g * gw, (g + 1) * gw)
        s = st[ci, g]
        tk = _dot(s["tinv"], jnp.concatenate([stack(pre[ci]["kkg"][:, ls]), stack(s["av"][:lc])], axis=1))
        s["kkgp"], s["uv"] = tk[:, :gw], tk[:, gw:]
    for ci, g in units:
        ls = slice(g * gw, (g + 1) * gw)
        s = st[ci, g]
        pc = pre[ci]
        s["kb"] = jnp.where(same, _dot_tn(s["kkgp"], pc["bend"][:, ls]), 0.0)
        s["gmat"] = jnp.where(same, _dot_tn(jnp.concatenate([pc["v"][:, ls], -s["uv"]], axis=0),
                                            jnp.concatenate([pc["kend"][:, ls], pc["bend"][:, ls]], axis=0)),
                              0.0)

    hts = [ht_s[g] for g in range(ngrp)]
    ys = [None] * nchunk
    for ci in chunks:
        cur = list(hts)
        for g in range(ngrp):
            ls = slice(g * gw, (g + 1) * gw)
            s = st[ci, g]
            hts[g] = cur[g] * pre[ci]["gamma"][:, ls] - _dot(cur[g], s["kb"]) + s["gmat"]
        pus = [_dot_nt(jnp.concatenate([st[ci, g]["kkgp"], pre[ci]["rg"][:, g * gw:(g + 1) * gw]], axis=0),
                       cur[g]) for g in range(ngrp)]
        us = [pus[g][:lc] + st[ci, g]["uv"] for g in range(ngrp)]
        ys[ci] = jnp.concatenate(
            [pus[g][lc:] + st[ci, g]["av"][lc:] - _dot(st[ci, g]["a_r_b"], stack(us[g]))
             for g in range(ngrp)], axis=1)
    for g in range(ngrp):
        ht_s[g] = hts[g]
    y = jnp.concatenate(ys, axis=0)

    if finalize:
        y = yf_ref[...] + y
        inv_n = 1.0 / _RW_HEAD
        mean = segsum(y) * inv_n
        d = y - mean
        var = segsum(d * d) * inv_n
        yn = d * lax.rsqrt(var + _RW_GN_EPS) * lnw_ref[...] + lnb_ref[...]
        bonus = segsum(r * k * rk_ref[...]) * v
        gd = rws[:, 3 * c + _LANES:3 * c + 2 * _LANES]
        g = _dot(_sigmoid(gd), gup_ref[...])
        o_ref[...] = (yn + bonus) * g
    else:
        o_ref[...] = y


def _rwkv_pass(p_rw, prm, d, yf, reverse, tt):
    b, t, cols = p_rw.shape
    c = prm["k_k"].shape[-1]
    nt = t // tt
    finalize = yf is not None
    main, prev, nxt = _time_specs(cols, tt, nt, t, reverse)
    tidx = (lambda i: nt - 1 - i) if reverse else (lambda i: i)
    tile = pl.BlockSpec((None, tt, c), lambda bi, i: (bi, tidx(i), 0))
    consts = [prm["mu"], prm["k_k"], prm["k_a"], prm["w0"][d], prm["a0"][d], prm["w_up"][d],
              prm["a_up"][d], prm["ones_r"], prm["ones_e"]]
    args = [p_rw, p_rw, p_rw] + consts
    in_specs = [main, prev, nxt] + [_const_spec(z.shape) for z in consts]
    if finalize:
        extra = [prm["r_k"], prm["ln_w"], prm["ln_b"], prm["g_up"]]
        args += [yf] + extra
        in_specs += [tile] + [_const_spec(z.shape) for z in extra]
    return pl.pallas_call(
        functools.partial(_rwkv_kernel, reverse=reverse, finalize=finalize, tt=tt, nt=nt),
        grid=(b, nt),
        in_specs=in_specs,
        out_specs=tile,
        out_shape=jax.ShapeDtypeStruct((b, t, c), _F32),
        scratch_shapes=[pltpu.VMEM((c // _RW_GROUP, _RW_GROUP, _RW_GROUP), _F32)],
        compiler_params=_cparams(2),
        name="rwkv_bwd" if reverse else "rwkv_fwd",
    )(*args)


def _out_proj2_kernel(x_ref, a_ref, b_ref, wa_ref, wb_ref, o_ref):
    o_ref[...] = x_ref[...] + _dot(a_ref[...], wa_ref[...]) + _dot(b_ref[...], wb_ref[...])


def _out_proj2(x, a, bm, wa, wb, tm):
    b, t, d = x.shape
    nt = t // tm
    row = lambda wd: pl.BlockSpec((None, tm, wd), lambda bi, i: (bi, i, 0))
    return pl.pallas_call(
        _out_proj2_kernel,
        grid=(b, nt),
        in_specs=[row(d), row(a.shape[-1]), row(bm.shape[-1]), _const_spec(wa.shape),
                  _const_spec(wb.shape)],
        out_specs=row(d),
        out_shape=jax.ShapeDtypeStruct((b, t, d), _F32),
        compiler_params=_cparams(2),
        name="even_out_proj",
    )(x, a, bm, wa, wb)


def _ffn_kernel(*refs, tm, nt, ck, final_norm):
    x_ref, xp_ref, xn_ref, g_ref, wup_ref, cw_ref, cb_ref, wd_ref = refs[:8]
    if final_norm:
        gf_ref, o_ref, h_s, act_s = refs[8:]
    else:
        o_ref, h_s, act_s = refs[8:]
    i = pl.program_id(1)
    dff = wd_ref.shape[0]
    halo = 2 * _SUBLANES
    g = g_ref[...]
    zeros = jnp.zeros((_SUBLANES, x_ref.shape[-1]), _F32)
    prev = jnp.where(i == 0, 0.0, _rms(xp_ref[...], g))
    nxt = jnp.where(i == nt - 1, 0.0, _rms(xn_ref[...], g))
    h_s[...] = jnp.concatenate([zeros, prev, _rms(x_ref[...], g), nxt, zeros], axis=0).astype(_BF16)
    hext = h_s[...]
    hmain = h_s[pl.ds(halo, tm), :]
    cw = cw_ref[...]
    cb = cb_ref[...]
    for c in range(dff // ck):
        cs = slice(c * ck, (c + 1) * ck)
        ug = jnp.dot(hext, wup_ref[:, dff + c * ck:dff + (c + 1) * ck], preferred_element_type=_F32)
        gate = cb[:, cs] + _shifted(ug, 1, tm, halo) * cw[0:1, cs]
        gate = gate + _shifted(ug, 0, tm, halo) * cw[1:2, cs]
        gate = gate + _shifted(ug, -1, tm, halo) * cw[2:3, cs]
        val = jnp.dot(hmain, wup_ref[:, cs], preferred_element_type=_F32)
        act_s[:, cs] = (_gelu_tanh(gate) * val).astype(_BF16)
    y = x_ref[...] + jnp.dot(act_s[...], wd_ref[...], preferred_element_type=_F32)
    if final_norm:
        y = _rms(y, gf_ref[...])
    o_ref[...] = y


def _ffn(x, g, w_up, conv_w, conv_b, w_down, g_final, tm, ck):
    b, t, d = x.shape
    dff = w_down.shape[0]
    nt = t // tm
    final_norm = g_final is not None
    main, prev, nxt = _time_specs(d, tm, nt, t, False)
    consts = [g, w_up, conv_w, conv_b, w_down] + ([g_final] if final_norm else [])
    return pl.pallas_call(
        functools.partial(_ffn_kernel, tm=tm, nt=nt, ck=ck, final_norm=final_norm),
        grid=(b, nt),
        in_specs=[main, prev, nxt] + [_resident_spec(z.shape) for z in consts],
        out_specs=main,
        out_shape=jax.ShapeDtypeStruct((b, t, d), _F32),
        scratch_shapes=[pltpu.VMEM((tm + 4 * _SUBLANES, d), _BF16), pltpu.VMEM((tm, dff), _BF16)],
        compiler_params=_cparams(2),
        name="conv_ffn",
    )(x, x, x, *consts)


def _mlstm_prep_kernel(xm_ref, xp_ref, xn_ref, cw_ref, cb_ref, wq_ref, wk_ref, wv_ref, wg_ref,
                       bg_ref, q_ref, k_ref, v_ref, xc_ref, gt_ref, *, tt, nt):
    i = pl.program_id(1)
    inner = xm_ref.shape[-1]
    cw = cw_ref[...]
    xm = xm_ref[...]
    shift = _halo_shifter(xm, xp_ref[...], xn_ref[...], i, nt)
    z = cb_ref[...] + shift(2) * cw[0:1]
    z = z + shift(1) * cw[1:2]
    z = z + xm * cw[2:3]
    z = z + shift(-1) * cw[3:4]
    xc = z * _sigmoid(z)
    xc_ref[...] = xc.astype(xc_ref.dtype)
    qscale = (inner // _ML_HEADS) ** -0.5
    gates = bg_ref[...]
    for j in range(inner // _ML_BLK):
        sl = slice(j * _ML_BLK, (j + 1) * _ML_BLK)
        q = _dot(xc[:, sl], wq_ref[j])
        k = _dot(xc[:, sl], wk_ref[j])
        v = _dot(xm[:, sl], wv_ref[j])
        gates = (gates + _dot(q, wg_ref[sl]) + _dot(k, wg_ref[inner + j * _ML_BLK:inner + (j + 1) * _ML_BLK])
                 + _dot(v, wg_ref[2 * inner + j * _ML_BLK:2 * inner + (j + 1) * _ML_BLK]))
        q_ref[:, sl] = (q * qscale).astype(q_ref.dtype)
        k_ref[:, sl] = k.astype(k_ref.dtype)
        v_ref[:, sl] = v.astype(v_ref.dtype)
    gt_ref[...] = gates


def _mlstm_prep(xm, prm, tt):
    b, t, inner = xm.shape
    nt = t // tt
    main, prev, nxt = _time_specs(inner, tt, nt, t, False)
    consts = [prm["conv_w"], prm["conv_b"], prm["wq"], prm["wk"], prm["wv"], prm["wg"], prm["bg"]]
    row = lambda wd: pl.BlockSpec((None, tt, wd), lambda bi, i: (bi, i, 0))
    big = jax.ShapeDtypeStruct((b, t, inner), _BF16)
    return pl.pallas_call(
        functools.partial(_mlstm_prep_kernel, tt=tt, nt=nt),
        grid=(b, nt),
        in_specs=[main, prev, nxt] + [_const_spec(z.shape) for z in consts],
        out_specs=[row(inner)] * 4 + [row(_LANES)],
        out_shape=[big, big, big, big, jax.ShapeDtypeStruct((b, t, _LANES), _F32)],
        compiler_params=_cparams(2),
        name="mlstm_prep",
    )(xm, xm, xm, *consts)


def _mlstm_kernel(q_ref, k_ref, v_ref, gt_ref, o_ref, c_s, n_s, m_s, *, reverse, d, lc):
    i = pl.program_id(1)
    dh = c_s.shape[-1]

    @pl.when(i == 0)
    def _():
        c_s[...] = jnp.zeros_like(c_s)
        n_s[...] = jnp.zeros_like(n_s)
        m_s[...] = jnp.zeros_like(m_s)

    gates = gt_ref[...]
    lane = lax.broadcasted_iota(jnp.int32, gates.shape, 1)
    row = lax.broadcasted_iota(jnp.int32, (lc, lc), 0)
    col = lax.broadcasted_iota(jnp.int32, (lc, lc), 1)
    tri = (col >= row) if reverse else (col <= row)
    bcum_all = _dot01_left(tri.astype(_BF16), -_softplus(-gates))

    def lane_col(mat, j):
        return jnp.sum(jnp.where(lane == j, mat, 0.0), axis=-1, keepdims=True)

    hs = range(_ML_HEADS)
    sls = [slice(h * dh, (h + 1) * dh) for h in hs]
    li = [lane_col(gates, d * _ML_HEADS + h) for h in hs]
    bcum = [lane_col(bcum_all, 2 * _ML_HEADS + d * _ML_HEADS + h) for h in hs]
    m_prev = [m_s[h][:, 0:1] for h in hs]
    src_row = [jnp.broadcast_to(li[h] - bcum[h], (lc, lc)).T for h in hs]
    dmat = [jnp.where(tri, bcum[h] + src_row[h], -jnp.inf) for h in hs]
    inter = [bcum[h] + m_prev[h] for h in hs]
    m_t = [jnp.maximum(inter[h], jnp.max(dmat[h], axis=-1, keepdims=True)) for h in hs]
    g = [bcum[h][0:1] if reverse else bcum[h][lc - 1:lc] for h in hs]
    wlog = [g[h] - bcum[h] + li[h] for h in hs]
    m_new = [jnp.maximum(g[h] + m_prev[h], jnp.max(wlog[h], axis=0, keepdims=True)) for h in hs]
    wexp = [jnp.exp(wlog[h] - m_new[h]) for h in hs]
    wexp_row = [jnp.exp(g[h] + src_row[h][0:2 * _SUBLANES] - m_new[h]) for h in hs]
    decay = [jnp.exp(g[h] + m_prev[h] - m_new[h]) for h in hs]
    q = [q_ref[:, sls[h]] for h in hs]
    k = [k_ref[:, sls[h]] for h in hs]
    v = [v_ref[:, sls[h]] for h in hs]
    cmat = [c_s[h] for h in hs]
    nvec = [n_s[h] for h in hs]
    s_raw = [_dot_nt(q[h], k[h]) for h in hs]
    cq = [_dot_nt(q[h], cmat[h]) for h in hs]
    qn = [_dot_nt(q[h], jnp.broadcast_to(nvec[h], (_LANES, dh)))[:, 0:1] for h in hs]
    upd = [_dot_tn(v[h] * wexp[h].astype(v[h].dtype), k[h]) for h in hs]
    nupd = [_dot(wexp_row[h], k[h])[0:1] for h in hs]
    s = [s_raw[h] * jnp.exp(dmat[h] - m_t[h]) for h in hs]
    pv = [_dot(s[h], v[h]) for h in hs]
    for h in hs:
        scale = jnp.exp(inter[h] - m_t[h])
        num = pv[h] + scale * cq[h]
        den = jnp.sum(s[h], axis=-1, keepdims=True) + scale * qn[h]
        o_ref[:, sls[h]] = (num / jnp.maximum(jnp.abs(den), jnp.exp(-m_t[h]))).astype(o_ref.dtype)
        c_s[h] = decay[h] * cmat[h] + upd[h]
        n_s[h] = decay[h] * nvec[h] + nupd[h]
        m_s[h] = jnp.broadcast_to(m_new[h], m_s.shape[1:])


def _mlstm_pass(q, k, v, gt, d, reverse, lc):
    b, t, inner = q.shape
    nc = t // lc
    dh = inner // _ML_HEADS
    tidx = (lambda i: nc - 1 - i) if reverse else (lambda i: i)
    row = lambda wd: pl.BlockSpec((None, lc, wd), lambda bi, i: (bi, tidx(i), 0))
    return pl.pallas_call(
        functools.partial(_mlstm_kernel, reverse=reverse, d=d, lc=lc),
        grid=(b, nc),
        in_specs=[row(inner)] * 3 + [row(_LANES)],
        out_specs=row(inner),
        out_shape=jax.ShapeDtypeStruct((b, t, inner), _BF16),
        scratch_shapes=[pltpu.VMEM((_ML_HEADS, dh, dh), _F32), pltpu.VMEM((_ML_HEADS, 1, dh), _F32),
                        pltpu.VMEM((_ML_HEADS, 1, _LANES), _F32)],
        compiler_params=_cparams(2),
        name="mlstm_bwd" if reverse else "mlstm_fwd",
    )(q, k, v, gt)


def _mlstm_out_kernel(x_ref, hf_ref, hb_ref, xc_ref, z_ref, nw_ref, skip_ref, wo_ref, o_ref):
    hs = hf_ref[...].astype(_F32) + hb_ref[...].astype(_F32)
    inner = hs.shape[-1]
    dh = inner // _ML_HEADS
    parts = []
    for h in range(_ML_HEADS):
        hh = hs[:, h * dh:(h + 1) * dh]
        mean = jnp.mean(hh, axis=-1, keepdims=True)
        dlt = hh - mean
        var = jnp.mean(dlt * dlt, axis=-1, keepdims=True)
        parts.append(dlt * lax.rsqrt(var + _ML_LN_EPS))
    hn = jnp.concatenate(parts, axis=-1) * nw_ref[...]
    z = z_ref[...].astype(_F32)
    out = (hn + skip_ref[...] * xc_ref[...].astype(_F32)) * (z * _sigmoid(z))
    o_ref[...] = x_ref[...] + _dot(out, wo_ref[...])


def _mlstm_out(x, hf, hb, xc, z, norm_w, skip, w_out, tm):
    b, t, d = x.shape
    inner = hf.shape[-1]
    nt = t // tm
    row = lambda wd: pl.BlockSpec((None, tm, wd), lambda bi, i: (bi, i, 0))
    return pl.pallas_call(
        _mlstm_out_kernel,
        grid=(b, nt),
        in_specs=[row(d)] + [row(inner)] * 4 + [_const_spec(norm_w.shape), _const_spec(skip.shape),
                                                _const_spec(w_out.shape)],
        out_specs=row(d),
        out_shape=jax.ShapeDtypeStruct((b, t, d), _F32),
        compiler_params=_cparams(2),
        name="mlstm_out_proj",
    )(x, hf, hb, xc, z, norm_w, skip, w_out)


def _block_diag_dense(w):
    nb, bi, bo = w.shape
    eye = jnp.eye(nb, dtype=w.dtype)
    return jnp.einsum("nio,nm->nimo", w, eye).reshape(nb * bi, nb * bo)


def _block_diag_blocks(w, blk):
    nb, bi, bo = w.shape
    per = blk // bi
    return jax.vmap(_block_diag_dense)(w.reshape(nb // per, per, bi, bo))


def _row(v):
    return v.reshape(1, -1).astype(_F32)


def _tile(t_len, want):
    return min(want, t_len)


def _even_layer(x, li, p):
    (w_in, conv_w, conv_b, wa, ba, wx, bx, lam, mu, w0, w_up, a0, a_up, g_up, k_k, k_a, r_k,
     ln_w, ln_b, w_out) = [z[li] for z in p["even"]]
    t = x.shape[1]
    width = lam.shape[-1]
    c = k_k.shape[-1]
    p_lru, p_rw = _norm_proj(x, _row(p["norm_mix"][2 * li]), w_in.astype(_BF16),
                             (2 * width, w_in.shape[1] - 2 * width), (_F32, _F32),
                             _tile(t, _NORM_TILE))

    tt = _tile(t, _LRU_TILE)
    h_dir = None
    for d, rev in enumerate((False, True)):
        wg = jnp.concatenate([_block_diag_dense(wa[d]), _block_diag_dense(wx[d])], axis=1).astype(_BF16)
        bg = jnp.concatenate([ba[d], bx[d]]).reshape(1, -1)
        h_dir = _lru_pass(p_lru, conv_w, _row(conv_b), wg, bg, _row(lam[d]), h_dir, rev, tt)
    lru_out = h_dir

    zeros = jnp.zeros((_RW_HEAD, c), _F32)
    head = jnp.arange(c) // _RW_HEAD
    prm = {
        "mu": _row(mu), "k_k": _row(k_k), "k_a": _row(k_a),
        "w0": [_row(w0[d]) for d in range(2)], "a0": [_row(a0[d]) for d in range(2)],
        "w_up": [jnp.concatenate([w_up[d], zeros], axis=0).astype(_BF16) for d in range(2)],
        "a_up": [jnp.concatenate([zeros, a_up[d]], axis=0).astype(_BF16) for d in range(2)],
        "ones_r": (head[:, None] == jnp.arange(_LANES)[None, :]).astype(_BF16),
        "ones_e": (jnp.arange(_LANES)[:, None] == head[None, :]).astype(_BF16),
        "r_k": _row(r_k), "ln_w": _row(ln_w), "ln_b": _row(ln_b), "g_up": g_up.astype(_BF16),
    }
    tt = _tile(t, _RW_TILE)
    y_dir = None
    for d, rev in enumerate((False, True)):
        y_dir = _rwkv_pass(p_rw, prm, d, y_dir, rev, tt)
    rw_out = y_dir

    wo = w_out.astype(_BF16)
    return _out_proj2(x, lru_out, rw_out, wo[:width], wo[width:], _tile(t, _ROW_TILE))


def _odd_layer(x, li, p):
    (w_in, conv_w, conv_b, w_q, w_k, w_v, w_ig, b_ig, w_fg, b_fg, skip, norm_w,
     w_out) = [z[li] for z in p["odd"]]
    t = x.shape[1]
    inner = skip.shape[-1]
    xm, z = _norm_proj(x, _row(p["norm_mix"][2 * li + 1]), w_in.astype(_BF16), (inner, inner),
                       (_F32, _BF16), _tile(t, _NORM_TILE))
    nh = b_ig.shape[-1]
    wg = jnp.concatenate([w_ig[0], w_ig[1], w_fg[0], w_fg[1],
                          jnp.zeros((3 * inner, _LANES - 4 * nh), _F32)], axis=1).astype(_BF16)
    bg = jnp.concatenate([b_ig[0], b_ig[1], b_fg[0], b_fg[1], jnp.zeros((_LANES - 4 * nh,), _F32)])
    prm = {
        "conv_w": conv_w, "conv_b": _row(conv_b),
        "wq": _block_diag_blocks(w_q, _ML_BLK).astype(_BF16),
        "wk": _block_diag_blocks(w_k, _ML_BLK).astype(_BF16),
        "wv": _block_diag_blocks(w_v, _ML_BLK).astype(_BF16),
        "wg": wg, "bg": _row(bg),
    }
    q, k, v, xc, gt = _mlstm_prep(xm, prm, _tile(t, _ML_ROW_TILE))
    lc = _tile(t, _ML_TILE)
    hf = _mlstm_pass(q, k, v, gt, 0, False, lc)
    hb = _mlstm_pass(q, k, v, gt, 1, True, lc)
    return _mlstm_out(x, hf, hb, xc, z, _row(norm_w), _row(skip), w_out.astype(_BF16),
                      _tile(t, _ML_OUT_TILE))


def _trunk(x, p):
    depth = p["norm_mix"].shape[0]
    t = x.shape[1]
    for layer in range(depth):
        if layer % 2 == 0:
            x = _even_layer(x, layer // 2, p)
        else:
            x = _odd_layer(x, layer // 2, p)
        w_up, conv_w, conv_b, w_down = [z[layer] for z in p["ffn"]]
        dff = w_down.shape[0]
        ck = _FFN_COLS if dff % _FFN_COLS == 0 else dff
        g_final = _row(p["norm_final"]) if layer == depth - 1 else None
        x = _ffn(x, _row(p["norm_ffn"][layer]), w_up.astype(_BF16), conv_w, _row(conv_b),
                 w_down.astype(_BF16), g_final, _tile(t, _ROW_TILE), ck)
    return x


def kernel(x_prompt, x_sample, ev_w_in, ev_lru_conv_w, ev_lru_conv_b, ev_lru_wa, ev_lru_ba, ev_lru_wx, ev_lru_bx, ev_lru_lambda, ev_rw_mu, ev_rw_w0, ev_rw_w_up, ev_rw_a0, ev_rw_a_up, ev_rw_g_up, ev_rw_k_k, ev_rw_k_a, ev_rw_r_k, ev_rw_ln_w, ev_rw_ln_b, ev_w_out, od_w_in, od_conv_w, od_conv_b, od_w_q, od_w_k, od_w_v, od_w_ig, od_b_ig, od_w_fg, od_b_fg, od_skip, od_norm_w, od_w_out, ffn_w_up, ffn_conv_w, ffn_conv_b, ffn_w_down, norm_mix, norm_ffn, norm_final):
    p = {
        "even": (ev_w_in, ev_lru_conv_w, ev_lru_conv_b, ev_lru_wa, ev_lru_ba, ev_lru_wx, ev_lru_bx,
                 ev_lru_lambda, ev_rw_mu, ev_rw_w0, ev_rw_w_up, ev_rw_a0, ev_rw_a_up, ev_rw_g_up,
                 ev_rw_k_k, ev_rw_k_a, ev_rw_r_k, ev_rw_ln_w, ev_rw_ln_b, ev_w_out),
        "odd": (od_w_in, od_conv_w, od_conv_b, od_w_q, od_w_k, od_w_v, od_w_ig, od_b_ig, od_w_fg,
                od_b_fg, od_skip, od_norm_w, od_w_out),
        "ffn": (ffn_w_up, ffn_conv_w, ffn_conv_b, ffn_w_down),
        "norm_mix": norm_mix, "norm_ffn": norm_ffn, "norm_final": norm_final,
    }
    return (_trunk(x_prompt, p), _trunk(x_sample, p))
```

```python
import functools
import math

import jax
import jax.numpy as jnp
from jax import lax
from jax.experimental import pallas as pl
from jax.experimental.pallas import tpu as pltpu

_F32 = jnp.float32
_BF16 = jnp.bfloat16

_RMS_EPS = 1e-6
_LRU_C = 8.0
_RW_HEAD = 64
_RW_DECAY_SCALE = math.exp(-0.5)
_RW_GN_EPS = 64e-5
_ML_HEADS = 4
_ML_LN_EPS = 1e-5
_ML_BLK = 256

_SUBLANES = 8
_LANES = 128
_VMEM_LIMIT = 56 * 1024 * 1024
_ROW_TILE = 512
_NORM_TILE = 512
_LRU_TILE = 512
_LRU_SEQS = 2
_RW_TILE = 256
_RW_CHUNK = 64
_RW_GROUP = 256
_RW_SEQS = 2
_ML_ROW_TILE = 256
_ML_OUT_TILE = 512
_ML_TILE = 256
_FFN_COLS = 256


def _cparams(n_axes):
    return pltpu.CompilerParams(dimension_semantics=("arbitrary",) * n_axes,
                                vmem_limit_bytes=_VMEM_LIMIT)


def _dot(a, b):
    return jnp.dot(a.astype(_BF16), b.astype(_BF16), preferred_element_type=_F32)


def _dot_nt(a, b):
    return lax.dot_general(a.astype(_BF16), b.astype(_BF16), (((1,), (1,)), ((), ())),
                           preferred_element_type=_F32)


def _dot_tn(a, b):
    return lax.dot_general(a.astype(_BF16), b.astype(_BF16), (((0,), (0,)), ((), ())),
                           preferred_element_type=_F32)


def _split3(x):
    hi = x.astype(_BF16)
    r1 = x - hi.astype(_F32)
    mid = r1.astype(_BF16)
    lo = (r1 - mid.astype(_F32)).astype(_BF16)
    return hi, mid, lo


def _dot01_left(m01, x):
    hi, mid, lo = _split3(x)
    f = lambda p: jnp.dot(m01, p, preferred_element_type=_F32)
    return (f(lo) + f(mid)) + f(hi)


def _dot01_right(x, m01):
    hi = x.astype(_BF16)
    lo = (x - hi.astype(_F32)).astype(_BF16)
    f = lambda p: jnp.dot(p, m01, preferred_element_type=_F32)
    return f(lo) + f(hi)


def _sigmoid(x):
    return 0.5 * jnp.tanh(0.5 * x) + 0.5


def _softplus(x):
    return jnp.maximum(x, 0.0) + jnp.log1p(jnp.exp(-jnp.abs(x)))


def _gelu_tanh(x):
    c = math.sqrt(2.0 / math.pi)
    return x * (0.5 * (1.0 + jnp.tanh(c * (x + 0.044715 * (x * x * x)))))


def _rms(x, g):
    ms = jnp.mean(x * x, axis=-1, keepdims=True)
    return x * lax.rsqrt(ms + _RMS_EPS) * g


def _halo_shifter(main, prev, nxt, ti, nt):
    rows = main.shape[0]
    prev = jnp.where(ti == 0, 0.0, prev)
    nxt = jnp.where(ti == nt - 1, 0.0, nxt)
    ridx = lax.broadcasted_iota(jnp.int32, main.shape, 0)
    first, second, last = ridx == 0, ridx == 1, ridx == rows - 1

    def shift(k):
        rolled = pltpu.roll(main, k % rows, axis=0)
        if k == 1:
            return jnp.where(first, prev[_SUBLANES - 1:_SUBLANES], rolled)
        if k == 2:
            return jnp.where(first, prev[_SUBLANES - 2:_SUBLANES - 1],
                             jnp.where(second, prev[_SUBLANES - 1:_SUBLANES], rolled))
        assert k == -1
        return jnp.where(last, nxt[0:1], rolled)

    return shift


def _shifted(ext, k, rows, halo=_SUBLANES):
    if k == 0:
        return ext[halo:halo + rows]
    return pltpu.roll(ext, k % ext.shape[0], axis=0)[halo:halo + rows]


def _time_specs(width, rows, nt, t_len, reverse, col=0, bb=None):
    per = rows // _SUBLANES
    last = t_len // _SUBLANES - 1

    def tidx(i):
        return (nt - 1 - i) if reverse else i

    main = pl.BlockSpec((bb, rows, width), lambda b, i: (b, tidx(i), col))
    prev = pl.BlockSpec((bb, _SUBLANES, width),
                        lambda b, i: (b, jnp.maximum(tidx(i) * per - 1, 0), col))
    nxt = pl.BlockSpec((bb, _SUBLANES, width),
                       lambda b, i: (b, jnp.minimum((tidx(i) + 1) * per, last), col))
    return main, prev, nxt


def _const_spec(shape):
    nd = len(shape)
    return pl.BlockSpec(shape, lambda *_: (0,) * nd)


def _resident_spec(shape):
    nd = len(shape)
    return pl.BlockSpec(shape, lambda *_: (0,) * nd, pipeline_mode=pl.Buffered(1))


def _norm_proj_kernel(x_ref, g_ref, w_ref, *o_refs):
    h = _rms(x_ref[...], g_ref[...])
    res = _dot(h, w_ref[...])
    off = 0
    for o in o_refs:
        wdt = o.shape[-1]
        o[...] = res[:, off:off + wdt].astype(o.dtype)
        off += wdt


def _norm_proj(x, g, w, splits, dtypes, tm):
    b, t, d = x.shape
    n = w.shape[1]
    assert sum(splits) == n
    nt = t // tm
    row = lambda wd: pl.BlockSpec((None, tm, wd), lambda bi, i: (bi, i, 0))
    return pl.pallas_call(
        _norm_proj_kernel,
        grid=(b, nt),
        in_specs=[row(d), _const_spec((1, d)), _resident_spec((d, n))],
        out_specs=[row(s) for s in splits],
        out_shape=[jax.ShapeDtypeStruct((b, t, s), dt) for s, dt in zip(splits, dtypes)],
        compiler_params=_cparams(2),
        name="norm_proj",
    )(x, g, w)


def _lru_kernel(*refs, reverse, finalize, tt, nt):
    x_ref, xp_ref, xn_ref, cw_ref, cb_ref, wg_ref, bg_ref, lam_ref = refs[:8]
    if finalize:
        hf_ref, gate_ref, o_ref, a_s, u_s, h_s, carry = refs[8:]
    else:
        o_ref, a_s, u_s, carry = refs[8:]
        h_s = o_ref
    i = pl.program_id(1)
    ti = (nt - 1 - i) if reverse else i
    nseq, _, width = x_ref.shape

    cw = cw_ref[...]
    neg_c_sp = (-_LRU_C) * _softplus(-lam_ref[...])
    for b in range(nseq):
        xb = x_ref[b]
        shift = _halo_shifter(xb, xp_ref[b], xn_ref[b], ti, nt)
        xc = cb_ref[...] + shift(2) * cw[0:1]
        xc = xc + shift(1) * cw[1:2]
        xc = xc + xb * cw[2:3]
        xc = xc + shift(-1) * cw[3:4]
        gates = _dot(xc, wg_ref[...]) + bg_ref[...]
        r = _sigmoid(gates[:, :width])
        ig = _sigmoid(gates[:, width:])
        log_a = r * neg_c_sp
        th = jnp.tanh(log_a)
        a_s[b] = jnp.exp(log_a)
        u_s[b] = jnp.sqrt(-2.0 * th) * lax.rsqrt(1.0 - th) * (ig * xc)

    @pl.when(i == 0)
    def _():
        carry[...] = jnp.zeros_like(carry)

    ng = tt // _SUBLANES
    sub = lax.broadcasted_iota(jnp.int32, (_SUBLANES, width), 0)

    def body(s, hs):
        g = (ng - 1 - s) if reverse else s
        off = pl.multiple_of(g * _SUBLANES, _SUBLANES)
        hs = list(hs)
        outs = [jnp.zeros((_SUBLANES, width), _F32)] * nseq
        order = range(_SUBLANES - 1, -1, -1) if reverse else range(_SUBLANES)
        for j in order:
            for b in range(nseq):
                hs[b] = a_s[b, pl.ds(off + j, 1), :] * hs[b] + u_s[b, pl.ds(off + j, 1), :]
                outs[b] = jnp.where(sub == j, hs[b], outs[b])
        for b in range(nseq):
            h_s[b, pl.ds(off, _SUBLANES), :] = outs[b]
        return tuple(hs)

    hs = lax.fori_loop(0, ng, body, tuple(carry[b] for b in range(nseq)))
    for b in range(nseq):
        carry[b] = hs[b]

    if finalize:
        o_ref[...] = (hf_ref[...] + h_s[...]) * _gelu_tanh(gate_ref[...])


def _lru_pass(p_lru, conv_w, conv_b, wg, bg, lam, hf, reverse, tt):
    b, t, w2 = p_lru.shape
    w = w2 // 2
    nt = t // tt
    nseq = _LRU_SEQS if b % _LRU_SEQS == 0 else 1
    finalize = hf is not None
    main, prev, nxt = _time_specs(w, tt, nt, t, reverse, bb=nseq)
    tidx = (lambda i: nt - 1 - i) if reverse else (lambda i: i)
    tile = lambda col: pl.BlockSpec((nseq, tt, w), lambda bi, i: (bi, tidx(i), col))
    in_specs = [main, prev, nxt, _const_spec(conv_w.shape), _const_spec(conv_b.shape),
                _const_spec(wg.shape), _const_spec(bg.shape), _const_spec(lam.shape)]
    args = [p_lru, p_lru, p_lru, conv_w, conv_b, wg, bg, lam]
    scratch = [pltpu.VMEM((nseq, tt, w), _F32), pltpu.VMEM((nseq, tt, w), _F32)]
    if finalize:
        in_specs += [tile(0), tile(1)]
        args += [hf, p_lru]
        scratch += [pltpu.VMEM((nseq, tt, w), _F32)]
    scratch += [pltpu.VMEM((nseq, 1, w), _F32)]
    return pl.pallas_call(
        functools.partial(_lru_kernel, reverse=reverse, finalize=finalize, tt=tt, nt=nt),
        grid=(b // nseq, nt),
        in_specs=in_specs,
        out_specs=tile(0),
        out_shape=jax.ShapeDtypeStruct((b, t, w), _F32),
        scratch_shapes=scratch,
        compiler_params=_cparams(2),
        name="lru_bwd" if reverse else "lru_fwd",
    )(*args)


def _rwkv_kernel(*refs, reverse, finalize, tt, nt):
    (rw_ref, rwp_ref, rwn_ref, mu_ref, kk_ref, ka_ref, w0_ref, a0_ref, wup_ref, aup_ref,
     onesr_ref, onese_ref) = refs[:12]
    if finalize:
        yf_ref, rk_ref, lnw_ref, lnb_ref, gup_ref, o_ref, ht_s = refs[12:]
    else:
        o_ref, ht_s = refs[12:]
    i = pl.program_id(1)
    ti = (nt - 1 - i) if reverse else i
    nseq, ngrp, gw, _ = ht_s.shape
    c = ngrp * gw
    lc = _RW_CHUNK
    hpg = gw // _RW_HEAD
    ones_r = onesr_ref[...]
    ones_e = onese_ref[...]
    seqs = range(nseq)

    def segsum(x):
        return _dot01_right(_dot01_right(x, ones_r), ones_e)

    tok = []
    for b in seqs:
        x = rw_ref[b]
        tshift = _halo_shifter(x, rwp_ref[b], rwn_ref[b], ti, nt)
        rws = x + mu_ref[...] * (0.5 * (tshift(1) + tshift(-1)) - x)
        r = rws[:, 0:c]
        k = rws[:, c:2 * c]
        v = rws[:, 2 * c:3 * c]
        wad = rws[:, 3 * c:3 * c + _LANES]
        kk = k * kk_ref[...]
        kk = kk * jnp.minimum(lax.rsqrt(segsum(kk * kk)), 1e12)
        lw = (-_RW_DECAY_SCALE) * _sigmoid(w0_ref[...] + _dot(jnp.tanh(wad), wup_ref[...]))
        a = _sigmoid(a0_ref[...] + _dot(wad, aup_ref[...]))
        tok.append(dict(rws=rws, r=r, k=k, v=v, kk=kk, lw=lw, kt=k * (1.0 + (a - 1.0) * ka_ref[...]),
                        bvec=a * kk))

    shift = _RW_HEAD.bit_length() - 1
    row = lax.broadcasted_iota(jnp.int32, (gw, gw), 0)
    col = lax.broadcasted_iota(jnp.int32, (gw, gw), 1)
    same = (row >> shift) == (col >> shift)
    trow = lax.broadcasted_iota(jnp.int32, (lc, gw), 0)
    tcol = lax.broadcasted_iota(jnp.int32, (lc, gw), 1) & (lc - 1)
    m_incl = (tcol >= trow) if reverse else (tcol <= trow)
    m_strict = (tcol > trow) if reverse else (tcol < trow)
    eye_cat = jnp.where(tcol == trow, 1.0, 0.0)
    srow = lax.broadcasted_iota(jnp.int32, (lc, lc), 0)
    scol = lax.broadcasted_iota(jnp.int32, (lc, lc), 1)
    tri01 = ((scol >= srow) if reverse else (scol <= srow)).astype(_BF16)

    def stack(xv):
        xb = xv.astype(_BF16)
        return jnp.where(same, jnp.concatenate([xb] * hpg, axis=0), jnp.zeros((), _BF16))

    @pl.when(i == 0)
    def _():
        ht_s[...] = jnp.zeros_like(ht_s)

    nchunk = tt // lc
    chunks = list(range(nchunk - 1, -1, -1) if reverse else range(nchunk))
    groups = range(ngrp)
    lanes = [slice(g * gw, (g + 1) * gw) for g in groups]
    units = [(b, ci, g) for ci in chunks for b in seqs for g in groups]

    pre = {}
    for ci in chunks:
        sl = slice(ci * lc, (ci + 1) * lc)
        for b in seqs:
            tb = tok[b]
            r_c, lw_c, k_c, v_c, kk_c, b_c = (tb[n][sl] for n in ("r", "lw", "kt", "v", "kk", "bvec"))
            cum = _dot01_left(tri01, lw_c)
            clast = cum[0:1] if reverse else cum[lc - 1:lc]
            einv = jnp.exp(-cum)
            eend = jnp.exp(clast - cum)
            pre[b, ci] = dict(gamma=jnp.exp(clast), rg=r_c * jnp.exp(cum), kkg=kk_c * jnp.exp(cum - lw_c),
                              kd=k_c * einv, bd=b_c * einv, kend=k_c * eend, bend=b_c * eend, v=v_c)

    st = {}
    for b, ci, g in units:
        ls = lanes[g]
        pc = pre[b, ci]
        lhs = jnp.concatenate([pc["kkg"][:, ls], pc["rg"][:, ls]], axis=0)
        rhs = jnp.concatenate([stack(pc["kd"][:, ls]), stack(pc["bd"][:, ls])], axis=0)
        amat = _dot_nt(lhs, rhs)
        nj = -jnp.where(m_strict, amat[:lc, gw:], 0.0)
        st[b, ci, g] = dict(a_kk_k=jnp.where(m_strict, amat[:lc, :gw], 0.0),
                            a_r_k=jnp.where(m_incl, amat[lc:, :gw], 0.0),
                            a_r_b=jnp.where(m_incl, amat[lc:, gw:], 0.0),
                            nj=nj, tinv=eye_cat + nj)
    for key in units:
        s = st[key]
        s["nj"] = _dot(s["nj"], stack(s["nj"]))
    for _ in range(lc.bit_length() - 3):
        for key in units:
            s = st[key]
            pn = _dot(jnp.concatenate([s["tinv"], s["nj"]], axis=0), stack(s["nj"]))
            s["tinv"] = s["tinv"] + pn[:lc]
            s["nj"] = pn[lc:]
    for key in units:
        s = st[key]
        s["tinv"] = s["tinv"] + _dot(s["tinv"], stack(s["nj"]))
    for b, ci, g in units:
        s = st[b, ci, g]
        s["av"] = _dot(jnp.concatenate([s["a_kk_k"], s["a_r_k"]], axis=0), stack(pre[b, ci]["v"][:, lanes[g]]))
    for b, ci, g in units:
        s = st[b, ci, g]
        tk = _dot(s["tinv"], jnp.concatenate([stack(pre[b, ci]["kkg"][:, lanes[g]]), stack(s["av"][:lc])],
                                             axis=1))
        s["kkgp"], s["uv"] = tk[:, :gw], tk[:, gw:]
    for b, ci, g in units:
        ls = lanes[g]
        s = st[b, ci, g]
        pc = pre[b, ci]
        s["kb"] = jnp.where(same, _dot_tn(s["kkgp"], pc["bend"][:, ls]), 0.0)
        s["gmat"] = jnp.where(same, _dot_tn(jnp.concatenate([pc["v"][:, ls], -s["uv"]], axis=0),
                                            jnp.concatenate([pc["kend"][:, ls], pc["bend"][:, ls]], axis=0)),
                              0.0)

    chains = [(b, g) for b in seqs for g in groups]
    hts = {key: ht_s[key[0], key[1]] for key in chains}
    ys = {}
    for ci in chunks:
        cur = dict(hts)
        for b, g in chains:
            s = st[b, ci, g]
            hts[b, g] = cur[b, g] * pre[b, ci]["gamma"][:, lanes[g]] - _dot(cur[b, g], s["kb"]) + s["gmat"]
        pus = {(b, g): _dot_nt(jnp.concatenate([st[b, ci, g]["kkgp"], pre[b, ci]["rg"][:, lanes[g]]], axis=0),
                               cur[b, g]) for b, g in chains}
        us = {key: pus[key][:lc] + st[key[0], ci, key[1]]["uv"] for key in chains}
        for b in seqs:
            ys[b, ci] = jnp.concatenate(
                [pus[b, g][lc:] + st[b, ci, g]["av"][lc:] - _dot(st[b, ci, g]["a_r_b"], stack(us[b, g]))
                 for g in groups], axis=1)
    for b, g in chains:
        ht_s[b, g] = hts[b, g]

    for b in seqs:
        y = jnp.concatenate([ys[b, ci] for ci in range(nchunk)], axis=0)
        if finalize:
            tb = tok[b]
            y = yf_ref[b] + y
            inv_n = 1.0 / _RW_HEAD
            mean = segsum(y) * inv_n
            d = y - mean
            var = segsum(d * d) * inv_n
            yn = d * lax.rsqrt(var + _RW_GN_EPS) * lnw_ref[...] + lnb_ref[...]
            bonus = segsum(tb["r"] * tb["k"] * rk_ref[...]) * tb["v"]
            gd = tb["rws"][:, 3 * c + _LANES:3 * c + 2 * _LANES]
            y = (yn + bonus) * _dot(_sigmoid(gd), gup_ref[...])
        o_ref[b] = y


def _rwkv_pass(p_rw, prm, d, yf, reverse, tt):
    b, t, cols = p_rw.shape
    c = prm["k_k"].shape[-1]
    nt = t // tt
    nseq = _RW_SEQS if b % _RW_SEQS == 0 else 1
    finalize = yf is not None
    main, prev, nxt = _time_specs(cols, tt, nt, t, reverse, bb=nseq)
    tidx = (lambda i: nt - 1 - i) if reverse else (lambda i: i)
    tile = pl.BlockSpec((nseq, tt, c), lambda bi, i: (bi, tidx(i), 0))
    consts = [prm["mu"], prm["k_k"], prm["k_a"], prm["w0"][d], prm["a0"][d], prm["w_up"][d],
              prm["a_up"][d], prm["ones_r"], prm["ones_e"]]
    args = [p_rw, p_rw, p_rw] + consts
    in_specs = [main, prev, nxt] + [_const_spec(z.shape) for z in consts]
    if finalize:
        extra = [prm["r_k"], prm["ln_w"], prm["ln_b"], prm["g_up"]]
        args += [yf] + extra
        in_specs += [tile] + [_const_spec(z.shape) for z in extra]
    return pl.pallas_call(
        functools.partial(_rwkv_kernel, reverse=reverse, finalize=finalize, tt=tt, nt=nt),
        grid=(b // nseq, nt),
        in_specs=in_specs,
        out_specs=tile,
        out_shape=jax.ShapeDtypeStruct((b, t, c), _F32),
        scratch_shapes=[pltpu.VMEM((nseq, c // _RW_GROUP, _RW_GROUP, _RW_GROUP), _F32)],
        compiler_params=_cparams(2),
        name="rwkv_bwd" if reverse else "rwkv_fwd",
    )(*args)


def _out_proj2_kernel(x_ref, a_ref, b_ref, wa_ref, wb_ref, o_ref):
    o_ref[...] = x_ref[...] + _dot(a_ref[...], wa_ref[...]) + _dot(b_ref[...], wb_ref[...])


def _out_proj2(x, a, bm, wa, wb, tm):
    b, t, d = x.shape
    nt = t // tm
    row = lambda wd: pl.BlockSpec((None, tm, wd), lambda bi, i: (bi, i, 0))
    return pl.pallas_call(
        _out_proj2_kernel,
        grid=(b, nt),
        in_specs=[row(d), row(a.shape[-1]), row(bm.shape[-1]), _const_spec(wa.shape),
                  _const_spec(wb.shape)],
        out_specs=row(d),
        out_shape=jax.ShapeDtypeStruct((b, t, d), _F32),
        compiler_params=_cparams(2),
        name="even_out_proj",
    )(x, a, bm, wa, wb)


def _ffn_kernel(*refs, tm, nt, ck, final_norm):
    x_ref, xp_ref, xn_ref, g_ref, wup_ref, cw_ref, cb_ref, wd_ref = refs[:8]
    if final_norm:
        gf_ref, o_ref, h_s, act_s = refs[8:]
    else:
        o_ref, h_s, act_s = refs[8:]
    i = pl.program_id(1)
    dff = wd_ref.shape[0]
    halo = 2 * _SUBLANES
    g = g_ref[...]
    zeros = jnp.zeros((_SUBLANES, x_ref.shape[-1]), _F32)
    prev = jnp.where(i == 0, 0.0, _rms(xp_ref[...], g))
    nxt = jnp.where(i == nt - 1, 0.0, _rms(xn_ref[...], g))
    h_s[...] = jnp.concatenate([zeros, prev, _rms(x_ref[...], g), nxt, zeros], axis=0).astype(_BF16)
    hext = h_s[...]
    hmain = h_s[pl.ds(halo, tm), :]
    cw = cw_ref[...]
    cb = cb_ref[...]
    for c in range(dff // ck):
        cs = slice(c * ck, (c + 1) * ck)
        ug = jnp.dot(hext, wup_ref[:, dff + c * ck:dff + (c + 1) * ck], preferred_element_type=_F32)
        gate = cb[:, cs] + _shifted(ug, 1, tm, halo) * cw[0:1, cs]
        gate = gate + _shifted(ug, 0, tm, halo) * cw[1:2, cs]
        gate = gate + _shifted(ug, -1, tm, halo) * cw[2:3, cs]
        val = jnp.dot(hmain, wup_ref[:, cs], preferred_element_type=_F32)
        act_s[:, cs] = (_gelu_tanh(gate) * val).astype(_BF16)
    y = x_ref[...] + jnp.dot(act_s[...], wd_ref[...], preferred_element_type=_F32)
    if final_norm:
        y = _rms(y, gf_ref[...])
    o_ref[...] = y


def _ffn(x, g, w_up, conv_w, conv_b, w_down, g_final, tm, ck):
    b, t, d = x.shape
    dff = w_down.shape[0]
    nt = t // tm
    final_norm = g_final is not None
    main, prev, nxt = _time_specs(d, tm, nt, t, False)
    consts = [g, w_up, conv_w, conv_b, w_down] + ([g_final] if final_norm else [])
    return pl.pallas_call(
        functools.partial(_ffn_kernel, tm=tm, nt=nt, ck=ck, final_norm=final_norm),
        grid=(b, nt),
        in_specs=[main, prev, nxt] + [_resident_spec(z.shape) for z in consts],
        out_specs=main,
        out_shape=jax.ShapeDtypeStruct((b, t, d), _F32),
        scratch_shapes=[pltpu.VMEM((tm + 4 * _SUBLANES, d), _BF16), pltpu.VMEM((tm, dff), _BF16)],
        compiler_params=_cparams(2),
        name="conv_ffn",
    )(x, x, x, *consts)


def _mlstm_prep_kernel(xm_ref, xp_ref, xn_ref, cw_ref, cb_ref, wq_ref, wk_ref, wv_ref, wg_ref,
                       bg_ref, q_ref, k_ref, v_ref, xc_ref, gt_ref, *, tt, nt):
    i = pl.program_id(1)
    inner = xm_ref.shape[-1]
    cw = cw_ref[...]
    xm = xm_ref[...]
    shift = _halo_shifter(xm, xp_ref[...], xn_ref[...], i, nt)
    z = cb_ref[...] + shift(2) * cw[0:1]
    z = z + shift(1) * cw[1:2]
    z = z + xm * cw[2:3]
    z = z + shift(-1) * cw[3:4]
    xc = z * _sigmoid(z)
    xc_ref[...] = xc.astype(xc_ref.dtype)
    qscale = (inner // _ML_HEADS) ** -0.5
    gates = bg_ref[...]
    for j in range(inner // _ML_BLK):
        sl = slice(j * _ML_BLK, (j + 1) * _ML_BLK)
        q = _dot(xc[:, sl], wq_ref[j])
        k = _dot(xc[:, sl], wk_ref[j])
        v = _dot(xm[:, sl], wv_ref[j])
        gates = (gates + _dot(q, wg_ref[sl]) + _dot(k, wg_ref[inner + j * _ML_BLK:inner + (j + 1) * _ML_BLK])
                 + _dot(v, wg_ref[2 * inner + j * _ML_BLK:2 * inner + (j + 1) * _ML_BLK]))
        q_ref[:, sl] = (q * qscale).astype(q_ref.dtype)
        k_ref[:, sl] = k.astype(k_ref.dtype)
        v_ref[:, sl] = v.astype(v_ref.dtype)
    gt_ref[...] = gates


def _mlstm_prep(xm, prm, tt):
    b, t, inner = xm.shape
    nt = t // tt
    main, prev, nxt = _time_specs(inner, tt, nt, t, False)
    consts = [prm["conv_w"], prm["conv_b"], prm["wq"], prm["wk"], prm["wv"], prm["wg"], prm["bg"]]
    row = lambda wd: pl.BlockSpec((None, tt, wd), lambda bi, i: (bi, i, 0))
    big = jax.ShapeDtypeStruct((b, t, inner), _BF16)
    return pl.pallas_call(
        functools.partial(_mlstm_prep_kernel, tt=tt, nt=nt),
        grid=(b, nt),
        in_specs=[main, prev, nxt] + [_const_spec(z.shape) for z in consts],
        out_specs=[row(inner)] * 4 + [row(_LANES)],
        out_shape=[big, big, big, big, jax.ShapeDtypeStruct((b, t, _LANES), _F32)],
        compiler_params=_cparams(2),
        name="mlstm_prep",
    )(xm, xm, xm, *consts)


def _mlstm_kernel(q_ref, k_ref, v_ref, gt_ref, o_ref, c_s, n_s, m_s, *, reverse, d, lc):
    i = pl.program_id(1)
    dh = c_s.shape[-1]

    @pl.when(i == 0)
    def _():
        c_s[...] = jnp.zeros_like(c_s)
        n_s[...] = jnp.zeros_like(n_s)
        m_s[...] = jnp.zeros_like(m_s)

    gates = gt_ref[...]
    lane = lax.broadcasted_iota(jnp.int32, gates.shape, 1)
    row = lax.broadcasted_iota(jnp.int32, (lc, lc), 0)
    col = lax.broadcasted_iota(jnp.int32, (lc, lc), 1)
    tri = (col >= row) if reverse else (col <= row)
    bcum_all = _dot01_left(tri.astype(_BF16), -_softplus(-gates))

    def lane_col(mat, j):
        return jnp.sum(jnp.where(lane == j, mat, 0.0), axis=-1, keepdims=True)

    hs = range(_ML_HEADS)
    sls = [slice(h * dh, (h + 1) * dh) for h in hs]
    li = [lane_col(gates, d * _ML_HEADS + h) for h in hs]
    bcum = [lane_col(bcum_all, 2 * _ML_HEADS + d * _ML_HEADS + h) for h in hs]
    m_prev = [m_s[h][:, 0:1] for h in hs]
    src_row = [jnp.broadcast_to(li[h] - bcum[h], (lc, lc)).T for h in hs]
    dmat = [jnp.where(tri, bcum[h] + src_row[h], -jnp.inf) for h in hs]
    inter = [bcum[h] + m_prev[h] for h in hs]
    m_t = [jnp.maximum(inter[h], jnp.max(dmat[h], axis=-1, keepdims=True)) for h in hs]
    g = [bcum[h][0:1] if reverse else bcum[h][lc - 1:lc] for h in hs]
    wlog = [g[h] - bcum[h] + li[h] for h in hs]
    m_new = [jnp.maximum(g[h] + m_prev[h], jnp.max(wlog[h], axis=0, keepdims=True)) for h in hs]
    wexp = [jnp.exp(wlog[h] - m_new[h]) for h in hs]
    wexp_row = [jnp.exp(g[h] + src_row[h][0:2 * _SUBLANES] - m_new[h]) for h in hs]
    decay = [jnp.exp(g[h] + m_prev[h] - m_new[h]) for h in hs]
    q = [q_ref[:, sls[h]] for h in hs]
    k = [k_ref[:, sls[h]] for h in hs]
    v = [v_ref[:, sls[h]] for h in hs]
    cmat = [c_s[h] for h in hs]
    nvec = [n_s[h] for h in hs]
    s_raw = [_dot_nt(q[h], k[h]) for h in hs]
    cq = [_dot_nt(q[h], cmat[h]) for h in hs]
    qn = [_dot_nt(q[h], jnp.broadcast_to(nvec[h], (_LANES, dh)))[:, 0:1] for h in hs]
    upd = [_dot_tn(v[h] * wexp[h].astype(v[h].dtype), k[h]) for h in hs]
    nupd = [_dot(wexp_row[h], k[h])[0:1] for h in hs]
    s = [s_raw[h] * jnp.exp(dmat[h] - m_t[h]) for h in hs]
    pv = [_dot(s[h], v[h]) for h in hs]
    for h in hs:
        scale = jnp.exp(inter[h] - m_t[h])
        num = pv[h] + scale * cq[h]
        den = jnp.sum(s[h], axis=-1, keepdims=True) + scale * qn[h]
        o_ref[:, sls[h]] = (num / jnp.maximum(jnp.abs(den), jnp.exp(-m_t[h]))).astype(o_ref.dtype)
        c_s[h] = decay[h] * cmat[h] + upd[h]
        n_s[h] = decay[h] * nvec[h] + nupd[h]
        m_s[h] = jnp.broadcast_to(m_new[h], m_s.shape[1:])


def _mlstm_pass(q, k, v, gt, d, reverse, lc):
    b, t, inner = q.shape
    nc = t // lc
    dh = inner // _ML_HEADS
    tidx = (lambda i: nc - 1 - i) if reverse else (lambda i: i)
    row = lambda wd: pl.BlockSpec((None, lc, wd), lambda bi, i: (bi, tidx(i), 0))
    return pl.pallas_call(
        functools.partial(_mlstm_kernel, reverse=reverse, d=d, lc=lc),
        grid=(b, nc),
        in_specs=[row(inner)] * 3 + [row(_LANES)],
        out_specs=row(inner),
        out_shape=jax.ShapeDtypeStruct((b, t, inner), _BF16),
        scratch_shapes=[pltpu.VMEM((_ML_HEADS, dh, dh), _F32), pltpu.VMEM((_ML_HEADS, 1, dh), _F32),
                        pltpu.VMEM((_ML_HEADS, 1, _LANES), _F32)],
        compiler_params=_cparams(2),
        name="mlstm_bwd" if reverse else "mlstm_fwd",
    )(q, k, v, gt)


def _mlstm_out_kernel(x_ref, hf_ref, hb_ref, xc_ref, z_ref, nw_ref, skip_ref, wo_ref, o_ref):
    hs = hf_ref[...].astype(_F32) + hb_ref[...].astype(_F32)
    inner = hs.shape[-1]
    dh = inner // _ML_HEADS
    parts = []
    for h in range(_ML_HEADS):
        hh = hs[:, h * dh:(h + 1) * dh]
        mean = jnp.mean(hh, axis=-1, keepdims=True)
        dlt = hh - mean
        var = jnp.mean(dlt * dlt, axis=-1, keepdims=True)
        parts.append(dlt * lax.rsqrt(var + _ML_LN_EPS))
    hn = jnp.concatenate(parts, axis=-1) * nw_ref[...]
    z = z_ref[...].astype(_F32)
    out = (hn + skip_ref[...] * xc_ref[...].astype(_F32)) * (z * _sigmoid(z))
    o_ref[...] = x_ref[...] + _dot(out, wo_ref[...])


def _mlstm_out(x, hf, hb, xc, z, norm_w, skip, w_out, tm):
    b, t, d = x.shape
    inner = hf.shape[-1]
    nt = t // tm
    row = lambda wd: pl.BlockSpec((None, tm, wd), lambda bi, i: (bi, i, 0))
    return pl.pallas_call(
        _mlstm_out_kernel,
        grid=(b, nt),
        in_specs=[row(d)] + [row(inner)] * 4 + [_const_spec(norm_w.shape), _const_spec(skip.shape),
                                                _const_spec(w_out.shape)],
        out_specs=row(d),
        out_shape=jax.ShapeDtypeStruct((b, t, d), _F32),
        compiler_params=_cparams(2),
        name="mlstm_out_proj",
    )(x, hf, hb, xc, z, norm_w, skip, w_out)


def _block_diag_dense(w):
    nb, bi, bo = w.shape
    eye = jnp.eye(nb, dtype=w.dtype)
    return jnp.einsum("nio,nm->nimo", w, eye).reshape(nb * bi, nb * bo)


def _block_diag_blocks(w, blk):
    nb, bi, bo = w.shape
    per = blk // bi
    return jax.vmap(_block_diag_dense)(w.reshape(nb // per, per, bi, bo))


def _row(v):
    return v.reshape(1, -1).astype(_F32)


def _tile(t_len, want):
    return min(want, t_len)


def _even_layer(x, li, p):
    (w_in, conv_w, conv_b, wa, ba, wx, bx, lam, mu, w0, w_up, a0, a_up, g_up, k_k, k_a, r_k,
     ln_w, ln_b, w_out) = [z[li] for z in p["even"]]
    t = x.shape[1]
    width = lam.shape[-1]
    c = k_k.shape[-1]
    p_lru, p_rw = _norm_proj(x, _row(p["norm_mix"][2 * li]), w_in.astype(_BF16),
                             (2 * width, w_in.shape[1] - 2 * width), (_F32, _F32),
                             _tile(t, _NORM_TILE))

    tt = _tile(t, _LRU_TILE)
    h_dir = None
    for d, rev in enumerate((False, True)):
        wg = jnp.concatenate([_block_diag_dense(wa[d]), _block_diag_dense(wx[d])], axis=1).astype(_BF16)
        bg = jnp.concatenate([ba[d], bx[d]]).reshape(1, -1)
        h_dir = _lru_pass(p_lru, conv_w, _row(conv_b), wg, bg, _row(lam[d]), h_dir, rev, tt)
    lru_out = h_dir

    zeros = jnp.zeros((_RW_HEAD, c), _F32)
    head = jnp.arange(c) // _RW_HEAD
    prm = {
        "mu": _row(mu), "k_k": _row(k_k), "k_a": _row(k_a),
        "w0": [_row(w0[d]) for d in range(2)], "a0": [_row(a0[d]) for d in range(2)],
        "w_up": [jnp.concatenate([w_up[d], zeros], axis=0).astype(_BF16) for d in range(2)],
        "a_up": [jnp.concatenate([zeros, a_up[d]], axis=0).astype(_BF16) for d in range(2)],
        "ones_r": (head[:, None] == jnp.arange(_LANES)[None, :]).astype(_BF16),
        "ones_e": (jnp.arange(_LANES)[:, None] == head[None, :]).astype(_BF16),
        "r_k": _row(r_k), "ln_w": _row(ln_w), "ln_b": _row(ln_b), "g_up": g_up.astype(_BF16),
    }
    tt = _tile(t, _RW_TILE)
    y_dir = None
    for d, rev in enumerate((False, True)):
        y_dir = _rwkv_pass(p_rw, prm, d, y_dir, rev, tt)
    rw_out = y_dir

    wo = w_out.astype(_BF16)
    return _out_proj2(x, lru_out, rw_out, wo[:width], wo[width:], _tile(t, _ROW_TILE))


def _odd_layer(x, li, p):
    (w_in, conv_w, conv_b, w_q, w_k, w_v, w_ig, b_ig, w_fg, b_fg, skip, norm_w,
     w_out) = [z[li] for z in p["odd"]]
    t = x.shape[1]
    inner = skip.shape[-1]
    xm, z = _norm_proj(x, _row(p["norm_mix"][2 * li + 1]), w_in.astype(_BF16), (inner, inner),
                       (_F32, _BF16), _tile(t, _NORM_TILE))
    nh = b_ig.shape[-1]
    wg = jnp.concatenate([w_ig[0], w_ig[1], w_fg[0], w_fg[1],
                          jnp.zeros((3 * inner, _LANES - 4 * nh), _F32)], axis=1).astype(_BF16)
    bg = jnp.concatenate([b_ig[0], b_ig[1], b_fg[0], b_fg[1], jnp.zeros((_LANES - 4 * nh,), _F32)])
    prm = {
        "conv_w": conv_w, "conv_b": _row(conv_b),
        "wq": _block_diag_blocks(w_q, _ML_BLK).astype(_BF16),
        "wk": _block_diag_blocks(w_k, _ML_BLK).astype(_BF16),
        "wv": _block_diag_blocks(w_v, _ML_BLK).astype(_BF16),
        "wg": wg, "bg": _row(bg),
    }
    q, k, v, xc, gt = _mlstm_prep(xm, prm, _tile(t, _ML_ROW_TILE))
    lc = _tile(t, _ML_TILE)
    hf = _mlstm_pass(q, k, v, gt, 0, False, lc)
    hb = _mlstm_pass(q, k, v, gt, 1, True, lc)
    return _mlstm_out(x, hf, hb, xc, z, _row(norm_w), _row(skip), w_out.astype(_BF16),
                      _tile(t, _ML_OUT_TILE))


def _trunk(x, p):
    depth = p["norm_mix"].shape[0]
    t = x.shape[1]
    for layer in range(depth):
        if layer % 2 == 0:
            x = _even_layer(x, layer // 2, p)
        else:
            x = _odd_layer(x, layer // 2, p)
        w_up, conv_w, conv_b, w_down = [z[layer] for z in p["ffn"]]
        dff = w_down.shape[0]
        ck = _FFN_COLS if dff % _FFN_COLS == 0 else dff
        g_final = _row(p["norm_final"]) if layer == depth - 1 else None
        x = _ffn(x, _row(p["norm_ffn"][layer]), w_up.astype(_BF16), conv_w, _row(conv_b),
                 w_down.astype(_BF16), g_final, _tile(t, _ROW_TILE), ck)
    return x


def kernel(x_prompt, x_sample, ev_w_in, ev_lru_conv_w, ev_lru_conv_b, ev_lru_wa, ev_lru_ba, ev_lru_wx, ev_lru_bx, ev_lru_lambda, ev_rw_mu, ev_rw_w0, ev_rw_w_up, ev_rw_a0, ev_rw_a_up, ev_rw_g_up, ev_rw_k_k, ev_rw_k_a, ev_rw_r_k, ev_rw_ln_w, ev_rw_ln_b, ev_w_out, od_w_in, od_conv_w, od_conv_b, od_w_q, od_w_k, od_w_v, od_w_ig, od_b_ig, od_w_fg, od_b_fg, od_skip, od_norm_w, od_w_out, ffn_w_up, ffn_conv_w, ffn_conv_b, ffn_w_down, norm_mix, norm_ffn, norm_final):
    p = {
        "even": (ev_w_in, ev_lru_conv_w, ev_lru_conv_b, ev_lru_wa, ev_lru_ba, ev_lru_wx, ev_lru_bx,
                 ev_lru_lambda, ev_rw_mu, ev_rw_w0, ev_rw_w_up, ev_rw_a0, ev_rw_a_up, ev_rw_g_up,
                 ev_rw_k_k, ev_rw_k_a, ev_rw_r_k, ev_rw_ln_w, ev_rw_ln_b, ev_w_out),
        "odd": (od_w_in, od_conv_w, od_conv_b, od_w_q, od_w_k, od_w_v, od_w_ig, od_b_ig, od_w_fg,
                od_b_fg, od_skip, od_norm_w, od_w_out),
        "ffn": (ffn_w_up, ffn_conv_w, ffn_conv_b, ffn_w_down),
        "norm_mix": norm_mix, "norm_ffn": norm_ffn, "norm_final": norm_final,
    }
    return (_trunk(x_prompt, p), _trunk(x_sample, p))
```

```python
import functools
import math

import jax
import jax.numpy as jnp
from jax import lax
from jax.experimental import pallas as pl
from jax.experimental.pallas import tpu as pltpu

_F32 = jnp.float32
_BF16 = jnp.bfloat16

_RMS_EPS = 1e-6
_LRU_C = 8.0
_RW_HEAD = 64
_RW_DECAY_SCALE = math.exp(-0.5)
_RW_GN_EPS = 64e-5
_ML_HEADS = 4
_ML_LN_EPS = 1e-5
_ML_BLK = 256

_SUBLANES = 8
_LANES = 128
_VMEM_LIMIT = 56 * 1024 * 1024
_ROW_TILE = 512
_NORM_TILE = 512
_LRU_TILE = 512
_LRU_SEQS = 2
_RW_TILE = 256
_RW_CHUNK = 64
_RW_GROUP = 256
_RW_SEQS = 2
_ML_ROW_TILE = 512
_ML_IN_COLS = 512
_ML_OUT_TILE = 512
_ML_TILE = 256
_FFN_COLS = 256


def _cparams(n_axes):
    return pltpu.CompilerParams(dimension_semantics=("arbitrary",) * n_axes,
                                vmem_limit_bytes=_VMEM_LIMIT)


def _dot(a, b):
    return jnp.dot(a.astype(_BF16), b.astype(_BF16), preferred_element_type=_F32)


def _dot_nt(a, b):
    return lax.dot_general(a.astype(_BF16), b.astype(_BF16), (((1,), (1,)), ((), ())),
                           preferred_element_type=_F32)


def _dot_tn(a, b):
    return lax.dot_general(a.astype(_BF16), b.astype(_BF16), (((0,), (0,)), ((), ())),
                           preferred_element_type=_F32)


def _split3(x):
    hi = x.astype(_BF16)
    r1 = x - hi.astype(_F32)
    mid = r1.astype(_BF16)
    lo = (r1 - mid.astype(_F32)).astype(_BF16)
    return hi, mid, lo


def _dot01_left(m01, x):
    hi, mid, lo = _split3(x)
    f = lambda p: jnp.dot(m01, p, preferred_element_type=_F32)
    return (f(lo) + f(mid)) + f(hi)


def _dot01_right(x, m01):
    hi = x.astype(_BF16)
    lo = (x - hi.astype(_F32)).astype(_BF16)
    f = lambda p: jnp.dot(p, m01, preferred_element_type=_F32)
    return f(lo) + f(hi)


def _sigmoid(x):
    return 0.5 * jnp.tanh(0.5 * x) + 0.5


def _softplus(x):
    return jnp.maximum(x, 0.0) + jnp.log1p(jnp.exp(-jnp.abs(x)))


def _gelu_tanh(x):
    c = math.sqrt(2.0 / math.pi)
    return x * (0.5 * (1.0 + jnp.tanh(c * (x + 0.044715 * (x * x * x)))))


def _rms(x, g):
    ms = jnp.mean(x * x, axis=-1, keepdims=True)
    return x * lax.rsqrt(ms + _RMS_EPS) * g


def _halo_shifter(main, prev, nxt, ti=None, nt=None):
    rows = main.shape[0]
    if ti is not None:
        prev = jnp.where(ti == 0, 0.0, prev)
        nxt = jnp.where(ti == nt - 1, 0.0, nxt)
    ridx = lax.broadcasted_iota(jnp.int32, main.shape, 0)
    first, second, last = ridx == 0, ridx == 1, ridx == rows - 1

    def shift(k):
        rolled = pltpu.roll(main, k % rows, axis=0)
        if k == 1:
            return jnp.where(first, prev[_SUBLANES - 1:_SUBLANES], rolled)
        if k == 2:
            return jnp.where(first, prev[_SUBLANES - 2:_SUBLANES - 1],
                             jnp.where(second, prev[_SUBLANES - 1:_SUBLANES], rolled))
        assert k == -1
        return jnp.where(last, nxt[0:1], rolled)

    return shift


def _shifted(ext, k, rows, halo=_SUBLANES):
    if k == 0:
        return ext[halo:halo + rows]
    return pltpu.roll(ext, k % ext.shape[0], axis=0)[halo:halo + rows]


def _time_specs(width, rows, nt, t_len, reverse, col=0, bb=None):
    per = rows // _SUBLANES
    last = t_len // _SUBLANES - 1

    def tidx(i):
        return (nt - 1 - i) if reverse else i

    main = pl.BlockSpec((bb, rows, width), lambda b, i: (b, tidx(i), col))
    prev = pl.BlockSpec((bb, _SUBLANES, width),
                        lambda b, i: (b, jnp.maximum(tidx(i) * per - 1, 0), col))
    nxt = pl.BlockSpec((bb, _SUBLANES, width),
                       lambda b, i: (b, jnp.minimum((tidx(i) + 1) * per, last), col))
    return main, prev, nxt


def _const_spec(shape):
    nd = len(shape)
    return pl.BlockSpec(shape, lambda *_: (0,) * nd)


def _resident_spec(shape):
    nd = len(shape)
    return pl.BlockSpec(shape, lambda *_: (0,) * nd, pipeline_mode=pl.Buffered(1))


def _norm_proj_kernel(x_ref, g_ref, w_ref, *o_refs):
    h = _rms(x_ref[...], g_ref[...])
    res = _dot(h, w_ref[...])
    off = 0
    for o in o_refs:
        wdt = o.shape[-1]
        o[...] = res[:, off:off + wdt].astype(o.dtype)
        off += wdt


def _norm_proj(x, g, w, splits, dtypes, tm):
    b, t, d = x.shape
    n = w.shape[1]
    assert sum(splits) == n
    nt = t // tm
    row = lambda wd: pl.BlockSpec((None, tm, wd), lambda bi, i: (bi, i, 0))
    return pl.pallas_call(
        _norm_proj_kernel,
        grid=(b, nt),
        in_specs=[row(d), _const_spec((1, d)), _resident_spec((d, n))],
        out_specs=[row(s) for s in splits],
        out_shape=[jax.ShapeDtypeStruct((b, t, s), dt) for s, dt in zip(splits, dtypes)],
        compiler_params=_cparams(2),
        name="norm_proj",
    )(x, g, w)


def _lru_kernel(*refs, reverse, finalize, tt, nt):
    x_ref, xp_ref, xn_ref, cw_ref, cb_ref, wg_ref, bg_ref, lam_ref = refs[:8]
    if finalize:
        hf_ref, gate_ref, o_ref, a_s, u_s, h_s, carry = refs[8:]
    else:
        o_ref, a_s, u_s, carry = refs[8:]
        h_s = o_ref
    i = pl.program_id(1)
    ti = (nt - 1 - i) if reverse else i
    nseq, _, width = x_ref.shape

    cw = cw_ref[...]
    neg_c_sp = (-_LRU_C) * _softplus(-lam_ref[...])
    for b in range(nseq):
        xb = x_ref[b]
        shift = _halo_shifter(xb, xp_ref[b], xn_ref[b], ti, nt)
        xc = cb_ref[...] + shift(2) * cw[0:1]
        xc = xc + shift(1) * cw[1:2]
        xc = xc + xb * cw[2:3]
        xc = xc + shift(-1) * cw[3:4]
        gates = _dot(xc, wg_ref[...]) + bg_ref[...]
        r = _sigmoid(gates[:, :width])
        ig = _sigmoid(gates[:, width:])
        log_a = r * neg_c_sp
        th = jnp.tanh(log_a)
        a_s[b] = jnp.exp(log_a)
        u_s[b] = jnp.sqrt(-2.0 * th) * lax.rsqrt(1.0 - th) * (ig * xc)

    @pl.when(i == 0)
    def _():
        carry[...] = jnp.zeros_like(carry)

    ng = tt // _SUBLANES
    sub = lax.broadcasted_iota(jnp.int32, (_SUBLANES, width), 0)

    def body(s, hs):
        g = (ng - 1 - s) if reverse else s
        off = pl.multiple_of(g * _SUBLANES, _SUBLANES)
        hs = list(hs)
        outs = [jnp.zeros((_SUBLANES, width), _F32)] * nseq
        order = range(_SUBLANES - 1, -1, -1) if reverse else range(_SUBLANES)
        for j in order:
            for b in range(nseq):
                hs[b] = a_s[b, pl.ds(off + j, 1), :] * hs[b] + u_s[b, pl.ds(off + j, 1), :]
                outs[b] = jnp.where(sub == j, hs[b], outs[b])
        for b in range(nseq):
            h_s[b, pl.ds(off, _SUBLANES), :] = outs[b]
        return tuple(hs)

    hs = lax.fori_loop(0, ng, body, tuple(carry[b] for b in range(nseq)))
    for b in range(nseq):
        carry[b] = hs[b]

    if finalize:
        o_ref[...] = (hf_ref[...] + h_s[...]) * _gelu_tanh(gate_ref[...])


def _lru_pass(p_lru, conv_w, conv_b, wg, bg, lam, hf, reverse, tt):
    b, t, w2 = p_lru.shape
    w = w2 // 2
    nt = t // tt
    nseq = _LRU_SEQS if b % _LRU_SEQS == 0 else 1
    finalize = hf is not None
    main, prev, nxt = _time_specs(w, tt, nt, t, reverse, bb=nseq)
    tidx = (lambda i: nt - 1 - i) if reverse else (lambda i: i)
    tile = lambda col: pl.BlockSpec((nseq, tt, w), lambda bi, i: (bi, tidx(i), col))
    in_specs = [main, prev, nxt, _const_spec(conv_w.shape), _const_spec(conv_b.shape),
                _const_spec(wg.shape), _const_spec(bg.shape), _const_spec(lam.shape)]
    args = [p_lru, p_lru, p_lru, conv_w, conv_b, wg, bg, lam]
    scratch = [pltpu.VMEM((nseq, tt, w), _F32), pltpu.VMEM((nseq, tt, w), _F32)]
    if finalize:
        in_specs += [tile(0), tile(1)]
        args += [hf, p_lru]
        scratch += [pltpu.VMEM((nseq, tt, w), _F32)]
    scratch += [pltpu.VMEM((nseq, 1, w), _F32)]
    return pl.pallas_call(
        functools.partial(_lru_kernel, reverse=reverse, finalize=finalize, tt=tt, nt=nt),
        grid=(b // nseq, nt),
        in_specs=in_specs,
        out_specs=tile(0),
        out_shape=jax.ShapeDtypeStruct((b, t, w), _F32),
        scratch_shapes=scratch,
        compiler_params=_cparams(2),
        name="lru_bwd" if reverse else "lru_fwd",
    )(*args)


def _rwkv_kernel(*refs, reverse, finalize, tt, nt):
    (rw_ref, rwp_ref, rwn_ref, mu_ref, kk_ref, ka_ref, w0_ref, a0_ref, wup_ref, aup_ref,
     onesr_ref, onese_ref) = refs[:12]
    if finalize:
        yf_ref, rk_ref, lnw_ref, lnb_ref, gup_ref, o_ref, ht_s = refs[12:]
    else:
        o_ref, ht_s = refs[12:]
    i = pl.program_id(1)
    ti = (nt - 1 - i) if reverse else i
    nseq, ngrp, gw, _ = ht_s.shape
    c = ngrp * gw
    lc = _RW_CHUNK
    hpg = gw // _RW_HEAD
    ones_r = onesr_ref[...]
    ones_e = onese_ref[...]
    seqs = range(nseq)

    def segsum(x):
        return _dot01_right(_dot01_right(x, ones_r), ones_e)

    tok = []
    for b in seqs:
        x = rw_ref[b]
        tshift = _halo_shifter(x, rwp_ref[b], rwn_ref[b], ti, nt)
        rws = x + mu_ref[...] * (0.5 * (tshift(1) + tshift(-1)) - x)
        r = rws[:, 0:c]
        k = rws[:, c:2 * c]
        v = rws[:, 2 * c:3 * c]
        wad = rws[:, 3 * c:3 * c + _LANES]
        kk = k * kk_ref[...]
        kk = kk * jnp.minimum(lax.rsqrt(segsum(kk * kk)), 1e12)
        lw = (-_RW_DECAY_SCALE) * _sigmoid(w0_ref[...] + _dot(jnp.tanh(wad), wup_ref[...]))
        a = _sigmoid(a0_ref[...] + _dot(wad, aup_ref[...]))
        tok.append(dict(rws=rws, r=r, k=k, v=v, kk=kk, lw=lw, kt=k * (1.0 + (a - 1.0) * ka_ref[...]),
                        bvec=a * kk))

    shift = _RW_HEAD.bit_length() - 1
    row = lax.broadcasted_iota(jnp.int32, (gw, gw), 0)
    col = lax.broadcasted_iota(jnp.int32, (gw, gw), 1)
    same = (row >> shift) == (col >> shift)
    trow = lax.broadcasted_iota(jnp.int32, (lc, gw), 0)
    tcol = lax.broadcasted_iota(jnp.int32, (lc, gw), 1) & (lc - 1)
    m_incl = (tcol >= trow) if reverse else (tcol <= trow)
    m_strict = (tcol > trow) if reverse else (tcol < trow)
    eye_cat = jnp.where(tcol == trow, 1.0, 0.0)
    srow = lax.broadcasted_iota(jnp.int32, (lc, lc), 0)
    scol = lax.broadcasted_iota(jnp.int32, (lc, lc), 1)
    tri01 = ((scol >= srow) if reverse else (scol <= srow)).astype(_BF16)

    def stack(xv):
        xb = xv.astype(_BF16)
        return jnp.where(same, jnp.concatenate([xb] * hpg, axis=0), jnp.zeros((), _BF16))

    @pl.when(i == 0)
    def _():
        ht_s[...] = jnp.zeros_like(ht_s)

    nchunk = tt // lc
    chunks = list(range(nchunk - 1, -1, -1) if reverse else range(nchunk))
    groups = range(ngrp)
    lanes = [slice(g * gw, (g + 1) * gw) for g in groups]
    units = [(b, ci, g) for ci in chunks for b in seqs for g in groups]

    pre = {}
    for ci in chunks:
        sl = slice(ci * lc, (ci + 1) * lc)
        for b in seqs:
            tb = tok[b]
            r_c, lw_c, k_c, v_c, kk_c, b_c = (tb[n][sl] for n in ("r", "lw", "kt", "v", "kk", "bvec"))
            cum = _dot01_left(tri01, lw_c)
            clast = cum[0:1] if reverse else cum[lc - 1:lc]
            einv = jnp.exp(-cum)
            eend = jnp.exp(clast - cum)
            pre[b, ci] = dict(gamma=jnp.exp(clast), rg=r_c * jnp.exp(cum), kkg=kk_c * jnp.exp(cum - lw_c),
                              kd=k_c * einv, bd=b_c * einv, kend=k_c * eend, bend=b_c * eend, v=v_c)

    st = {}
    for b, ci, g in units:
        ls = lanes[g]
        pc = pre[b, ci]
        lhs = jnp.concatenate([pc["kkg"][:, ls], pc["rg"][:, ls]], axis=0)
        rhs = jnp.concatenate([stack(pc["kd"][:, ls]), stack(pc["bd"][:, ls])], axis=0)
        amat = _dot_nt(lhs, rhs)
        nj = -jnp.where(m_strict, amat[:lc, gw:], 0.0)
        st[b, ci, g] = dict(a_kk_k=jnp.where(m_strict, amat[:lc, :gw], 0.0),
                            a_r_k=jnp.where(m_incl, amat[lc:, :gw], 0.0),
                            a_r_b=jnp.where(m_incl, amat[lc:, gw:], 0.0),
                            nj=nj, tinv=eye_cat + nj)
    for key in units:
        s = st[key]
        s["nj"] = _dot(s["nj"], stack(s["nj"]))
    for _ in range(lc.bit_length() - 3):
        for key in units:
            s = st[key]
            pn = _dot(jnp.concatenate([s["tinv"], s["nj"]], axis=0), stack(s["nj"]))
            s["tinv"] = s["tinv"] + pn[:lc]
            s["nj"] = pn[lc:]
    for key in units:
        s = st[key]
        s["tinv"] = s["tinv"] + _dot(s["tinv"], stack(s["nj"]))
    for b, ci, g in units:
        s = st[b, ci, g]
        s["av"] = _dot(jnp.concatenate([s["a_kk_k"], s["a_r_k"]], axis=0), stack(pre[b, ci]["v"][:, lanes[g]]))
    for b, ci, g in units:
        s = st[b, ci, g]
        tk = _dot(s["tinv"], jnp.concatenate([stack(pre[b, ci]["kkg"][:, lanes[g]]), stack(s["av"][:lc])],
                                             axis=1))
        s["kkgp"], s["uv"] = tk[:, :gw], tk[:, gw:]
    for b, ci, g in units:
        ls = lanes[g]
        s = st[b, ci, g]
        pc = pre[b, ci]
        s["kb"] = jnp.where(same, _dot_tn(s["kkgp"], pc["bend"][:, ls]), 0.0)
        s["gmat"] = jnp.where(same, _dot_tn(jnp.concatenate([pc["v"][:, ls], -s["uv"]], axis=0),
                                            jnp.concatenate([pc["kend"][:, ls], pc["bend"][:, ls]], axis=0)),
                              0.0)

    chains = [(b, g) for b in seqs for g in groups]
    hts = {key: ht_s[key[0], key[1]] for key in chains}
    ys = {}
    for ci in chunks:
        cur = dict(hts)
        for b, g in chains:
            s = st[b, ci, g]
            hts[b, g] = cur[b, g] * pre[b, ci]["gamma"][:, lanes[g]] - _dot(cur[b, g], s["kb"]) + s["gmat"]
        pus = {(b, g): _dot_nt(jnp.concatenate([st[b, ci, g]["kkgp"], pre[b, ci]["rg"][:, lanes[g]]], axis=0),
                               cur[b, g]) for b, g in chains}
        us = {key: pus[key][:lc] + st[key[0], ci, key[1]]["uv"] for key in chains}
        for b in seqs:
            ys[b, ci] = jnp.concatenate(
                [pus[b, g][lc:] + st[b, ci, g]["av"][lc:] - _dot(st[b, ci, g]["a_r_b"], stack(us[b, g]))
                 for g in groups], axis=1)
    for b, g in chains:
        ht_s[b, g] = hts[b, g]

    for b in seqs:
        y = jnp.concatenate([ys[b, ci] for ci in range(nchunk)], axis=0)
        if finalize:
            tb = tok[b]
            y = yf_ref[b] + y
            inv_n = 1.0 / _RW_HEAD
            mean = segsum(y) * inv_n
            d = y - mean
            var = segsum(d * d) * inv_n
            yn = d * lax.rsqrt(var + _RW_GN_EPS) * lnw_ref[...] + lnb_ref[...]
            bonus = segsum(tb["r"] * tb["k"] * rk_ref[...]) * tb["v"]
            gd = tb["rws"][:, 3 * c + _LANES:3 * c + 2 * _LANES]
            y = (yn + bonus) * _dot(_sigmoid(gd), gup_ref[...])
        o_ref[b] = y


def _rwkv_pass(p_rw, prm, d, yf, reverse, tt):
    b, t, cols = p_rw.shape
    c = prm["k_k"].shape[-1]
    nt = t // tt
    nseq = _RW_SEQS if b % _RW_SEQS == 0 else 1
    finalize = yf is not None
    main, prev, nxt = _time_specs(cols, tt, nt, t, reverse, bb=nseq)
    tidx = (lambda i: nt - 1 - i) if reverse else (lambda i: i)
    tile = pl.BlockSpec((nseq, tt, c), lambda bi, i: (bi, tidx(i), 0))
    consts = [prm["mu"], prm["k_k"], prm["k_a"], prm["w0"][d], prm["a0"][d], prm["w_up"][d],
              prm["a_up"][d], prm["ones_r"], prm["ones_e"]]
    args = [p_rw, p_rw, p_rw] + consts
    in_specs = [main, prev, nxt] + [_const_spec(z.shape) for z in consts]
    if finalize:
        extra = [prm["r_k"], prm["ln_w"], prm["ln_b"], prm["g_up"]]
        args += [yf] + extra
        in_specs += [tile] + [_const_spec(z.shape) for z in extra]
    return pl.pallas_call(
        functools.partial(_rwkv_kernel, reverse=reverse, finalize=finalize, tt=tt, nt=nt),
        grid=(b // nseq, nt),
        in_specs=in_specs,
        out_specs=tile,
        out_shape=jax.ShapeDtypeStruct((b, t, c), _F32),
        scratch_shapes=[pltpu.VMEM((nseq, c // _RW_GROUP, _RW_GROUP, _RW_GROUP), _F32)],
        compiler_params=_cparams(2),
        name="rwkv_bwd" if reverse else "rwkv_fwd",
    )(*args)


def _out_proj2_kernel(x_ref, a_ref, b_ref, wa_ref, wb_ref, o_ref):
    o_ref[...] = x_ref[...] + _dot(a_ref[...], wa_ref[...]) + _dot(b_ref[...], wb_ref[...])


def _out_proj2(x, a, bm, wa, wb, tm):
    b, t, d = x.shape
    nt = t // tm
    row = lambda wd: pl.BlockSpec((None, tm, wd), lambda bi, i: (bi, i, 0))
    return pl.pallas_call(
        _out_proj2_kernel,
        grid=(b, nt),
        in_specs=[row(d), row(a.shape[-1]), row(bm.shape[-1]), _const_spec(wa.shape),
                  _const_spec(wb.shape)],
        out_specs=row(d),
        out_shape=jax.ShapeDtypeStruct((b, t, d), _F32),
        compiler_params=_cparams(2),
        name="even_out_proj",
    )(x, a, bm, wa, wb)


def _ffn_kernel(*refs, tm, nt, ck, final_norm):
    x_ref, xp_ref, xn_ref, g_ref, wup_ref, cw_ref, cb_ref, wd_ref = refs[:8]
    if final_norm:
        gf_ref, o_ref, h_s, act_s = refs[8:]
    else:
        o_ref, h_s, act_s = refs[8:]
    i = pl.program_id(1)
    dff = wd_ref.shape[0]
    halo = 2 * _SUBLANES
    g = g_ref[...]
    zeros = jnp.zeros((_SUBLANES, x_ref.shape[-1]), _F32)
    prev = jnp.where(i == 0, 0.0, _rms(xp_ref[...], g))
    nxt = jnp.where(i == nt - 1, 0.0, _rms(xn_ref[...], g))
    h_s[...] = jnp.concatenate([zeros, prev, _rms(x_ref[...], g), nxt, zeros], axis=0).astype(_BF16)
    hext = h_s[...]
    hmain = h_s[pl.ds(halo, tm), :]
    cw = cw_ref[...]
    cb = cb_ref[...]
    for c in range(dff // ck):
        cs = slice(c * ck, (c + 1) * ck)
        ug = jnp.dot(hext, wup_ref[:, dff + c * ck:dff + (c + 1) * ck], preferred_element_type=_F32)
        gate = cb[:, cs] + _shifted(ug, 1, tm, halo) * cw[0:1, cs]
        gate = gate + _shifted(ug, 0, tm, halo) * cw[1:2, cs]
        gate = gate + _shifted(ug, -1, tm, halo) * cw[2:3, cs]
        val = jnp.dot(hmain, wup_ref[:, cs], preferred_element_type=_F32)
        act_s[:, cs] = (_gelu_tanh(gate) * val).astype(_BF16)
    y = x_ref[...] + jnp.dot(act_s[...], wd_ref[...], preferred_element_type=_F32)
    if final_norm:
        y = _rms(y, gf_ref[...])
    o_ref[...] = y


def _ffn(x, g, w_up, conv_w, conv_b, w_down, g_final, tm, ck):
    b, t, d = x.shape
    dff = w_down.shape[0]
    nt = t // tm
    final_norm = g_final is not None
    main, prev, nxt = _time_specs(d, tm, nt, t, False)
    consts = [g, w_up, conv_w, conv_b, w_down] + ([g_final] if final_norm else [])
    return pl.pallas_call(
        functools.partial(_ffn_kernel, tm=tm, nt=nt, ck=ck, final_norm=final_norm),
        grid=(b, nt),
        in_specs=[main, prev, nxt] + [_resident_spec(z.shape) for z in consts],
        out_specs=main,
        out_shape=jax.ShapeDtypeStruct((b, t, d), _F32),
        scratch_shapes=[pltpu.VMEM((tm + 4 * _SUBLANES, d), _BF16), pltpu.VMEM((tm, dff), _BF16)],
        compiler_params=_cparams(2),
        name="conv_ffn",
    )(x, x, x, *consts)


def _mlstm_in_kernel(x_ref, xp_ref, xn_ref, g_ref, w_ref, cw_ref, cb_ref, wq_ref, wk_ref, wv_ref,
                     wg_ref, bg_ref, q_ref, k_ref, v_ref, xc_ref, z_ref, gt_ref, *, tm, nt, cols):
    i = pl.program_id(1)
    inner = z_ref.shape[-1]
    g = g_ref[...]
    hext = jnp.concatenate([_rms(xp_ref[...], g), _rms(x_ref[...], g), _rms(xn_ref[...], g)], axis=0)
    hext_b = hext.astype(_BF16)
    hmain_b = hext[_SUBLANES:_SUBLANES + tm].astype(_BF16)
    cw = cw_ref[...]
    cb = cb_ref[...]
    qscale = (inner // _ML_HEADS) ** -0.5
    nblk = inner // cols
    proj = lambda j: jnp.dot(hext_b, w_ref[:, j * cols:(j + 1) * cols], preferred_element_type=_F32)
    gates = bg_ref[...]
    xme_next = proj(0)
    for j in range(nblk):
        cs = slice(j * cols, (j + 1) * cols)
        xme = xme_next
        if j + 1 < nblk:
            xme_next = proj(j + 1)
        z_ref[:, cs] = jnp.dot(hmain_b, w_ref[:, inner + j * cols:inner + (j + 1) * cols],
                               preferred_element_type=_F32).astype(z_ref.dtype)
        prev = jnp.where(i == 0, 0.0, xme[:_SUBLANES])
        nxt = jnp.where(i == nt - 1, 0.0, xme[_SUBLANES + tm:])
        xm = xme[_SUBLANES:_SUBLANES + tm]
        shift = _halo_shifter(xm, prev, nxt)
        zc = cb[:, cs] + shift(2) * cw[0:1, cs]
        zc = zc + shift(1) * cw[1:2, cs]
        zc = zc + xm * cw[2:3, cs]
        zc = zc + shift(-1) * cw[3:4, cs]
        xc = zc * _sigmoid(zc)
        xc_ref[:, cs] = xc.astype(xc_ref.dtype)
        for jb in range(cols // _ML_BLK):
            blk = j * (cols // _ML_BLK) + jb
            sub = slice(jb * _ML_BLK, (jb + 1) * _ML_BLK)
            sl = slice(blk * _ML_BLK, (blk + 1) * _ML_BLK)
            q = _dot(xc[:, sub], wq_ref[blk])
            k = _dot(xc[:, sub], wk_ref[blk])
            v = _dot(xm[:, sub], wv_ref[blk])
            gates = (gates + _dot(q, wg_ref[sl])
                     + _dot(k, wg_ref[inner + blk * _ML_BLK:inner + (blk + 1) * _ML_BLK])
                     + _dot(v, wg_ref[2 * inner + blk * _ML_BLK:2 * inner + (blk + 1) * _ML_BLK]))
            q_ref[:, sl] = (q * qscale).astype(q_ref.dtype)
            k_ref[:, sl] = k.astype(k_ref.dtype)
            v_ref[:, sl] = v.astype(v_ref.dtype)
    gt_ref[...] = gates


def _mlstm_in(x, g, w_in, prm, tm):
    b, t, d = x.shape
    inner = w_in.shape[1] // 2
    nt = t // tm
    cols = _ML_IN_COLS if inner % _ML_IN_COLS == 0 else inner
    main, prev, nxt = _time_specs(d, tm, nt, t, False)
    consts = [g, w_in, prm["conv_w"], prm["conv_b"], prm["wq"], prm["wk"], prm["wv"], prm["wg"], prm["bg"]]
    row = lambda wd: pl.BlockSpec((None, tm, wd), lambda bi, i: (bi, i, 0))
    big = jax.ShapeDtypeStruct((b, t, inner), _BF16)
    return pl.pallas_call(
        functools.partial(_mlstm_in_kernel, tm=tm, nt=nt, cols=cols),
        grid=(b, nt),
        in_specs=[main, prev, nxt] + [_resident_spec(z.shape) for z in consts],
        out_specs=[row(inner)] * 5 + [row(_LANES)],
        out_shape=[big] * 5 + [jax.ShapeDtypeStruct((b, t, _LANES), _F32)],
        compiler_params=_cparams(2),
        name="mlstm_in",
    )(x, x, x, *consts)


def _mlstm_kernel(q_ref, k_ref, v_ref, gt_ref, o_ref, c_s, n_s, m_s, *, reverse, d, lc):
    i = pl.program_id(1)
    dh = c_s.shape[-1]

    @pl.when(i == 0)
    def _():
        c_s[...] = jnp.zeros_like(c_s)
        n_s[...] = jnp.zeros_like(n_s)
        m_s[...] = jnp.zeros_like(m_s)

    gates = gt_ref[...]
    lane = lax.broadcasted_iota(jnp.int32, gates.shape, 1)
    row = lax.broadcasted_iota(jnp.int32, (lc, lc), 0)
    col = lax.broadcasted_iota(jnp.int32, (lc, lc), 1)
    tri = (col >= row) if reverse else (col <= row)
    bcum_all = _dot01_left(tri.astype(_BF16), -_softplus(-gates))

    def lane_col(mat, j):
        return jnp.sum(jnp.where(lane == j, mat, 0.0), axis=-1, keepdims=True)

    hs = range(_ML_HEADS)
    sls = [slice(h * dh, (h + 1) * dh) for h in hs]
    li = [lane_col(gates, d * _ML_HEADS + h) for h in hs]
    bcum = [lane_col(bcum_all, 2 * _ML_HEADS + d * _ML_HEADS + h) for h in hs]
    m_prev = [m_s[h][:, 0:1] for h in hs]
    src_row = [jnp.broadcast_to(li[h] - bcum[h], (lc, lc)).T for h in hs]
    dmat = [jnp.where(tri, bcum[h] + src_row[h], -jnp.inf) for h in hs]
    inter = [bcum[h] + m_prev[h] for h in hs]
    m_t = [jnp.maximum(inter[h], jnp.max(dmat[h], axis=-1, keepdims=True)) for h in hs]
    g = [bcum[h][0:1] if reverse else bcum[h][lc - 1:lc] for h in hs]
    wlog = [g[h] - bcum[h] + li[h] for h in hs]
    m_new = [jnp.maximum(g[h] + m_prev[h], jnp.max(wlog[h], axis=0, keepdims=True)) for h in hs]
    wexp = [jnp.exp(wlog[h] - m_new[h]) for h in hs]
    wexp_row = [jnp.exp(g[h] + src_row[h][0:2 * _SUBLANES] - m_new[h]) for h in hs]
    decay = [jnp.exp(g[h] + m_prev[h] - m_new[h]) for h in hs]
    q = [q_ref[:, sls[h]] for h in hs]
    k = [k_ref[:, sls[h]] for h in hs]
    v = [v_ref[:, sls[h]] for h in hs]
    cmat = [c_s[h] for h in hs]
    nvec = [n_s[h] for h in hs]
    s_raw = [_dot_nt(q[h], k[h]) for h in hs]
    cq = [_dot_nt(q[h], cmat[h]) for h in hs]
    qn = [_dot_nt(q[h], jnp.broadcast_to(nvec[h], (_LANES, dh)))[:, 0:1] for h in hs]
    upd = [_dot_tn(v[h] * wexp[h].astype(v[h].dtype), k[h]) for h in hs]
    nupd = [_dot(wexp_row[h], k[h])[0:1] for h in hs]
    s = [s_raw[h] * jnp.exp(dmat[h] - m_t[h]) for h in hs]
    pv = [_dot(s[h], v[h]) for h in hs]
    for h in hs:
        scale = jnp.exp(inter[h] - m_t[h])
        num = pv[h] + scale * cq[h]
        den = jnp.sum(s[h], axis=-1, keepdims=True) + scale * qn[h]
        o_ref[:, sls[h]] = (num / jnp.maximum(jnp.abs(den), jnp.exp(-m_t[h]))).astype(o_ref.dtype)
        c_s[h] = decay[h] * cmat[h] + upd[h]
        n_s[h] = decay[h] * nvec[h] + nupd[h]
        m_s[h] = jnp.broadcast_to(m_new[h], m_s.shape[1:])


def _mlstm_pass(q, k, v, gt, d, reverse, lc):
    b, t, inner = q.shape
    nc = t // lc
    dh = inner // _ML_HEADS
    tidx = (lambda i: nc - 1 - i) if reverse else (lambda i: i)
    row = lambda wd: pl.BlockSpec((None, lc, wd), lambda bi, i: (bi, tidx(i), 0))
    return pl.pallas_call(
        functools.partial(_mlstm_kernel, reverse=reverse, d=d, lc=lc),
        grid=(b, nc),
        in_specs=[row(inner)] * 3 + [row(_LANES)],
        out_specs=row(inner),
        out_shape=jax.ShapeDtypeStruct((b, t, inner), _BF16),
        scratch_shapes=[pltpu.VMEM((_ML_HEADS, dh, dh), _F32), pltpu.VMEM((_ML_HEADS, 1, dh), _F32),
                        pltpu.VMEM((_ML_HEADS, 1, _LANES), _F32)],
        compiler_params=_cparams(2),
        name="mlstm_bwd" if reverse else "mlstm_fwd",
    )(q, k, v, gt)


def _mlstm_out_kernel(x_ref, hf_ref, hb_ref, xc_ref, z_ref, nw_ref, skip_ref, wo_ref, o_ref):
    hs = hf_ref[...].astype(_F32) + hb_ref[...].astype(_F32)
    inner = hs.shape[-1]
    dh = inner // _ML_HEADS
    parts = []
    for h in range(_ML_HEADS):
        hh = hs[:, h * dh:(h + 1) * dh]
        mean = jnp.mean(hh, axis=-1, keepdims=True)
        dlt = hh - mean
        var = jnp.mean(dlt * dlt, axis=-1, keepdims=True)
        parts.append(dlt * lax.rsqrt(var + _ML_LN_EPS))
    hn = jnp.concatenate(parts, axis=-1) * nw_ref[...]
    z = z_ref[...].astype(_F32)
    out = (hn + skip_ref[...] * xc_ref[...].astype(_F32)) * (z * _sigmoid(z))
    o_ref[...] = x_ref[...] + _dot(out, wo_ref[...])


def _mlstm_out(x, hf, hb, xc, z, norm_w, skip, w_out, tm):
    b, t, d = x.shape
    inner = hf.shape[-1]
    nt = t // tm
    row = lambda wd: pl.BlockSpec((None, tm, wd), lambda bi, i: (bi, i, 0))
    return pl.pallas_call(
        _mlstm_out_kernel,
        grid=(b, nt),
        in_specs=[row(d)] + [row(inner)] * 4 + [_const_spec(norm_w.shape), _const_spec(skip.shape),
                                                _const_spec(w_out.shape)],
        out_specs=row(d),
        out_shape=jax.ShapeDtypeStruct((b, t, d), _F32),
        compiler_params=_cparams(2),
        name="mlstm_out_proj",
    )(x, hf, hb, xc, z, norm_w, skip, w_out)


def _block_diag_dense(w):
    nb, bi, bo = w.shape
    eye = jnp.eye(nb, dtype=w.dtype)
    return jnp.einsum("nio,nm->nimo", w, eye).reshape(nb * bi, nb * bo)


def _block_diag_blocks(w, blk):
    nb, bi, bo = w.shape
    per = blk // bi
    return jax.vmap(_block_diag_dense)(w.reshape(nb // per, per, bi, bo))


def _row(v):
    return v.reshape(1, -1).astype(_F32)


def _tile(t_len, want):
    return min(want, t_len)


def _even_layer(x, li, p):
    (w_in, conv_w, conv_b, wa, ba, wx, bx, lam, mu, w0, w_up, a0, a_up, g_up, k_k, k_a, r_k,
     ln_w, ln_b, w_out) = [z[li] for z in p["even"]]
    t = x.shape[1]
    width = lam.shape[-1]
    c = k_k.shape[-1]
    p_lru, p_rw = _norm_proj(x, _row(p["norm_mix"][2 * li]), w_in.astype(_BF16),
                             (2 * width, w_in.shape[1] - 2 * width), (_F32, _F32),
                             _tile(t, _NORM_TILE))

    tt = _tile(t, _LRU_TILE)
    h_dir = None
    for d, rev in enumerate((False, True)):
        wg = jnp.concatenate([_block_diag_dense(wa[d]), _block_diag_dense(wx[d])], axis=1).astype(_BF16)
        bg = jnp.concatenate([ba[d], bx[d]]).reshape(1, -1)
        h_dir = _lru_pass(p_lru, conv_w, _row(conv_b), wg, bg, _row(lam[d]), h_dir, rev, tt)
    lru_out = h_dir

    zeros = jnp.zeros((_RW_HEAD, c), _F32)
    head = jnp.arange(c) // _RW_HEAD
    prm = {
        "mu": _row(mu), "k_k": _row(k_k), "k_a": _row(k_a),
        "w0": [_row(w0[d]) for d in range(2)], "a0": [_row(a0[d]) for d in range(2)],
        "w_up": [jnp.concatenate([w_up[d], zeros], axis=0).astype(_BF16) for d in range(2)],
        "a_up": [jnp.concatenate([zeros, a_up[d]], axis=0).astype(_BF16) for d in range(2)],
        "ones_r": (head[:, None] == jnp.arange(_LANES)[None, :]).astype(_BF16),
        "ones_e": (jnp.arange(_LANES)[:, None] == head[None, :]).astype(_BF16),
        "r_k": _row(r_k), "ln_w": _row(ln_w), "ln_b": _row(ln_b), "g_up": g_up.astype(_BF16),
    }
    tt = _tile(t, _RW_TILE)
    y_dir = None
    for d, rev in enumerate((False, True)):
        y_dir = _rwkv_pass(p_rw, prm, d, y_dir, rev, tt)
    rw_out = y_dir

    wo = w_out.astype(_BF16)
    return _out_proj2(x, lru_out, rw_out, wo[:width], wo[width:], _tile(t, _ROW_TILE))


def _odd_layer(x, li, p):
    (w_in, conv_w, conv_b, w_q, w_k, w_v, w_ig, b_ig, w_fg, b_fg, skip, norm_w,
     w_out) = [z[li] for z in p["odd"]]
    t = x.shape[1]
    inner = skip.shape[-1]
    nh = b_ig.shape[-1]
    wg = jnp.concatenate([w_ig[0], w_ig[1], w_fg[0], w_fg[1],
                          jnp.zeros((3 * inner, _LANES - 4 * nh), _F32)], axis=1).astype(_BF16)
    bg = jnp.concatenate([b_ig[0], b_ig[1], b_fg[0], b_fg[1], jnp.zeros((_LANES - 4 * nh,), _F32)])
    prm = {
        "conv_w": conv_w, "conv_b": _row(conv_b),
        "wq": _block_diag_blocks(w_q, _ML_BLK).astype(_BF16),
        "wk": _block_diag_blocks(w_k, _ML_BLK).astype(_BF16),
        "wv": _block_diag_blocks(w_v, _ML_BLK).astype(_BF16),
        "wg": wg, "bg": _row(bg),
    }
    q, k, v, xc, z, gt = _mlstm_in(x, _row(p["norm_mix"][2 * li + 1]), w_in.astype(_BF16), prm,
                                   _tile(t, _ML_ROW_TILE))
    lc = _tile(t, _ML_TILE)
    hf = _mlstm_pass(q, k, v, gt, 0, False, lc)
    hb = _mlstm_pass(q, k, v, gt, 1, True, lc)
    return _mlstm_out(x, hf, hb, xc, z, _row(norm_w), _row(skip), w_out.astype(_BF16),
                      _tile(t, _ML_OUT_TILE))


def _trunk(x, p):
    depth = p["norm_mix"].shape[0]
    t = x.shape[1]
    for layer in range(depth):
        if layer % 2 == 0:
            x = _even_layer(x, layer // 2, p)
        else:
            x = _odd_layer(x, layer // 2, p)
        w_up, conv_w, conv_b, w_down = [z[layer] for z in p["ffn"]]
        dff = w_down.shape[0]
        ck = _FFN_COLS if dff % _FFN_COLS == 0 else dff
        g_final = _row(p["norm_final"]) if layer == depth - 1 else None
        x = _ffn(x, _row(p["norm_ffn"][layer]), w_up.astype(_BF16), conv_w, _row(conv_b),
                 w_down.astype(_BF16), g_final, _tile(t, _ROW_TILE), ck)
    return x


def kernel(x_prompt, x_sample, ev_w_in, ev_lru_conv_w, ev_lru_conv_b, ev_lru_wa, ev_lru_ba, ev_lru_wx, ev_lru_bx, ev_lru_lambda, ev_rw_mu, ev_rw_w0, ev_rw_w_up, ev_rw_a0, ev_rw_a_up, ev_rw_g_up, ev_rw_k_k, ev_rw_k_a, ev_rw_r_k, ev_rw_ln_w, ev_rw_ln_b, ev_w_out, od_w_in, od_conv_w, od_conv_b, od_w_q, od_w_k, od_w_v, od_w_ig, od_b_ig, od_w_fg, od_b_fg, od_skip, od_norm_w, od_w_out, ffn_w_up, ffn_conv_w, ffn_conv_b, ffn_w_down, norm_mix, norm_ffn, norm_final):
    p = {
        "even": (ev_w_in, ev_lru_conv_w, ev_lru_conv_b, ev_lru_wa, ev_lru_ba, ev_lru_wx, ev_lru_bx,
                 ev_lru_lambda, ev_rw_mu, ev_rw_w0, ev_rw_w_up, ev_rw_a0, ev_rw_a_up, ev_rw_g_up,
                 ev_rw_k_k, ev_rw_k_a, ev_rw_r_k, ev_rw_ln_w, ev_rw_ln_b, ev_w_out),
        "odd": (od_w_in, od_conv_w, od_conv_b, od_w_q, od_w_k, od_w_v, od_w_ig, od_b_ig, od_w_fg,
                od_b_fg, od_skip, od_norm_w, od_w_out),
        "ffn": (ffn_w_up, ffn_conv_w, ffn_conv_b, ffn_w_down),
        "norm_mix": norm_mix, "norm_ffn": norm_ffn, "norm_final": norm_final,
    }
    return (_trunk(x_prompt, p), _trunk(x_sample, p))
```

```python
import functools
import math

import jax
import jax.numpy as jnp
from jax import lax
from jax.experimental import pallas as pl
from jax.experimental.pallas import tpu as pltpu

_F32 = jnp.float32
_BF16 = jnp.bfloat16

_RMS_EPS = 1e-6
_LRU_C = 8.0
_RW_HEAD = 64
_RW_DECAY_SCALE = math.exp(-0.5)
_RW_GN_EPS = 64e-5
_ML_HEADS = 4
_ML_LN_EPS = 1e-5
_ML_BLK = 256

_SUBLANES = 8
_LANES = 128
_VMEM_LIMIT = 56 * 1024 * 1024
_ROW_TILE = 512
_NORM_TILE = 512
_LRU_TILE = 1024
_RW_TILE = 256
_RW_CHUNK = 64
_RW_GROUP = 256
_RW_SEQS = 2
_ML_ROW_TILE = 512
_ML_IN_COLS = 512
_ML_OUT_TILE = 512
_ML_TILE = 256
_FFN_COLS = 256


def _cparams(n_axes):
    return pltpu.CompilerParams(dimension_semantics=("arbitrary",) * n_axes,
                                vmem_limit_bytes=_VMEM_LIMIT)


def _dot(a, b):
    return jnp.dot(a.astype(_BF16), b.astype(_BF16), preferred_element_type=_F32)


def _dot_nt(a, b):
    return lax.dot_general(a.astype(_BF16), b.astype(_BF16), (((1,), (1,)), ((), ())),
                           preferred_element_type=_F32)


def _dot_tn(a, b):
    return lax.dot_general(a.astype(_BF16), b.astype(_BF16), (((0,), (0,)), ((), ())),
                           preferred_element_type=_F32)


def _split3(x):
    hi = x.astype(_BF16)
    r1 = x - hi.astype(_F32)
    mid = r1.astype(_BF16)
    lo = (r1 - mid.astype(_F32)).astype(_BF16)
    return hi, mid, lo


def _dot01_left(m01, x):
    hi, mid, lo = _split3(x)
    f = lambda p: jnp.dot(m01, p, preferred_element_type=_F32)
    return (f(lo) + f(mid)) + f(hi)


def _dot01_right(x, m01):
    hi = x.astype(_BF16)
    lo = (x - hi.astype(_F32)).astype(_BF16)
    f = lambda p: jnp.dot(p, m01, preferred_element_type=_F32)
    return f(lo) + f(hi)


def _sigmoid(x):
    return 0.5 * jnp.tanh(0.5 * x) + 0.5


def _softplus(x):
    return jnp.maximum(x, 0.0) + jnp.log1p(jnp.exp(-jnp.abs(x)))


def _gelu_tanh(x):
    c = math.sqrt(2.0 / math.pi)
    return x * (0.5 * (1.0 + jnp.tanh(c * (x + 0.044715 * (x * x * x)))))


def _rms(x, g):
    ms = jnp.mean(x * x, axis=-1, keepdims=True)
    return x * lax.rsqrt(ms + _RMS_EPS) * g


def _halo_shifter(main, prev, nxt, ti=None, nt=None):
    rows = main.shape[0]
    if ti is not None:
        prev = jnp.where(ti == 0, 0.0, prev)
        nxt = jnp.where(ti == nt - 1, 0.0, nxt)
    ridx = lax.broadcasted_iota(jnp.int32, main.shape, 0)
    first, second, last = ridx == 0, ridx == 1, ridx == rows - 1

    def shift(k):
        rolled = pltpu.roll(main, k % rows, axis=0)
        if k == 1:
            return jnp.where(first, prev[_SUBLANES - 1:_SUBLANES], rolled)
        if k == 2:
            return jnp.where(first, prev[_SUBLANES - 2:_SUBLANES - 1],
                             jnp.where(second, prev[_SUBLANES - 1:_SUBLANES], rolled))
        assert k == -1
        return jnp.where(last, nxt[0:1], rolled)

    return shift


def _shifted(ext, k, rows, halo=_SUBLANES):
    if k == 0:
        return ext[halo:halo + rows]
    return pltpu.roll(ext, k % ext.shape[0], axis=0)[halo:halo + rows]


def _time_specs(width, rows, nt, t_len, reverse, col=0, bb=None):
    per = rows // _SUBLANES
    last = t_len // _SUBLANES - 1

    def tidx(i):
        return (nt - 1 - i) if reverse else i

    main = pl.BlockSpec((bb, rows, width), lambda b, i: (b, tidx(i), col))
    prev = pl.BlockSpec((bb, _SUBLANES, width),
                        lambda b, i: (b, jnp.maximum(tidx(i) * per - 1, 0), col))
    nxt = pl.BlockSpec((bb, _SUBLANES, width),
                       lambda b, i: (b, jnp.minimum((tidx(i) + 1) * per, last), col))
    return main, prev, nxt


def _const_spec(shape):
    nd = len(shape)
    return pl.BlockSpec(shape, lambda *_: (0,) * nd)


def _resident_spec(shape):
    nd = len(shape)
    return pl.BlockSpec(shape, lambda *_: (0,) * nd, pipeline_mode=pl.Buffered(1))


def _norm_proj_kernel(x_ref, g_ref, w_ref, *o_refs):
    h = _rms(x_ref[...], g_ref[...])
    res = _dot(h, w_ref[...])
    off = 0
    for o in o_refs:
        wdt = o.shape[-1]
        o[...] = res[:, off:off + wdt].astype(o.dtype)
        off += wdt


def _norm_proj(x, g, w, splits, dtypes, tm):
    b, t, d = x.shape
    n = w.shape[1]
    assert sum(splits) == n
    nt = t // tm
    row = lambda wd: pl.BlockSpec((None, tm, wd), lambda bi, i: (bi, i, 0))
    return pl.pallas_call(
        _norm_proj_kernel,
        grid=(b, nt),
        in_specs=[row(d), _const_spec((1, d)), _resident_spec((d, n))],
        out_specs=[row(s) for s in splits],
        out_shape=[jax.ShapeDtypeStruct((b, t, s), dt) for s, dt in zip(splits, dtypes)],
        compiler_params=_cparams(2),
        name="norm_proj",
    )(x, g, w)


def _lru_kernel(*refs, reverse, finalize, tt, nt):
    x_ref, xp_ref, xn_ref, cw_ref, cb_ref, wg_ref, bg_ref, lam_ref = refs[:8]
    if finalize:
        hf_ref, gate_ref, o_ref, a_s, u_s, h_s, carry = refs[8:]
    else:
        o_ref, a_s, u_s, h_s, carry = refs[8:]
    i = pl.program_id(1)
    ti = (nt - 1 - i) if reverse else i
    nseq, _, width = x_ref.shape
    nslab = width // _LANES
    pitch = a_s.shape[1] // nseq

    cw = cw_ref[...]
    neg_c_sp = (-_LRU_C) * _softplus(-lam_ref[...])
    xcs = []
    for b in range(nseq):
        xb = x_ref[b]
        shift = _halo_shifter(xb, xp_ref[b], xn_ref[b], ti, nt)
        xc = cb_ref[...] + shift(2) * cw[0:1]
        xc = xc + shift(1) * cw[1:2]
        xc = xc + xb * cw[2:3]
        xcs.append(xc + shift(-1) * cw[3:4])
    gates_all = _dot(jnp.concatenate(xcs, axis=0), wg_ref[...]) + bg_ref[...]
    for b in range(nseq):
        gates = gates_all[b * tt:(b + 1) * tt]
        r = _sigmoid(gates[:, :width])
        ig = _sigmoid(gates[:, width:])
        log_a = r * neg_c_sp
        th = jnp.tanh(log_a)
        a = jnp.exp(log_a)
        u = jnp.sqrt(-2.0 * th) * lax.rsqrt(1.0 - th) * (ig * xcs[b])
        for j in range(nslab):
            a_s[j, pl.ds(b * pitch, tt), :] = a[:, j * _LANES:(j + 1) * _LANES]
            u_s[j, pl.ds(b * pitch, tt), :] = u[:, j * _LANES:(j + 1) * _LANES]

    @pl.when(i == 0)
    def _():
        carry[...] = jnp.zeros_like(carry)

    def body(s, hs):
        t = (tt - 1 - s) if reverse else s
        new = []
        for j in range(nslab):
            rows = pl.ds(t, nseq, stride=pitch)
            h = a_s[j, rows, :] * hs[j] + u_s[j, rows, :]
            h_s[j, rows, :] = h
            new.append(h)
        return tuple(new)

    hs = lax.fori_loop(0, tt, body, tuple(carry[j] for j in range(nslab)), unroll=_SUBLANES)
    for j in range(nslab):
        carry[j] = hs[j]

    for b in range(nseq):
        h = jnp.concatenate([h_s[j, pl.ds(b * pitch, tt), :] for j in range(nslab)], axis=1)
        if finalize:
            h = (hf_ref[b] + h) * _gelu_tanh(gate_ref[b])
        o_ref[b] = h


def _lru_pass(p_lru, conv_w, conv_b, wg, bg, lam, hf, reverse, tt):
    b, t, w2 = p_lru.shape
    w = w2 // 2
    nseq = max(n for n in (8, 4, 2, 1) if b % n == 0)
    tt = _tile(t, max(tt // nseq, _LANES))
    nt = t // tt
    pitch = tt + _SUBLANES if (tt // _SUBLANES) % 2 == 0 else tt
    nslab = w // _LANES
    finalize = hf is not None
    main, prev, nxt = _time_specs(w, tt, nt, t, reverse, bb=nseq)
    tidx = (lambda i: nt - 1 - i) if reverse else (lambda i: i)
    tile = lambda col: pl.BlockSpec((nseq, tt, w), lambda bi, i: (bi, tidx(i), col))
    in_specs = [main, prev, nxt, _const_spec(conv_w.shape), _const_spec(conv_b.shape),
                _const_spec(wg.shape), _const_spec(bg.shape), _const_spec(lam.shape)]
    args = [p_lru, p_lru, p_lru, conv_w, conv_b, wg, bg, lam]
    scratch = [pltpu.VMEM((nslab, nseq * pitch, _LANES), _F32)] * 3
    if finalize:
        in_specs += [tile(0), tile(1)]
        args += [hf, p_lru]
    scratch += [pltpu.VMEM((nslab, nseq, _LANES), _F32)]
    return pl.pallas_call(
        functools.partial(_lru_kernel, reverse=reverse, finalize=finalize, tt=tt, nt=nt),
        grid=(b // nseq, nt),
        in_specs=in_specs,
        out_specs=tile(0),
        out_shape=jax.ShapeDtypeStruct((b, t, w), _F32),
        scratch_shapes=scratch,
        compiler_params=_cparams(2),
        name="lru_bwd" if reverse else "lru_fwd",
    )(*args)


def _rwkv_kernel(*refs, reverse, finalize, tt, nt):
    (rw_ref, rwp_ref, rwn_ref, mu_ref, kk_ref, ka_ref, w0_ref, a0_ref, wup_ref, aup_ref,
     onesr_ref, onese_ref) = refs[:12]
    if finalize:
        yf_ref, rk_ref, lnw_ref, lnb_ref, gup_ref, o_ref, ht_s = refs[12:]
    else:
        o_ref, ht_s = refs[12:]
    i = pl.program_id(1)
    ti = (nt - 1 - i) if reverse else i
    nseq, ngrp, gw, _ = ht_s.shape
    c = ngrp * gw
    lc = _RW_CHUNK
    hpg = gw // _RW_HEAD
    ones_r = onesr_ref[...]
    ones_e = onese_ref[...]
    seqs = range(nseq)

    def segsum(x):
        return _dot01_right(_dot01_right(x, ones_r), ones_e)

    tok = []
    for b in seqs:
        x = rw_ref[b]
        tshift = _halo_shifter(x, rwp_ref[b], rwn_ref[b], ti, nt)
        rws = x + mu_ref[...] * (0.5 * (tshift(1) + tshift(-1)) - x)
        r = rws[:, 0:c]
        k = rws[:, c:2 * c]
        v = rws[:, 2 * c:3 * c]
        wad = rws[:, 3 * c:3 * c + _LANES]
        kk = k * kk_ref[...]
        kk = kk * jnp.minimum(lax.rsqrt(segsum(kk * kk)), 1e12)
        lw = (-_RW_DECAY_SCALE) * _sigmoid(w0_ref[...] + _dot(jnp.tanh(wad), wup_ref[...]))
        a = _sigmoid(a0_ref[...] + _dot(wad, aup_ref[...]))
        tok.append(dict(rws=rws, r=r, k=k, v=v, kk=kk, lw=lw, kt=k * (1.0 + (a - 1.0) * ka_ref[...]),
                        bvec=a * kk))

    shift = _RW_HEAD.bit_length() - 1
    row = lax.broadcasted_iota(jnp.int32, (gw, gw), 0)
    col = lax.broadcasted_iota(jnp.int32, (gw, gw), 1)
    same = (row >> shift) == (col >> shift)
    trow = lax.broadcasted_iota(jnp.int32, (lc, gw), 0)
    tcol = lax.broadcasted_iota(jnp.int32, (lc, gw), 1) & (lc - 1)
    m_incl = (tcol >= trow) if reverse else (tcol <= trow)
    m_strict = (tcol > trow) if reverse else (tcol < trow)
    eye_cat = jnp.where(tcol == trow, 1.0, 0.0)
    srow = lax.broadcasted_iota(jnp.int32, (lc, lc), 0)
    scol = lax.broadcasted_iota(jnp.int32, (lc, lc), 1)
    tri01 = ((scol >= srow) if reverse else (scol <= srow)).astype(_BF16)

    def stack(xv):
        xb = xv.astype(_BF16)
        return jnp.where(same, jnp.concatenate([xb] * hpg, axis=0), jnp.zeros((), _BF16))

    @pl.when(i == 0)
    def _():
        ht_s[...] = jnp.zeros_like(ht_s)

    nchunk = tt // lc
    chunks = list(range(nchunk - 1, -1, -1) if reverse else range(nchunk))
    groups = range(ngrp)
    lanes = [slice(g * gw, (g + 1) * gw) for g in groups]
    units = [(b, ci, g) for ci in chunks for b in seqs for g in groups]

    pre = {}
    for ci in chunks:
        sl = slice(ci * lc, (ci + 1) * lc)
        for b in seqs:
            tb = tok[b]
            r_c, lw_c, k_c, v_c, kk_c, b_c = (tb[n][sl] for n in ("r", "lw", "kt", "v", "kk", "bvec"))
            cum = _dot01_left(tri01, lw_c)
            clast = cum[0:1] if reverse else cum[lc - 1:lc]
            einv = jnp.exp(-cum)
            eend = jnp.exp(clast - cum)
            pre[b, ci] = dict(gamma=jnp.exp(clast), rg=r_c * jnp.exp(cum), kkg=kk_c * jnp.exp(cum - lw_c),
                              kd=k_c * einv, bd=b_c * einv, kend=k_c * eend, bend=b_c * eend, v=v_c)

    st = {}
    for b, ci, g in units:
        ls = lanes[g]
        pc = pre[b, ci]
        lhs = jnp.concatenate([pc["kkg"][:, ls], pc["rg"][:, ls]], axis=0)
        rhs = jnp.concatenate([stack(pc["kd"][:, ls]), stack(pc["bd"][:, ls])], axis=0)
        amat = _dot_nt(lhs, rhs)
        nj = -jnp.where(m_strict, amat[:lc, gw:], 0.0)
        st[b, ci, g] = dict(a_kk_k=jnp.where(m_strict, amat[:lc, :gw], 0.0),
                            a_r_k=jnp.where(m_incl, amat[lc:, :gw], 0.0),
                            a_r_b=jnp.where(m_incl, amat[lc:, gw:], 0.0),
                            nj=nj, tinv=eye_cat + nj)
    for key in units:
        s = st[key]
        s["nj"] = _dot(s["nj"], stack(s["nj"]))
    for _ in range(lc.bit_length() - 3):
        for key in units:
            s = st[key]
            pn = _dot(jnp.concatenate([s["tinv"], s["nj"]], axis=0), stack(s["nj"]))
            s["tinv"] = s["tinv"] + pn[:lc]
            s["nj"] = pn[lc:]
    for key in units:
        s = st[key]
        s["tinv"] = s["tinv"] + _dot(s["tinv"], stack(s["nj"]))
    for b, ci, g in units:
        s = st[b, ci, g]
        s["av"] = _dot(jnp.concatenate([s["a_kk_k"], s["a_r_k"]], axis=0), stack(pre[b, ci]["v"][:, lanes[g]]))
    for b, ci, g in units:
        s = st[b, ci, g]
        tk = _dot(s["tinv"], jnp.concatenate([stack(pre[b, ci]["kkg"][:, lanes[g]]), stack(s["av"][:lc])],
                                             axis=1))
        s["kkgp"], s["uv"] = tk[:, :gw], tk[:, gw:]
    for b, ci, g in units:
        ls = lanes[g]
        s = st[b, ci, g]
        pc = pre[b, ci]
        s["kb"] = jnp.where(same, _dot_tn(s["kkgp"], pc["bend"][:, ls]), 0.0)
        s["gmat"] = jnp.where(same, _dot_tn(jnp.concatenate([pc["v"][:, ls], -s["uv"]], axis=0),
                                            jnp.concatenate([pc["kend"][:, ls], pc["bend"][:, ls]], axis=0)),
                              0.0)

    chains = [(b, g) for b in seqs for g in groups]
    hts = {key: ht_s[key[0], key[1]] for key in chains}
    ys = {}
    for ci in chunks:
        cur = dict(hts)
        for b, g in chains:
            s = st[b, ci, g]
            hts[b, g] = cur[b, g] * pre[b, ci]["gamma"][:, lanes[g]] - _dot(cur[b, g], s["kb"]) + s["gmat"]
        pus = {(b, g): _dot_nt(jnp.concatenate([st[b, ci, g]["kkgp"], pre[b, ci]["rg"][:, lanes[g]]], axis=0),
                               cur[b, g]) for b, g in chains}
        us = {key: pus[key][:lc] + st[key[0], ci, key[1]]["uv"] for key in chains}
        for b in seqs:
            ys[b, ci] = jnp.concatenate(
                [pus[b, g][lc:] + st[b, ci, g]["av"][lc:] - _dot(st[b, ci, g]["a_r_b"], stack(us[b, g]))
                 for g in groups], axis=1)
    for b, g in chains:
        ht_s[b, g] = hts[b, g]

    for b in seqs:
        y = jnp.concatenate([ys[b, ci] for ci in range(nchunk)], axis=0)
        if finalize:
            tb = tok[b]
            y = yf_ref[b] + y
            inv_n = 1.0 / _RW_HEAD
            mean = segsum(y) * inv_n
            d = y - mean
            var = segsum(d * d) * inv_n
            yn = d * lax.rsqrt(var + _RW_GN_EPS) * lnw_ref[...] + lnb_ref[...]
            bonus = segsum(tb["r"] * tb["k"] * rk_ref[...]) * tb["v"]
            gd = tb["rws"][:, 3 * c + _LANES:3 * c + 2 * _LANES]
            y = (yn + bonus) * _dot(_sigmoid(gd), gup_ref[...])
        o_ref[b] = y


def _rwkv_pass(p_rw, prm, d, yf, reverse, tt):
    b, t, cols = p_rw.shape
    c = prm["k_k"].shape[-1]
    nt = t // tt
    nseq = _RW_SEQS if b % _RW_SEQS == 0 else 1
    finalize = yf is not None
    main, prev, nxt = _time_specs(cols, tt, nt, t, reverse, bb=nseq)
    tidx = (lambda i: nt - 1 - i) if reverse else (lambda i: i)
    tile = pl.BlockSpec((nseq, tt, c), lambda bi, i: (bi, tidx(i), 0))
    consts = [prm["mu"], prm["k_k"], prm["k_a"], prm["w0"][d], prm["a0"][d], prm["w_up"][d],
              prm["a_up"][d], prm["ones_r"], prm["ones_e"]]
    args = [p_rw, p_rw, p_rw] + consts
    in_specs = [main, prev, nxt] + [_const_spec(z.shape) for z in consts]
    if finalize:
        extra = [prm["r_k"], prm["ln_w"], prm["ln_b"], prm["g_up"]]
        args += [yf] + extra
        in_specs += [tile] + [_const_spec(z.shape) for z in extra]
    return pl.pallas_call(
        functools.partial(_rwkv_kernel, reverse=reverse, finalize=finalize, tt=tt, nt=nt),
        grid=(b // nseq, nt),
        in_specs=in_specs,
        out_specs=tile,
        out_shape=jax.ShapeDtypeStruct((b, t, c), _F32),
        scratch_shapes=[pltpu.VMEM((nseq, c // _RW_GROUP, _RW_GROUP, _RW_GROUP), _F32)],
        compiler_params=_cparams(2),
        name="rwkv_bwd" if reverse else "rwkv_fwd",
    )(*args)


def _out_proj2_kernel(x_ref, a_ref, b_ref, wa_ref, wb_ref, o_ref):
    o_ref[...] = x_ref[...] + _dot(a_ref[...], wa_ref[...]) + _dot(b_ref[...], wb_ref[...])


def _out_proj2(x, a, bm, wa, wb, tm):
    b, t, d = x.shape
    nt = t // tm
    row = lambda wd: pl.BlockSpec((None, tm, wd), lambda bi, i: (bi, i, 0))
    return pl.pallas_call(
        _out_proj2_kernel,
        grid=(b, nt),
        in_specs=[row(d), row(a.shape[-1]), row(bm.shape[-1]), _const_spec(wa.shape),
                  _const_spec(wb.shape)],
        out_specs=row(d),
        out_shape=jax.ShapeDtypeStruct((b, t, d), _F32),
        compiler_params=_cparams(2),
        name="even_out_proj",
    )(x, a, bm, wa, wb)


def _ffn_kernel(*refs, tm, nt, ck, final_norm):
    x_ref, xp_ref, xn_ref, g_ref, wup_ref, cw_ref, cb_ref, wd_ref = refs[:8]
    if final_norm:
        gf_ref, o_ref, h_s, act_s = refs[8:]
    else:
        o_ref, h_s, act_s = refs[8:]
    i = pl.program_id(1)
    dff = wd_ref.shape[0]
    halo = 2 * _SUBLANES
    g = g_ref[...]
    zeros = jnp.zeros((_SUBLANES, x_ref.shape[-1]), _F32)
    prev = jnp.where(i == 0, 0.0, _rms(xp_ref[...], g))
    nxt = jnp.where(i == nt - 1, 0.0, _rms(xn_ref[...], g))
    h_s[...] = jnp.concatenate([zeros, prev, _rms(x_ref[...], g), nxt, zeros], axis=0).astype(_BF16)
    hext = h_s[...]
    hmain = h_s[pl.ds(halo, tm), :]
    cw = cw_ref[...]
    cb = cb_ref[...]
    for c in range(dff // ck):
        cs = slice(c * ck, (c + 1) * ck)
        ug = jnp.dot(hext, wup_ref[:, dff + c * ck:dff + (c + 1) * ck], preferred_element_type=_F32)
        gate = cb[:, cs] + _shifted(ug, 1, tm, halo) * cw[0:1, cs]
        gate = gate + _shifted(ug, 0, tm, halo) * cw[1:2, cs]
        gate = gate + _shifted(ug, -1, tm, halo) * cw[2:3, cs]
        val = jnp.dot(hmain, wup_ref[:, cs], preferred_element_type=_F32)
        act_s[:, cs] = (_gelu_tanh(gate) * val).astype(_BF16)
    y = x_ref[...] + jnp.dot(act_s[...], wd_ref[...], preferred_element_type=_F32)
    if final_norm:
        y = _rms(y, gf_ref[...])
    o_ref[...] = y


def _ffn(x, g, w_up, conv_w, conv_b, w_down, g_final, tm, ck):
    b, t, d = x.shape
    dff = w_down.shape[0]
    nt = t // tm
    final_norm = g_final is not None
    main, prev, nxt = _time_specs(d, tm, nt, t, False)
    consts = [g, w_up, conv_w, conv_b, w_down] + ([g_final] if final_norm else [])
    return pl.pallas_call(
        functools.partial(_ffn_kernel, tm=tm, nt=nt, ck=ck, final_norm=final_norm),
        grid=(b, nt),
        in_specs=[main, prev, nxt] + [_resident_spec(z.shape) for z in consts],
        out_specs=main,
        out_shape=jax.ShapeDtypeStruct((b, t, d), _F32),
        scratch_shapes=[pltpu.VMEM((tm + 4 * _SUBLANES, d), _BF16), pltpu.VMEM((tm, dff), _BF16)],
        compiler_params=_cparams(2),
        name="conv_ffn",
    )(x, x, x, *consts)


def _mlstm_in_kernel(x_ref, xp_ref, xn_ref, g_ref, w_ref, cw_ref, cb_ref, wq_ref, wk_ref, wv_ref,
                     wg_ref, bg_ref, q_ref, k_ref, v_ref, xc_ref, z_ref, gt_ref, *, tm, nt, cols):
    i = pl.program_id(1)
    inner = z_ref.shape[-1]
    g = g_ref[...]
    hext = jnp.concatenate([_rms(xp_ref[...], g), _rms(x_ref[...], g), _rms(xn_ref[...], g)], axis=0)
    hext_b = hext.astype(_BF16)
    hmain_b = hext[_SUBLANES:_SUBLANES + tm].astype(_BF16)
    cw = cw_ref[...]
    cb = cb_ref[...]
    qscale = (inner // _ML_HEADS) ** -0.5
    nblk = inner // cols
    proj = lambda j: jnp.dot(hext_b, w_ref[:, j * cols:(j + 1) * cols], preferred_element_type=_F32)
    gates = bg_ref[...]
    xme_next = proj(0)
    for j in range(nblk):
        cs = slice(j * cols, (j + 1) * cols)
        xme = xme_next
        if j + 1 < nblk:
            xme_next = proj(j + 1)
        z_ref[:, cs] = jnp.dot(hmain_b, w_ref[:, inner + j * cols:inner + (j + 1) * cols],
                               preferred_element_type=_F32).astype(z_ref.dtype)
        prev = jnp.where(i == 0, 0.0, xme[:_SUBLANES])
        nxt = jnp.where(i == nt - 1, 0.0, xme[_SUBLANES + tm:])
        xm = xme[_SUBLANES:_SUBLANES + tm]
        shift = _halo_shifter(xm, prev, nxt)
        zc = cb[:, cs] + shift(2) * cw[0:1, cs]
        zc = zc + shift(1) * cw[1:2, cs]
        zc = zc + xm * cw[2:3, cs]
        zc = zc + shift(-1) * cw[3:4, cs]
        xc = zc * _sigmoid(zc)
        xc_ref[:, cs] = xc.astype(xc_ref.dtype)
        for jb in range(cols // _ML_BLK):
            blk = j * (cols // _ML_BLK) + jb
            sub = slice(jb * _ML_BLK, (jb + 1) * _ML_BLK)
            sl = slice(blk * _ML_BLK, (blk + 1) * _ML_BLK)
            q = _dot(xc[:, sub], wq_ref[blk])
            k = _dot(xc[:, sub], wk_ref[blk])
            v = _dot(xm[:, sub], wv_ref[blk])
            gates = (gates + _dot(q, wg_ref[sl])
                     + _dot(k, wg_ref[inner + blk * _ML_BLK:inner + (blk + 1) * _ML_BLK])
                     + _dot(v, wg_ref[2 * inner + blk * _ML_BLK:2 * inner + (blk + 1) * _ML_BLK]))
            q_ref[:, sl] = (q * qscale).astype(q_ref.dtype)
            k_ref[:, sl] = k.astype(k_ref.dtype)
            v_ref[:, sl] = v.astype(v_ref.dtype)
    gt_ref[...] = gates


def _mlstm_in(x, g, w_in, prm, tm):
    b, t, d = x.shape
    inner = w_in.shape[1] // 2
    nt = t // tm
    cols = _ML_IN_COLS if inner % _ML_IN_COLS == 0 else inner
    main, prev, nxt = _time_specs(d, tm, nt, t, False)
    consts = [g, w_in, prm["conv_w"], prm["conv_b"], prm["wq"], prm["wk"], prm["wv"], prm["wg"], prm["bg"]]
    row = lambda wd: pl.BlockSpec((None, tm, wd), lambda bi, i: (bi, i, 0))
    big = jax.ShapeDtypeStruct((b, t, inner), _BF16)
    return pl.pallas_call(
        functools.partial(_mlstm_in_kernel, tm=tm, nt=nt, cols=cols),
        grid=(b, nt),
        in_specs=[main, prev, nxt] + [_resident_spec(z.shape) for z in consts],
        out_specs=[row(inner)] * 5 + [row(_LANES)],
        out_shape=[big] * 5 + [jax.ShapeDtypeStruct((b, t, _LANES), _F32)],
        compiler_params=_cparams(2),
        name="mlstm_in",
    )(x, x, x, *consts)


def _mlstm_kernel(q_ref, k_ref, v_ref, gt_ref, o_ref, c_s, n_s, m_s, *, reverse, d, lc):
    i = pl.program_id(1)
    dh = c_s.shape[-1]

    @pl.when(i == 0)
    def _():
        c_s[...] = jnp.zeros_like(c_s)
        n_s[...] = jnp.zeros_like(n_s)
        m_s[...] = jnp.zeros_like(m_s)

    gates = gt_ref[...]
    lane = lax.broadcasted_iota(jnp.int32, gates.shape, 1)
    row = lax.broadcasted_iota(jnp.int32, (lc, lc), 0)
    col = lax.broadcasted_iota(jnp.int32, (lc, lc), 1)
    tri = (col >= row) if reverse else (col <= row)
    bcum_all = _dot01_left(tri.astype(_BF16), -_softplus(-gates))

    def lane_col(mat, j):
        return jnp.sum(jnp.where(lane == j, mat, 0.0), axis=-1, keepdims=True)

    hs = range(_ML_HEADS)
    sls = [slice(h * dh, (h + 1) * dh) for h in hs]
    li = [lane_col(gates, d * _ML_HEADS + h) for h in hs]
    bcum = [lane_col(bcum_all, 2 * _ML_HEADS + d * _ML_HEADS + h) for h in hs]
    m_prev = [m_s[h][:, 0:1] for h in hs]
    src_row = [jnp.broadcast_to(li[h] - bcum[h], (lc, lc)).T for h in hs]
    dmat = [jnp.where(tri, bcum[h] + src_row[h], -jnp.inf) for h in hs]
    inter = [bcum[h] + m_prev[h] for h in hs]
    m_t = [jnp.maximum(inter[h], jnp.max(dmat[h], axis=-1, keepdims=True)) for h in hs]
    g = [bcum[h][0:1] if reverse else bcum[h][lc - 1:lc] for h in hs]
    wlog = [g[h] - bcum[h] + li[h] for h in hs]
    m_new = [jnp.maximum(g[h] + m_prev[h], jnp.max(wlog[h], axis=0, keepdims=True)) for h in hs]
    wexp = [jnp.exp(wlog[h] - m_new[h]) for h in hs]
    wexp_row = [jnp.exp(g[h] + src_row[h][0:2 * _SUBLANES] - m_new[h]) for h in hs]
    decay = [jnp.exp(g[h] + m_prev[h] - m_new[h]) for h in hs]
    q = [q_ref[:, sls[h]] for h in hs]
    k = [k_ref[:, sls[h]] for h in hs]
    v = [v_ref[:, sls[h]] for h in hs]
    cmat = [c_s[h] for h in hs]
    nvec = [n_s[h] for h in hs]
    s_raw = [_dot_nt(q[h], k[h]) for h in hs]
    cq = [_dot_nt(q[h], cmat[h]) for h in hs]
    qn = [_dot_nt(q[h], jnp.broadcast_to(nvec[h], (_LANES, dh)))[:, 0:1] for h in hs]
    upd = [_dot_tn(v[h] * wexp[h].astype(v[h].dtype), k[h]) for h in hs]
    nupd = [_dot(wexp_row[h], k[h])[0:1] for h in hs]
    s = [s_raw[h] * jnp.exp(dmat[h] - m_t[h]) for h in hs]
    pv = [_dot(s[h], v[h]) for h in hs]
    for h in hs:
        scale = jnp.exp(inter[h] - m_t[h])
        num = pv[h] + scale * cq[h]
        den = jnp.sum(s[h], axis=-1, keepdims=True) + scale * qn[h]
        o_ref[:, sls[h]] = (num / jnp.maximum(jnp.abs(den), jnp.exp(-m_t[h]))).astype(o_ref.dtype)
        c_s[h] = decay[h] * cmat[h] + upd[h]
        n_s[h] = decay[h] * nvec[h] + nupd[h]
        m_s[h] = jnp.broadcast_to(m_new[h], m_s.shape[1:])


def _mlstm_pass(q, k, v, gt, d, reverse, lc):
    b, t, inner = q.shape
    nc = t // lc
    dh = inner // _ML_HEADS
    tidx = (lambda i: nc - 1 - i) if reverse else (lambda i: i)
    row = lambda wd: pl.BlockSpec((None, lc, wd), lambda bi, i: (bi, tidx(i), 0))
    return pl.pallas_call(
        functools.partial(_mlstm_kernel, reverse=reverse, d=d, lc=lc),
        grid=(b, nc),
        in_specs=[row(inner)] * 3 + [row(_LANES)],
        out_specs=row(inner),
        out_shape=jax.ShapeDtypeStruct((b, t, inner), _BF16),
        scratch_shapes=[pltpu.VMEM((_ML_HEADS, dh, dh), _F32), pltpu.VMEM((_ML_HEADS, 1, dh), _F32),
                        pltpu.VMEM((_ML_HEADS, 1, _LANES), _F32)],
        compiler_params=_cparams(2),
        name="mlstm_bwd" if reverse else "mlstm_fwd",
    )(q, k, v, gt)


def _mlstm_out_kernel(x_ref, hf_ref, hb_ref, xc_ref, z_ref, nw_ref, skip_ref, wo_ref, o_ref):
    inner = hf_ref.shape[-1]
    dh = inner // _ML_HEADS
    acc = x_ref[...]
    for h in range(_ML_HEADS):
        sl = slice(h * dh, (h + 1) * dh)
        hh = hf_ref[:, sl].astype(_F32) + hb_ref[:, sl].astype(_F32)
        mean = jnp.mean(hh, axis=-1, keepdims=True)
        dlt = hh - mean
        var = jnp.mean(dlt * dlt, axis=-1, keepdims=True)
        hn = dlt * lax.rsqrt(var + _ML_LN_EPS) * nw_ref[:, sl]
        z = z_ref[:, sl].astype(_F32)
        out = (hn + skip_ref[:, sl] * xc_ref[:, sl].astype(_F32)) * (z * _sigmoid(z))
        acc = acc + _dot(out, wo_ref[sl, :])
    o_ref[...] = acc


def _mlstm_out(x, hf, hb, xc, z, norm_w, skip, w_out, tm):
    b, t, d = x.shape
    inner = hf.shape[-1]
    nt = t // tm
    row = lambda wd: pl.BlockSpec((None, tm, wd), lambda bi, i: (bi, i, 0))
    return pl.pallas_call(
        _mlstm_out_kernel,
        grid=(b, nt),
        in_specs=[row(d)] + [row(inner)] * 4 + [_const_spec(norm_w.shape), _const_spec(skip.shape),
                                                _const_spec(w_out.shape)],
        out_specs=row(d),
        out_shape=jax.ShapeDtypeStruct((b, t, d), _F32),
        compiler_params=_cparams(2),
        name="mlstm_out_proj",
    )(x, hf, hb, xc, z, norm_w, skip, w_out)


def _block_diag_dense(w):
    nb, bi, bo = w.shape
    eye = jnp.eye(nb, dtype=w.dtype)
    return jnp.einsum("nio,nm->nimo", w, eye).reshape(nb * bi, nb * bo)


def _block_diag_blocks(w, blk):
    nb, bi, bo = w.shape
    per = blk // bi
    return jax.vmap(_block_diag_dense)(w.reshape(nb // per, per, bi, bo))


def _row(v):
    return v.reshape(1, -1).astype(_F32)


def _tile(t_len, want):
    return min(want, t_len)


def _even_layer(x, li, p):
    (w_in, conv_w, conv_b, wa, ba, wx, bx, lam, mu, w0, w_up, a0, a_up, g_up, k_k, k_a, r_k,
     ln_w, ln_b, w_out) = [z[li] for z in p["even"]]
    t = x.shape[1]
    width = lam.shape[-1]
    c = k_k.shape[-1]
    p_lru, p_rw = _norm_proj(x, _row(p["norm_mix"][2 * li]), w_in.astype(_BF16),
                             (2 * width, w_in.shape[1] - 2 * width), (_F32, _F32),
                             _tile(t, _NORM_TILE))

    tt = _tile(t, _LRU_TILE)
    h_dir = None
    for d, rev in enumerate((False, True)):
        wg = jnp.concatenate([_block_diag_dense(wa[d]), _block_diag_dense(wx[d])], axis=1).astype(_BF16)
        bg = jnp.concatenate([ba[d], bx[d]]).reshape(1, -1)
        h_dir = _lru_pass(p_lru, conv_w, _row(conv_b), wg, bg, _row(lam[d]), h_dir, rev, tt)
    lru_out = h_dir

    zeros = jnp.zeros((_RW_HEAD, c), _F32)
    head = jnp.arange(c) // _RW_HEAD
    prm = {
        "mu": _row(mu), "k_k": _row(k_k), "k_a": _row(k_a),
        "w0": [_row(w0[d]) for d in range(2)], "a0": [_row(a0[d]) for d in range(2)],
        "w_up": [jnp.concatenate([w_up[d], zeros], axis=0).astype(_BF16) for d in range(2)],
        "a_up": [jnp.concatenate([zeros, a_up[d]], axis=0).astype(_BF16) for d in range(2)],
        "ones_r": (head[:, None] == jnp.arange(_LANES)[None, :]).astype(_BF16),
        "ones_e": (jnp.arange(_LANES)[:, None] == head[None, :]).astype(_BF16),
        "r_k": _row(r_k), "ln_w": _row(ln_w), "ln_b": _row(ln_b), "g_up": g_up.astype(_BF16),
    }
    tt = _tile(t, _RW_TILE)
    y_dir = None
    for d, rev in enumerate((False, True)):
        y_dir = _rwkv_pass(p_rw, prm, d, y_dir, rev, tt)
    rw_out = y_dir

    wo = w_out.astype(_BF16)
    return _out_proj2(x, lru_out, rw_out, wo[:width], wo[width:], _tile(t, _ROW_TILE))


def _odd_layer(x, li, p):
    (w_in, conv_w, conv_b, w_q, w_k, w_v, w_ig, b_ig, w_fg, b_fg, skip, norm_w,
     w_out) = [z[li] for z in p["odd"]]
    t = x.shape[1]
    inner = skip.shape[-1]
    nh = b_ig.shape[-1]
    wg = jnp.concatenate([w_ig[0], w_ig[1], w_fg[0], w_fg[1],
                          jnp.zeros((3 * inner, _LANES - 4 * nh), _F32)], axis=1).astype(_BF16)
    bg = jnp.concatenate([b_ig[0], b_ig[1], b_fg[0], b_fg[1], jnp.zeros((_LANES - 4 * nh,), _F32)])
    prm = {
        "conv_w": conv_w, "conv_b": _row(conv_b),
        "wq": _block_diag_blocks(w_q, _ML_BLK).astype(_BF16),
        "wk": _block_diag_blocks(w_k, _ML_BLK).astype(_BF16),
        "wv": _block_diag_blocks(w_v, _ML_BLK).astype(_BF16),
        "wg": wg, "bg": _row(bg),
    }
    q, k, v, xc, z, gt = _mlstm_in(x, _row(p["norm_mix"][2 * li + 1]), w_in.astype(_BF16), prm,
                                   _tile(t, _ML_ROW_TILE))
    lc = _tile(t, _ML_TILE)
    hf = _mlstm_pass(q, k, v, gt, 0, False, lc)
    hb = _mlstm_pass(q, k, v, gt, 1, True, lc)
    return _mlstm_out(x, hf, hb, xc, z, _row(norm_w), _row(skip), w_out.astype(_BF16),
                      _tile(t, _ML_OUT_TILE))


def _trunk(x, p):
    depth = p["norm_mix"].shape[0]
    t = x.shape[1]
    for layer in range(depth):
        if layer % 2 == 0:
            x = _even_layer(x, layer // 2, p)
        else:
            x = _odd_layer(x, layer // 2, p)
        w_up, conv_w, conv_b, w_down = [z[layer] for z in p["ffn"]]
        dff = w_down.shape[0]
        ck = _FFN_COLS if dff % _FFN_COLS == 0 else dff
        g_final = _row(p["norm_final"]) if layer == depth - 1 else None
        x = _ffn(x, _row(p["norm_ffn"][layer]), w_up.astype(_BF16), conv_w, _row(conv_b),
                 w_down.astype(_BF16), g_final, _tile(t, _ROW_TILE), ck)
    return x


def kernel(x_prompt, x_sample, ev_w_in, ev_lru_conv_w, ev_lru_conv_b, ev_lru_wa, ev_lru_ba, ev_lru_wx, ev_lru_bx, ev_lru_lambda, ev_rw_mu, ev_rw_w0, ev_rw_w_up, ev_rw_a0, ev_rw_a_up, ev_rw_g_up, ev_rw_k_k, ev_rw_k_a, ev_rw_r_k, ev_rw_ln_w, ev_rw_ln_b, ev_w_out, od_w_in, od_conv_w, od_conv_b, od_w_q, od_w_k, od_w_v, od_w_ig, od_b_ig, od_w_fg, od_b_fg, od_skip, od_norm_w, od_w_out, ffn_w_up, ffn_conv_w, ffn_conv_b, ffn_w_down, norm_mix, norm_ffn, norm_final):
    p = {
        "even": (ev_w_in, ev_lru_conv_w, ev_lru_conv_b, ev_lru_wa, ev_lru_ba, ev_lru_wx, ev_lru_bx,
                 ev_lru_lambda, ev_rw_mu, ev_rw_w0, ev_rw_w_up, ev_rw_a0, ev_rw_a_up, ev_rw_g_up,
                 ev_rw_k_k, ev_rw_k_a, ev_rw_r_k, ev_rw_ln_w, ev_rw_ln_b, ev_w_out),
        "odd": (od_w_in, od_conv_w, od_conv_b, od_w_q, od_w_k, od_w_v, od_w_ig, od_b_ig, od_w_fg,
                od_b_fg, od_skip, od_norm_w, od_w_out),
        "ffn": (ffn_w_up, ffn_conv_w, ffn_conv_b, ffn_w_down),
        "norm_mix": norm_mix, "norm_ffn": norm_ffn, "norm_final": norm_final,
    }
    return (_trunk(x_prompt, p), _trunk(x_sample, p))
```

```python
import functools
import math

import jax
import jax.numpy as jnp
from jax import lax
from jax.experimental import pallas as pl
from jax.experimental.pallas import tpu as pltpu

_F32 = jnp.float32
_BF16 = jnp.bfloat16

_RMS_EPS = 1e-6
_LRU_C = 8.0
_RW_HEAD = 64
_RW_DECAY_SCALE = math.exp(-0.5)
_RW_GN_EPS = 64e-5
_ML_HEADS = 4
_ML_LN_EPS = 1e-5
_ML_BLK = 256

_SUBLANES = 8
_LANES = 128
_VMEM_LIMIT = 56 * 1024 * 1024
_ROW_TILE = 1024
_NORM_TILE = 1024
_LRU_TILE = 1024
_RW_TILE = 256
_RW_CHUNK = 64
_RW_GROUP = 256
_RW_SEQS = 2
_ML_ROW_TILE = 512
_ML_IN_COLS = 512
_ML_OUT_TILE = 512
_ML_TILE = 256
_FFN_COLS = 256


def _cparams(n_axes):
    return pltpu.CompilerParams(dimension_semantics=("arbitrary",) * n_axes,
                                vmem_limit_bytes=_VMEM_LIMIT)


def _dot(a, b):
    return jnp.dot(a.astype(_BF16), b.astype(_BF16), preferred_element_type=_F32)


def _dot_nt(a, b):
    return lax.dot_general(a.astype(_BF16), b.astype(_BF16), (((1,), (1,)), ((), ())),
                           preferred_element_type=_F32)


def _dot_tn(a, b):
    return lax.dot_general(a.astype(_BF16), b.astype(_BF16), (((0,), (0,)), ((), ())),
                           preferred_element_type=_F32)


def _split3(x):
    hi = x.astype(_BF16)
    r1 = x - hi.astype(_F32)
    mid = r1.astype(_BF16)
    lo = (r1 - mid.astype(_F32)).astype(_BF16)
    return hi, mid, lo


def _dot01_left(m01, x):
    hi, mid, lo = _split3(x)
    f = lambda p: jnp.dot(m01, p, preferred_element_type=_F32)
    return (f(lo) + f(mid)) + f(hi)


def _dot01_right(x, m01):
    hi = x.astype(_BF16)
    lo = (x - hi.astype(_F32)).astype(_BF16)
    f = lambda p: jnp.dot(p, m01, preferred_element_type=_F32)
    return f(lo) + f(hi)


def _sigmoid(x):
    return 0.5 * jnp.tanh(0.5 * x) + 0.5


def _softplus(x):
    return jnp.maximum(x, 0.0) + jnp.log1p(jnp.exp(-jnp.abs(x)))


def _gelu_tanh(x):
    c = math.sqrt(2.0 / math.pi)
    return x * (0.5 * (1.0 + jnp.tanh(c * (x + 0.044715 * (x * x * x)))))


def _rms(x, g):
    ms = jnp.mean(x * x, axis=-1, keepdims=True)
    return x * lax.rsqrt(ms + _RMS_EPS) * g


def _halo_shifter(main, prev, nxt, ti=None, nt=None):
    rows = main.shape[0]
    if ti is not None:
        prev = jnp.where(ti == 0, 0.0, prev)
        nxt = jnp.where(ti == nt - 1, 0.0, nxt)
    ridx = lax.broadcasted_iota(jnp.int32, main.shape, 0)
    first, second, last = ridx == 0, ridx == 1, ridx == rows - 1

    def shift(k):
        rolled = pltpu.roll(main, k % rows, axis=0)
        if k == 1:
            return jnp.where(first, prev[_SUBLANES - 1:_SUBLANES], rolled)
        if k == 2:
            return jnp.where(first, prev[_SUBLANES - 2:_SUBLANES - 1],
                             jnp.where(second, prev[_SUBLANES - 1:_SUBLANES], rolled))
        assert k == -1
        return jnp.where(last, nxt[0:1], rolled)

    return shift


def _shifted(ext, k, rows, halo=_SUBLANES):
    if k == 0:
        return ext[halo:halo + rows]
    return pltpu.roll(ext, k % ext.shape[0], axis=0)[halo:halo + rows]


def _time_specs(width, rows, nt, t_len, reverse, col=0, bb=None):
    per = rows // _SUBLANES
    last = t_len // _SUBLANES - 1

    def tidx(i):
        return (nt - 1 - i) if reverse else i

    main = pl.BlockSpec((bb, rows, width), lambda b, i: (b, tidx(i), col))
    prev = pl.BlockSpec((bb, _SUBLANES, width),
                        lambda b, i: (b, jnp.maximum(tidx(i) * per - 1, 0), col))
    nxt = pl.BlockSpec((bb, _SUBLANES, width),
                       lambda b, i: (b, jnp.minimum((tidx(i) + 1) * per, last), col))
    return main, prev, nxt


def _const_spec(shape):
    nd = len(shape)
    return pl.BlockSpec(shape, lambda *_: (0,) * nd)


def _resident_spec(shape):
    nd = len(shape)
    return pl.BlockSpec(shape, lambda *_: (0,) * nd, pipeline_mode=pl.Buffered(1))


def _norm_proj_kernel(x_ref, g_ref, w_ref, *o_refs):
    h = _rms(x_ref[...], g_ref[...])
    res = _dot(h, w_ref[...])
    off = 0
    for o in o_refs:
        wdt = o.shape[-1]
        o[...] = res[:, off:off + wdt].astype(o.dtype)
        off += wdt


def _norm_proj(x, g, w, splits, dtypes, tm):
    b, t, d = x.shape
    n = w.shape[1]
    assert sum(splits) == n
    nt = t // tm
    row = lambda wd: pl.BlockSpec((None, tm, wd), lambda bi, i: (bi, i, 0))
    return pl.pallas_call(
        _norm_proj_kernel,
        grid=(b, nt),
        in_specs=[row(d), _const_spec((1, d)), _resident_spec((d, n))],
        out_specs=[row(s) for s in splits],
        out_shape=[jax.ShapeDtypeStruct((b, t, s), dt) for s, dt in zip(splits, dtypes)],
        compiler_params=_cparams(2),
        name="norm_proj",
    )(x, g, w)


def _lru_kernel(*refs, reverse, finalize, tt, nt):
    x_ref, xp_ref, xn_ref, cw_ref, cb_ref, wg_ref, bg_ref, lam_ref = refs[:8]
    if finalize:
        hf_ref, gate_ref, o_ref, a_s, u_s, h_s, carry = refs[8:]
    else:
        o_ref, a_s, u_s, h_s, carry = refs[8:]
    i = pl.program_id(1)
    ti = (nt - 1 - i) if reverse else i
    nseq, _, width = x_ref.shape
    nslab = width // _LANES
    pitch = a_s.shape[1] // nseq

    cw = cw_ref[...]
    neg_c_sp = (-_LRU_C) * _softplus(-lam_ref[...])
    xcs = []
    for b in range(nseq):
        xb = x_ref[b]
        shift = _halo_shifter(xb, xp_ref[b], xn_ref[b], ti, nt)
        xc = cb_ref[...] + shift(2) * cw[0:1]
        xc = xc + shift(1) * cw[1:2]
        xc = xc + xb * cw[2:3]
        xcs.append(xc + shift(-1) * cw[3:4])
    gates_all = _dot(jnp.concatenate(xcs, axis=0), wg_ref[...]) + bg_ref[...]
    for b in range(nseq):
        gates = gates_all[b * tt:(b + 1) * tt]
        r = _sigmoid(gates[:, :width])
        ig = _sigmoid(gates[:, width:])
        log_a = r * neg_c_sp
        th = jnp.tanh(log_a)
        a = jnp.exp(log_a)
        u = jnp.sqrt(-2.0 * th) * lax.rsqrt(1.0 - th) * (ig * xcs[b])
        for j in range(nslab):
            a_s[j, pl.ds(b * pitch, tt), :] = a[:, j * _LANES:(j + 1) * _LANES]
            u_s[j, pl.ds(b * pitch, tt), :] = u[:, j * _LANES:(j + 1) * _LANES]

    @pl.when(i == 0)
    def _():
        carry[...] = jnp.zeros_like(carry)

    def body(s, hs):
        t = (tt - 1 - s) if reverse else s
        new = []
        for j in range(nslab):
            rows = pl.ds(t, nseq, stride=pitch)
            h = a_s[j, rows, :] * hs[j] + u_s[j, rows, :]
            h_s[j, rows, :] = h
            new.append(h)
        return tuple(new)

    hs = lax.fori_loop(0, tt, body, tuple(carry[j] for j in range(nslab)), unroll=_SUBLANES)
    for j in range(nslab):
        carry[j] = hs[j]

    for b in range(nseq):
        h = jnp.concatenate([h_s[j, pl.ds(b * pitch, tt), :] for j in range(nslab)], axis=1)
        if finalize:
            h = (hf_ref[b] + h) * _gelu_tanh(gate_ref[b])
        o_ref[b] = h


def _lru_pass(p_lru, conv_w, conv_b, wg, bg, lam, hf, reverse, tt):
    b, t, w2 = p_lru.shape
    w = w2 // 2
    nseq = max(n for n in (8, 4, 2, 1) if b % n == 0)
    tt = _tile(t, max(tt // nseq, _LANES))
    nt = t // tt
    pitch = tt + _SUBLANES if (tt // _SUBLANES) % 2 == 0 else tt
    nslab = w // _LANES
    finalize = hf is not None
    main, prev, nxt = _time_specs(w, tt, nt, t, reverse, bb=nseq)
    tidx = (lambda i: nt - 1 - i) if reverse else (lambda i: i)
    tile = lambda col: pl.BlockSpec((nseq, tt, w), lambda bi, i: (bi, tidx(i), col))
    in_specs = [main, prev, nxt, _const_spec(conv_w.shape), _const_spec(conv_b.shape),
                _const_spec(wg.shape), _const_spec(bg.shape), _const_spec(lam.shape)]
    args = [p_lru, p_lru, p_lru, conv_w, conv_b, wg, bg, lam]
    scratch = [pltpu.VMEM((nslab, nseq * pitch, _LANES), _F32)] * 3
    if finalize:
        in_specs += [tile(0), tile(1)]
        args += [hf, p_lru]
    scratch += [pltpu.VMEM((nslab, nseq, _LANES), _F32)]
    return pl.pallas_call(
        functools.partial(_lru_kernel, reverse=reverse, finalize=finalize, tt=tt, nt=nt),
        grid=(b // nseq, nt),
        in_specs=in_specs,
        out_specs=tile(0),
        out_shape=jax.ShapeDtypeStruct((b, t, w), _F32),
        scratch_shapes=scratch,
        compiler_params=_cparams(2),
        name="lru_bwd" if reverse else "lru_fwd",
    )(*args)


def _rwkv_kernel(*refs, reverse, finalize, tt, nt):
    (rw_ref, rwp_ref, rwn_ref, mu_ref, kk_ref, ka_ref, w0_ref, a0_ref, wup_ref, aup_ref,
     onesr_ref, onese_ref) = refs[:12]
    if finalize:
        yf_ref, rk_ref, lnw_ref, lnb_ref, gup_ref, o_ref, ht_s = refs[12:]
    else:
        o_ref, ht_s = refs[12:]
    i = pl.program_id(1)
    ti = (nt - 1 - i) if reverse else i
    nseq, ngrp, gw, _ = ht_s.shape
    c = ngrp * gw
    lc = _RW_CHUNK
    hpg = gw // _RW_HEAD
    ones_r = onesr_ref[...]
    ones_e = onese_ref[...]
    seqs = range(nseq)

    def segsum(x):
        return _dot01_right(_dot01_right(x, ones_r), ones_e)

    tok = []
    for b in seqs:
        x = rw_ref[b]
        tshift = _halo_shifter(x, rwp_ref[b], rwn_ref[b], ti, nt)
        rws = x + mu_ref[...] * (0.5 * (tshift(1) + tshift(-1)) - x)
        r = rws[:, 0:c]
        k = rws[:, c:2 * c]
        v = rws[:, 2 * c:3 * c]
        wad = rws[:, 3 * c:3 * c + _LANES]
        kk = k * kk_ref[...]
        kk = kk * jnp.minimum(lax.rsqrt(segsum(kk * kk)), 1e12)
        lw = (-_RW_DECAY_SCALE) * _sigmoid(w0_ref[...] + _dot(jnp.tanh(wad), wup_ref[...]))
        a = _sigmoid(a0_ref[...] + _dot(wad, aup_ref[...]))
        tok.append(dict(rws=rws, r=r, k=k, v=v, kk=kk, lw=lw, kt=k * (1.0 + (a - 1.0) * ka_ref[...]),
                        bvec=a * kk))

    shift = _RW_HEAD.bit_length() - 1
    row = lax.broadcasted_iota(jnp.int32, (gw, gw), 0)
    col = lax.broadcasted_iota(jnp.int32, (gw, gw), 1)
    same = (row >> shift) == (col >> shift)
    trow = lax.broadcasted_iota(jnp.int32, (lc, gw), 0)
    tcol = lax.broadcasted_iota(jnp.int32, (lc, gw), 1) & (lc - 1)
    m_incl = (tcol >= trow) if reverse else (tcol <= trow)
    m_strict = (tcol > trow) if reverse else (tcol < trow)
    eye_cat = jnp.where(tcol == trow, 1.0, 0.0)
    srow = lax.broadcasted_iota(jnp.int32, (lc, lc), 0)
    scol = lax.broadcasted_iota(jnp.int32, (lc, lc), 1)
    tri01 = ((scol >= srow) if reverse else (scol <= srow)).astype(_BF16)

    def stack(xv):
        xb = xv.astype(_BF16)
        return jnp.where(same, jnp.concatenate([xb] * hpg, axis=0), jnp.zeros((), _BF16))

    @pl.when(i == 0)
    def _():
        ht_s[...] = jnp.zeros_like(ht_s)

    nchunk = tt // lc
    chunks = list(range(nchunk - 1, -1, -1) if reverse else range(nchunk))
    groups = range(ngrp)
    lanes = [slice(g * gw, (g + 1) * gw) for g in groups]
    units = [(b, ci, g) for ci in chunks for b in seqs for g in groups]

    pre = {}
    for ci in chunks:
        sl = slice(ci * lc, (ci + 1) * lc)
        for b in seqs:
            tb = tok[b]
            r_c, lw_c, k_c, v_c, kk_c, b_c = (tb[n][sl] for n in ("r", "lw", "kt", "v", "kk", "bvec"))
            cum = _dot01_left(tri01, lw_c)
            clast = cum[0:1] if reverse else cum[lc - 1:lc]
            einv = jnp.exp(-cum)
            eend = jnp.exp(clast - cum)
            bf = lambda z: z.astype(_BF16)
            pre[b, ci] = dict(gamma=jnp.exp(clast), rg=bf(r_c * jnp.exp(cum)), kkg=bf(kk_c * jnp.exp(cum - lw_c)),
                              kd=bf(k_c * einv), bd=bf(b_c * einv), kend=bf(k_c * eend), bend=bf(b_c * eend),
                              v=bf(v_c))

    st = {}
    for b, ci, g in units:
        ls = lanes[g]
        pc = pre[b, ci]
        lhs = jnp.concatenate([pc["kkg"][:, ls], pc["rg"][:, ls]], axis=0)
        rhs = jnp.concatenate([stack(pc["kd"][:, ls]), stack(pc["bd"][:, ls])], axis=0)
        amat = _dot_nt(lhs, rhs)
        nj = -jnp.where(m_strict, amat[:lc, gw:], 0.0)
        st[b, ci, g] = dict(a_kk_k=jnp.where(m_strict, amat[:lc, :gw], 0.0).astype(_BF16),
                            a_r_k=jnp.where(m_incl, amat[lc:, :gw], 0.0).astype(_BF16),
                            a_r_b=jnp.where(m_incl, amat[lc:, gw:], 0.0).astype(_BF16),
                            nj=nj.astype(_BF16), tinv=eye_cat + nj)
    for key in units:
        s = st[key]
        s["nj"] = _dot(s["nj"], stack(s["nj"])).astype(_BF16)
    for _ in range(lc.bit_length() - 3):
        for key in units:
            s = st[key]
            pn = _dot(jnp.concatenate([s["tinv"].astype(_BF16), s["nj"]], axis=0), stack(s["nj"]))
            s["tinv"] = s["tinv"] + pn[:lc]
            s["nj"] = pn[lc:].astype(_BF16)
    for key in units:
        s = st[key]
        s["tinv"] = s["tinv"] + _dot(s["tinv"], stack(s["nj"]))
    for b, ci, g in units:
        s = st[b, ci, g]
        s["av"] = _dot(jnp.concatenate([s["a_kk_k"], s["a_r_k"]], axis=0), stack(pre[b, ci]["v"][:, lanes[g]]))
    for b, ci, g in units:
        s = st[b, ci, g]
        tk = _dot(s["tinv"], jnp.concatenate([stack(pre[b, ci]["kkg"][:, lanes[g]]), stack(s["av"][:lc])],
                                             axis=1))
        s["kkgp"], s["uv"] = tk[:, :gw].astype(_BF16), tk[:, gw:]
    for b, ci, g in units:
        ls = lanes[g]
        s = st[b, ci, g]
        pc = pre[b, ci]
        s["kb"] = jnp.where(same, _dot_tn(s["kkgp"], pc["bend"][:, ls]), 0.0).astype(_BF16)
        s["gmat"] = jnp.where(same, _dot_tn(jnp.concatenate([pc["v"][:, ls], (-s["uv"]).astype(_BF16)], axis=0),
                                            jnp.concatenate([pc["kend"][:, ls], pc["bend"][:, ls]], axis=0)),
                              0.0)

    chains = [(b, g) for b in seqs for g in groups]
    hts = {key: ht_s[key[0], key[1]] for key in chains}
    ys = {}
    for ci in chunks:
        cur = dict(hts)
        for b, g in chains:
            s = st[b, ci, g]
            hts[b, g] = cur[b, g] * pre[b, ci]["gamma"][:, lanes[g]] - _dot(cur[b, g], s["kb"]) + s["gmat"]
        pus = {(b, g): _dot_nt(jnp.concatenate([st[b, ci, g]["kkgp"], pre[b, ci]["rg"][:, lanes[g]]], axis=0),
                               cur[b, g]) for b, g in chains}
        us = {key: pus[key][:lc] + st[key[0], ci, key[1]]["uv"] for key in chains}
        for b in seqs:
            ys[b, ci] = jnp.concatenate(
                [pus[b, g][lc:] + st[b, ci, g]["av"][lc:] - _dot(st[b, ci, g]["a_r_b"], stack(us[b, g]))
                 for g in groups], axis=1)
    for b, g in chains:
        ht_s[b, g] = hts[b, g]

    for b in seqs:
        y = jnp.concatenate([ys[b, ci] for ci in range(nchunk)], axis=0)
        if finalize:
            tb = tok[b]
            y = yf_ref[b] + y
            inv_n = 1.0 / _RW_HEAD
            mean = segsum(y) * inv_n
            d = y - mean
            var = segsum(d * d) * inv_n
            yn = d * lax.rsqrt(var + _RW_GN_EPS) * lnw_ref[...] + lnb_ref[...]
            bonus = segsum(tb["r"] * tb["k"] * rk_ref[...]) * tb["v"]
            gd = tb["rws"][:, 3 * c + _LANES:3 * c + 2 * _LANES]
            y = (yn + bonus) * _dot(_sigmoid(gd), gup_ref[...])
        o_ref[b] = y


def _rwkv_pass(p_rw, prm, d, yf, reverse, tt):
    b, t, cols = p_rw.shape
    c = prm["k_k"].shape[-1]
    nt = t // tt
    nseq = _RW_SEQS if b % _RW_SEQS == 0 else 1
    finalize = yf is not None
    main, prev, nxt = _time_specs(cols, tt, nt, t, reverse, bb=nseq)
    tidx = (lambda i: nt - 1 - i) if reverse else (lambda i: i)
    tile = pl.BlockSpec((nseq, tt, c), lambda bi, i: (bi, tidx(i), 0))
    consts = [prm["mu"], prm["k_k"], prm["k_a"], prm["w0"][d], prm["a0"][d], prm["w_up"][d],
              prm["a_up"][d], prm["ones_r"], prm["ones_e"]]
    args = [p_rw, p_rw, p_rw] + consts
    in_specs = [main, prev, nxt] + [_const_spec(z.shape) for z in consts]
    if finalize:
        extra = [prm["r_k"], prm["ln_w"], prm["ln_b"], prm["g_up"]]
        args += [yf] + extra
        in_specs += [tile] + [_const_spec(z.shape) for z in extra]
    return pl.pallas_call(
        functools.partial(_rwkv_kernel, reverse=reverse, finalize=finalize, tt=tt, nt=nt),
        grid=(b // nseq, nt),
        in_specs=in_specs,
        out_specs=tile,
        out_shape=jax.ShapeDtypeStruct((b, t, c), _F32),
        scratch_shapes=[pltpu.VMEM((nseq, c // _RW_GROUP, _RW_GROUP, _RW_GROUP), _F32)],
        compiler_params=_cparams(2),
        name="rwkv_bwd" if reverse else "rwkv_fwd",
    )(*args)


def _out_proj2_kernel(x_ref, a_ref, b_ref, wa_ref, wb_ref, o_ref):
    o_ref[...] = x_ref[...] + _dot(a_ref[...], wa_ref[...]) + _dot(b_ref[...], wb_ref[...])


def _out_proj2(x, a, bm, wa, wb, tm):
    b, t, d = x.shape
    nt = t // tm
    row = lambda wd: pl.BlockSpec((None, tm, wd), lambda bi, i: (bi, i, 0))
    return pl.pallas_call(
        _out_proj2_kernel,
        grid=(b, nt),
        in_specs=[row(d), row(a.shape[-1]), row(bm.shape[-1]), _const_spec(wa.shape),
                  _const_spec(wb.shape)],
        out_specs=row(d),
        out_shape=jax.ShapeDtypeStruct((b, t, d), _F32),
        compiler_params=_cparams(2),
        name="even_out_proj",
    )(x, a, bm, wa, wb)


def _ffn_kernel(*refs, tm, nt, ck, final_norm):
    x_ref, xp_ref, xn_ref, g_ref, wup_ref, cw_ref, cb_ref, wd_ref = refs[:8]
    if final_norm:
        gf_ref, o_ref, h_s, act_s = refs[8:]
    else:
        o_ref, h_s, act_s = refs[8:]
    i = pl.program_id(1)
    dff = wd_ref.shape[0]
    halo = 2 * _SUBLANES
    g = g_ref[...]
    zeros = jnp.zeros((_SUBLANES, x_ref.shape[-1]), _F32)
    prev = jnp.where(i == 0, 0.0, _rms(xp_ref[...], g))
    nxt = jnp.where(i == nt - 1, 0.0, _rms(xn_ref[...], g))
    h_s[...] = jnp.concatenate([zeros, prev, _rms(x_ref[...], g), nxt, zeros], axis=0).astype(_BF16)
    hext = h_s[...]
    hmain = h_s[pl.ds(halo, tm), :]
    cw = cw_ref[...]
    cb = cb_ref[...]
    for c in range(dff // ck):
        cs = slice(c * ck, (c + 1) * ck)
        ug = jnp.dot(hext, wup_ref[:, dff + c * ck:dff + (c + 1) * ck], preferred_element_type=_F32)
        gate = cb[:, cs] + _shifted(ug, 1, tm, halo) * cw[0:1, cs]
        gate = gate + _shifted(ug, 0, tm, halo) * cw[1:2, cs]
        gate = gate + _shifted(ug, -1, tm, halo) * cw[2:3, cs]
        val = jnp.dot(hmain, wup_ref[:, cs], preferred_element_type=_F32)
        act_s[:, cs] = (_gelu_tanh(gate) * val).astype(_BF16)
    y = x_ref[...] + jnp.dot(act_s[...], wd_ref[...], preferred_element_type=_F32)
    if final_norm:
        y = _rms(y, gf_ref[...])
    o_ref[...] = y


def _ffn(x, g, w_up, conv_w, conv_b, w_down, g_final, tm, ck):
    b, t, d = x.shape
    dff = w_down.shape[0]
    nt = t // tm
    final_norm = g_final is not None
    main, prev, nxt = _time_specs(d, tm, nt, t, False)
    consts = [g, w_up, conv_w, conv_b, w_down] + ([g_final] if final_norm else [])
    return pl.pallas_call(
        functools.partial(_ffn_kernel, tm=tm, nt=nt, ck=ck, final_norm=final_norm),
        grid=(b, nt),
        in_specs=[main, prev, nxt] + [_resident_spec(z.shape) for z in consts],
        out_specs=main,
        out_shape=jax.ShapeDtypeStruct((b, t, d), _F32),
        scratch_shapes=[pltpu.VMEM((tm + 4 * _SUBLANES, d), _BF16), pltpu.VMEM((tm, dff), _BF16)],
        compiler_params=_cparams(2),
        name="conv_ffn",
    )(x, x, x, *consts)


def _mlstm_in_kernel(x_ref, xp_ref, xn_ref, g_ref, w_ref, cw_ref, cb_ref, wq_ref, wk_ref, wv_ref,
                     wg_ref, bg_ref, q_ref, k_ref, v_ref, xc_ref, z_ref, gt_ref, *, tm, nt, cols):
    i = pl.program_id(1)
    inner = z_ref.shape[-1]
    g = g_ref[...]
    hext = jnp.concatenate([_rms(xp_ref[...], g), _rms(x_ref[...], g), _rms(xn_ref[...], g)], axis=0)
    hext_b = hext.astype(_BF16)
    hmain_b = hext[_SUBLANES:_SUBLANES + tm].astype(_BF16)
    cw = cw_ref[...]
    cb = cb_ref[...]
    qscale = (inner // _ML_HEADS) ** -0.5
    nblk = inner // cols
    proj = lambda j: jnp.dot(hext_b, w_ref[:, j * cols:(j + 1) * cols], preferred_element_type=_F32)
    gates = bg_ref[...]
    xme_next = proj(0)
    for j in range(nblk):
        cs = slice(j * cols, (j + 1) * cols)
        xme = xme_next
        if j + 1 < nblk:
            xme_next = proj(j + 1)
        z_ref[:, cs] = jnp.dot(hmain_b, w_ref[:, inner + j * cols:inner + (j + 1) * cols],
                               preferred_element_type=_F32).astype(z_ref.dtype)
        prev = jnp.where(i == 0, 0.0, xme[:_SUBLANES])
        nxt = jnp.where(i == nt - 1, 0.0, xme[_SUBLANES + tm:])
        xm = xme[_SUBLANES:_SUBLANES + tm]
        shift = _halo_shifter(xm, prev, nxt)
        zc = cb[:, cs] + shift(2) * cw[0:1, cs]
        zc = zc + shift(1) * cw[1:2, cs]
        zc = zc + xm * cw[2:3, cs]
        zc = zc + shift(-1) * cw[3:4, cs]
        xc = zc * _sigmoid(zc)
        xc_ref[:, cs] = xc.astype(xc_ref.dtype)
        for jb in range(cols // _ML_BLK):
            blk = j * (cols // _ML_BLK) + jb
            sub = slice(jb * _ML_BLK, (jb + 1) * _ML_BLK)
            sl = slice(blk * _ML_BLK, (blk + 1) * _ML_BLK)
            q = _dot(xc[:, sub], wq_ref[blk])
            k = _dot(xc[:, sub], wk_ref[blk])
            v = _dot(xm[:, sub], wv_ref[blk])
            gates = (gates + _dot(q, wg_ref[sl])
                     + _dot(k, wg_ref[inner + blk * _ML_BLK:inner + (blk + 1) * _ML_BLK])
                     + _dot(v, wg_ref[2 * inner + blk * _ML_BLK:2 * inner + (blk + 1) * _ML_BLK]))
            q_ref[:, sl] = (q * qscale).astype(q_ref.dtype)
            k_ref[:, sl] = k.astype(k_ref.dtype)
            v_ref[:, sl] = v.astype(v_ref.dtype)
    gt_ref[...] = gates


def _mlstm_in(x, g, w_in, prm, tm):
    b, t, d = x.shape
    inner = w_in.shape[1] // 2
    nt = t // tm
    cols = _ML_IN_COLS if inner % _ML_IN_COLS == 0 else inner
    main, prev, nxt = _time_specs(d, tm, nt, t, False)
    consts = [g, w_in, prm["conv_w"], prm["conv_b"], prm["wq"], prm["wk"], prm["wv"], prm["wg"], prm["bg"]]
    row = lambda wd: pl.BlockSpec((None, tm, wd), lambda bi, i: (bi, i, 0))
    big = jax.ShapeDtypeStruct((b, t, inner), _BF16)
    return pl.pallas_call(
        functools.partial(_mlstm_in_kernel, tm=tm, nt=nt, cols=cols),
        grid=(b, nt),
        in_specs=[main, prev, nxt] + [_resident_spec(z.shape) for z in consts],
        out_specs=[row(inner)] * 5 + [row(_LANES)],
        out_shape=[big] * 5 + [jax.ShapeDtypeStruct((b, t, _LANES), _F32)],
        compiler_params=_cparams(2),
        name="mlstm_in",
    )(x, x, x, *consts)


def _mlstm_kernel(*refs, lc, dirs):
    nd = len(dirs)
    ins, outs = refs[:4 * nd], refs[4 * nd:5 * nd]
    c_s, n_s, m_s = refs[5 * nd:]
    i = pl.program_id(1)
    dh = c_s.shape[-1]

    @pl.when(i == 0)
    def _():
        c_s[...] = jnp.zeros_like(c_s)
        n_s[...] = jnp.zeros_like(n_s)
        m_s[...] = jnp.zeros_like(m_s)

    lane = lax.broadcasted_iota(jnp.int32, (lc, _LANES), 1)
    row = lax.broadcasted_iota(jnp.int32, (lc, lc), 0)
    col = lax.broadcasted_iota(jnp.int32, (lc, lc), 1)

    def lane_col(mat, j):
        return jnp.sum(jnp.where(lane == j, mat, 0.0), axis=-1, keepdims=True)

    units = [(x, h) for x in range(nd) for h in range(_ML_HEADS)]
    tri, gates, bcum_all = [], [], []
    for x, (d, reverse) in enumerate(dirs):
        tri.append((col >= row) if reverse else (col <= row))
        gates.append(ins[4 * x + 3][...])
        bcum_all.append(_dot01_left(tri[x].astype(_BF16), -_softplus(-gates[x])))
    sls = [slice(h * dh, (h + 1) * dh) for h in range(_ML_HEADS)]
    li, bcum, m_prev, src_row, dmat, inter, m_t, g, wlog, m_new, wexp, wexp_row, decay = ({} for _ in range(13))
    for u in units:
        x, h = u
        d, reverse = dirs[x]
        li[u] = lane_col(gates[x], d * _ML_HEADS + h)
        bcum[u] = lane_col(bcum_all[x], 2 * _ML_HEADS + d * _ML_HEADS + h)
        m_prev[u] = m_s[x * _ML_HEADS + h][:, 0:1]
        src_row[u] = jnp.broadcast_to(li[u] - bcum[u], (lc, lc)).T
        dmat[u] = jnp.where(tri[x], bcum[u] + src_row[u], -jnp.inf)
        inter[u] = bcum[u] + m_prev[u]
        m_t[u] = jnp.maximum(inter[u], jnp.max(dmat[u], axis=-1, keepdims=True))
        g[u] = bcum[u][0:1] if reverse else bcum[u][lc - 1:lc]
        wlog[u] = g[u] - bcum[u] + li[u]
        m_new[u] = jnp.maximum(g[u] + m_prev[u], jnp.max(wlog[u], axis=0, keepdims=True))
        wexp[u] = jnp.exp(wlog[u] - m_new[u])
        wexp_row[u] = jnp.exp(g[u] + src_row[u][0:2 * _SUBLANES] - m_new[u])
        decay[u] = jnp.exp(g[u] + m_prev[u] - m_new[u])
    q = {(x, h): ins[4 * x][:, sls[h]] for x, h in units}
    k = {(x, h): ins[4 * x + 1][:, sls[h]] for x, h in units}
    v = {(x, h): ins[4 * x + 2][:, sls[h]] for x, h in units}
    cmat = {(x, h): c_s[x * _ML_HEADS + h] for x, h in units}
    nvec = {(x, h): n_s[x * _ML_HEADS + h] for x, h in units}
    s_raw = {u: _dot_nt(q[u], k[u]) for u in units}
    cq = {u: _dot_nt(q[u], cmat[u]) for u in units}
    qn = {u: _dot_nt(q[u], jnp.broadcast_to(nvec[u], (_LANES, dh)))[:, 0:1] for u in units}
    upd = {u: _dot_tn(v[u] * wexp[u].astype(v[u].dtype), k[u]) for u in units}
    nupd = {u: _dot(wexp_row[u], k[u])[0:1] for u in units}
    s = {u: s_raw[u] * jnp.exp(dmat[u] - m_t[u]) for u in units}
    pv = {u: _dot(s[u], v[u]) for u in units}
    for u in units:
        x, h = u
        scale = jnp.exp(inter[u] - m_t[u])
        num = pv[u] + scale * cq[u]
        den = jnp.sum(s[u], axis=-1, keepdims=True) + scale * qn[u]
        outs[x][:, sls[h]] = (num / jnp.maximum(jnp.abs(den), jnp.exp(-m_t[u]))).astype(outs[x].dtype)
        c_s[x * _ML_HEADS + h] = decay[u] * cmat[u] + upd[u]
        n_s[x * _ML_HEADS + h] = decay[u] * nvec[u] + nupd[u]
        m_s[x * _ML_HEADS + h] = jnp.broadcast_to(m_new[u], m_s.shape[1:])


def _mlstm_pass(q, k, v, gt, lc, dirs):
    b, t, inner = q.shape
    nc = t // lc
    dh = inner // _ML_HEADS
    nd = len(dirs)
    in_specs, args = [], []
    out_specs = []
    for _, reverse in dirs:
        tidx = (lambda i: nc - 1 - i) if reverse else (lambda i: i)
        row = lambda wd, tidx=tidx: pl.BlockSpec((None, lc, wd), lambda bi, i: (bi, tidx(i), 0))
        in_specs += [row(inner)] * 3 + [row(_LANES)]
        args += [q, k, v, gt]
        out_specs.append(row(inner))
    return pl.pallas_call(
        functools.partial(_mlstm_kernel, lc=lc, dirs=tuple(dirs)),
        grid=(b, nc),
        in_specs=in_specs,
        out_specs=out_specs,
        out_shape=[jax.ShapeDtypeStruct((b, t, inner), _BF16)] * nd,
        scratch_shapes=[pltpu.VMEM((nd * _ML_HEADS, dh, dh), _F32), pltpu.VMEM((nd * _ML_HEADS, 1, dh), _F32),
                        pltpu.VMEM((nd * _ML_HEADS, 1, _LANES), _F32)],
        compiler_params=_cparams(2),
        name="mlstm_scan",
    )(*args)


def _mlstm_out_kernel(x_ref, hf_ref, hb_ref, xc_ref, z_ref, nw_ref, skip_ref, wo_ref, o_ref):
    inner = hf_ref.shape[-1]
    dh = inner // _ML_HEADS
    acc = x_ref[...]
    for h in range(_ML_HEADS):
        sl = slice(h * dh, (h + 1) * dh)
        hh = hf_ref[:, sl].astype(_F32) + hb_ref[:, sl].astype(_F32)
        mean = jnp.mean(hh, axis=-1, keepdims=True)
        dlt = hh - mean
        var = jnp.mean(dlt * dlt, axis=-1, keepdims=True)
        hn = dlt * lax.rsqrt(var + _ML_LN_EPS) * nw_ref[:, sl]
        z = z_ref[:, sl].astype(_F32)
        out = (hn + skip_ref[:, sl] * xc_ref[:, sl].astype(_F32)) * (z * _sigmoid(z))
        acc = acc + _dot(out, wo_ref[sl, :])
    o_ref[...] = acc


def _mlstm_out(x, hf, hb, xc, z, norm_w, skip, w_out, tm):
    b, t, d = x.shape
    inner = hf.shape[-1]
    nt = t // tm
    row = lambda wd: pl.BlockSpec((None, tm, wd), lambda bi, i: (bi, i, 0))
    return pl.pallas_call(
        _mlstm_out_kernel,
        grid=(b, nt),
        in_specs=[row(d)] + [row(inner)] * 4 + [_const_spec(norm_w.shape), _const_spec(skip.shape),
                                                _const_spec(w_out.shape)],
        out_specs=row(d),
        out_shape=jax.ShapeDtypeStruct((b, t, d), _F32),
        compiler_params=_cparams(2),
        name="mlstm_out_proj",
    )(x, hf, hb, xc, z, norm_w, skip, w_out)


def _block_diag_dense(w):
    nb, bi, bo = w.shape
    eye = jnp.eye(nb, dtype=w.dtype)
    return jnp.einsum("nio,nm->nimo", w, eye).reshape(nb * bi, nb * bo)


def _block_diag_blocks(w, blk):
    nb, bi, bo = w.shape
    per = blk // bi
    return jax.vmap(_block_diag_dense)(w.reshape(nb // per, per, bi, bo))


def _row(v):
    return v.reshape(1, -1).astype(_F32)


def _tile(t_len, want):
    return min(want, t_len)


def _even_layer(x, li, p):
    (w_in, conv_w, conv_b, wa, ba, wx, bx, lam, mu, w0, w_up, a0, a_up, g_up, k_k, k_a, r_k,
     ln_w, ln_b, w_out) = [z[li] for z in p["even"]]
    t = x.shape[1]
    width = lam.shape[-1]
    c = k_k.shape[-1]
    p_lru, p_rw = _norm_proj(x, _row(p["norm_mix"][2 * li]), w_in.astype(_BF16),
                             (2 * width, w_in.shape[1] - 2 * width), (_F32, _F32),
                             _tile(t, _NORM_TILE))

    tt = _tile(t, _LRU_TILE)
    h_dir = None
    for d, rev in enumerate((False, True)):
        wg = jnp.concatenate([_block_diag_dense(wa[d]), _block_diag_dense(wx[d])], axis=1).astype(_BF16)
        bg = jnp.concatenate([ba[d], bx[d]]).reshape(1, -1)
        h_dir = _lru_pass(p_lru, conv_w, _row(conv_b), wg, bg, _row(lam[d]), h_dir, rev, tt)
    lru_out = h_dir

    zeros = jnp.zeros((_RW_HEAD, c), _F32)
    head = jnp.arange(c) // _RW_HEAD
    prm = {
        "mu": _row(mu), "k_k": _row(k_k), "k_a": _row(k_a),
        "w0": [_row(w0[d]) for d in range(2)], "a0": [_row(a0[d]) for d in range(2)],
        "w_up": [jnp.concatenate([w_up[d], zeros], axis=0).astype(_BF16) for d in range(2)],
        "a_up": [jnp.concatenate([zeros, a_up[d]], axis=0).astype(_BF16) for d in range(2)],
        "ones_r": (head[:, None] == jnp.arange(_LANES)[None, :]).astype(_BF16),
        "ones_e": (jnp.arange(_LANES)[:, None] == head[None, :]).astype(_BF16),
        "r_k": _row(r_k), "ln_w": _row(ln_w), "ln_b": _row(ln_b), "g_up": g_up.astype(_BF16),
    }
    tt = _tile(t, _RW_TILE)
    y_dir = None
    for d, rev in enumerate((False, True)):
        y_dir = _rwkv_pass(p_rw, prm, d, y_dir, rev, tt)
    rw_out = y_dir

    wo = w_out.astype(_BF16)
    return _out_proj2(x, lru_out, rw_out, wo[:width], wo[width:], _tile(t, _ROW_TILE))


def _odd_layer(x, li, p):
    (w_in, conv_w, conv_b, w_q, w_k, w_v, w_ig, b_ig, w_fg, b_fg, skip, norm_w,
     w_out) = [z[li] for z in p["odd"]]
    t = x.shape[1]
    inner = skip.shape[-1]
    nh = b_ig.shape[-1]
    wg = jnp.concatenate([w_ig[0], w_ig[1], w_fg[0], w_fg[1],
                          jnp.zeros((3 * inner, _LANES - 4 * nh), _F32)], axis=1).astype(_BF16)
    bg = jnp.concatenate([b_ig[0], b_ig[1], b_fg[0], b_fg[1], jnp.zeros((_LANES - 4 * nh,), _F32)])
    prm = {
        "conv_w": conv_w, "conv_b": _row(conv_b),
        "wq": _block_diag_blocks(w_q, _ML_BLK).astype(_BF16),
        "wk": _block_diag_blocks(w_k, _ML_BLK).astype(_BF16),
        "wv": _block_diag_blocks(w_v, _ML_BLK).astype(_BF16),
        "wg": wg, "bg": _row(bg),
    }
    q, k, v, xc, z, gt = _mlstm_in(x, _row(p["norm_mix"][2 * li + 1]), w_in.astype(_BF16), prm,
                                   _tile(t, _ML_ROW_TILE))
    lc = _tile(t, _ML_TILE)
    (hf,) = _mlstm_pass(q, k, v, gt, lc, ((0, False),))
    (hb,) = _mlstm_pass(q, k, v, gt, lc, ((1, True),))
    return _mlstm_out(x, hf, hb, xc, z, _row(norm_w), _row(skip), w_out.astype(_BF16),
                      _tile(t, _ML_OUT_TILE))


def _trunk(x, p):
    depth = p["norm_mix"].shape[0]
    t = x.shape[1]
    for layer in range(depth):
        if layer % 2 == 0:
            x = _even_layer(x, layer // 2, p)
        else:
            x = _odd_layer(x, layer // 2, p)
        w_up, conv_w, conv_b, w_down = [z[layer] for z in p["ffn"]]
        dff = w_down.shape[0]
        ck = _FFN_COLS if dff % _FFN_COLS == 0 else dff
        g_final = _row(p["norm_final"]) if layer == depth - 1 else None
        x = _ffn(x, _row(p["norm_ffn"][layer]), w_up.astype(_BF16), conv_w, _row(conv_b),
                 w_down.astype(_BF16), g_final, _tile(t, _ROW_TILE), ck)
    return x


def kernel(x_prompt, x_sample, ev_w_in, ev_lru_conv_w, ev_lru_conv_b, ev_lru_wa, ev_lru_ba, ev_lru_wx, ev_lru_bx, ev_lru_lambda, ev_rw_mu, ev_rw_w0, ev_rw_w_up, ev_rw_a0, ev_rw_a_up, ev_rw_g_up, ev_rw_k_k, ev_rw_k_a, ev_rw_r_k, ev_rw_ln_w, ev_rw_ln_b, ev_w_out, od_w_in, od_conv_w, od_conv_b, od_w_q, od_w_k, od_w_v, od_w_ig, od_b_ig, od_w_fg, od_b_fg, od_skip, od_norm_w, od_w_out, ffn_w_up, ffn_conv_w, ffn_conv_b, ffn_w_down, norm_mix, norm_ffn, norm_final):
    p = {
        "even": (ev_w_in, ev_lru_conv_w, ev_lru_conv_b, ev_lru_wa, ev_lru_ba, ev_lru_wx, ev_lru_bx,
                 ev_lru_lambda, ev_rw_mu, ev_rw_w0, ev_rw_w_up, ev_rw_a0, ev_rw_a_up, ev_rw_g_up,
                 ev_rw_k_k, ev_rw_k_a, ev_rw_r_k, ev_rw_ln_w, ev_rw_ln_b, ev_w_out),
        "odd": (od_w_in, od_conv_w, od_conv_b, od_w_q, od_w_k, od_w_v, od_w_ig, od_b_ig, od_w_fg,
                od_b_fg, od_skip, od_norm_w, od_w_out),
        "ffn": (ffn_w_up, ffn_conv_w, ffn_conv_b, ffn_w_down),
        "norm_mix": norm_mix, "norm_ffn": norm_ffn, "norm_final": norm_final,
    }
    return (_trunk(x_prompt, p), _trunk(x_sample, p))
```

```python
import functools
import math

import jax
import jax.numpy as jnp
from jax import lax
from jax.experimental import pallas as pl
from jax.experimental.pallas import tpu as pltpu

_F32 = jnp.float32
_BF16 = jnp.bfloat16

_RMS_EPS = 1e-6
_LRU_C = 8.0
_RW_HEAD = 64
_RW_DECAY_SCALE = math.exp(-0.5)
_RW_GN_EPS = 64e-5
_ML_HEADS = 4
_ML_LN_EPS = 1e-5
_ML_BLK = 256

_SUBLANES = 8
_LANES = 128
_VMEM_LIMIT = 56 * 1024 * 1024
_ROW_TILE = 1024
_NORM_TILE = 1024
_LRU_TILE = 1024
_RW_TILE = 256
_RW_CHUNK = 64
_RW_GROUP = 256
_RW_SEQS = 2
_ML_ROW_TILE = 512
_ML_IN_COLS = 1024
_ML_OUT_TILE = 512
_ML_TILE = 256
_FFN_COLS = 256


def _cparams(n_axes):
    return pltpu.CompilerParams(dimension_semantics=("arbitrary",) * n_axes,
                                vmem_limit_bytes=_VMEM_LIMIT)


def _dot(a, b):
    return jnp.dot(a.astype(_BF16), b.astype(_BF16), preferred_element_type=_F32)


def _dot_nt(a, b):
    return lax.dot_general(a.astype(_BF16), b.astype(_BF16), (((1,), (1,)), ((), ())),
                           preferred_element_type=_F32)


def _dot_tn(a, b):
    return lax.dot_general(a.astype(_BF16), b.astype(_BF16), (((0,), (0,)), ((), ())),
                           preferred_element_type=_F32)


def _split3(x):
    hi = x.astype(_BF16)
    r1 = x - hi.astype(_F32)
    mid = r1.astype(_BF16)
    lo = (r1 - mid.astype(_F32)).astype(_BF16)
    return hi, mid, lo


def _dot01_left(m01, x):
    hi, mid, lo = _split3(x)
    f = lambda p: jnp.dot(m01, p, preferred_element_type=_F32)
    return (f(lo) + f(mid)) + f(hi)


def _dot01_right(x, m01):
    hi = x.astype(_BF16)
    lo = (x - hi.astype(_F32)).astype(_BF16)
    f = lambda p: jnp.dot(p, m01, preferred_element_type=_F32)
    return f(lo) + f(hi)


def _sigmoid(x):
    return 0.5 * jnp.tanh(0.5 * x) + 0.5


def _softplus(x):
    return jnp.maximum(x, 0.0) + jnp.log1p(jnp.exp(-jnp.abs(x)))


def _gelu_tanh(x):
    c = math.sqrt(2.0 / math.pi)
    return x * (0.5 * (1.0 + jnp.tanh(c * (x + 0.044715 * (x * x * x)))))


def _rms(x, g):
    ms = jnp.mean(x * x, axis=-1, keepdims=True)
    return x * lax.rsqrt(ms + _RMS_EPS) * g


def _halo_shifter(main, prev, nxt, ti=None, nt=None):
    rows = main.shape[0]
    if ti is not None:
        prev = jnp.where(ti == 0, 0.0, prev)
        nxt = jnp.where(ti == nt - 1, 0.0, nxt)
    ridx = lax.broadcasted_iota(jnp.int32, main.shape, 0)
    first, second, last = ridx == 0, ridx == 1, ridx == rows - 1

    def shift(k):
        rolled = pltpu.roll(main, k % rows, axis=0)
        if k == 1:
            return jnp.where(first, prev[_SUBLANES - 1:_SUBLANES], rolled)
        if k == 2:
            return jnp.where(first, prev[_SUBLANES - 2:_SUBLANES - 1],
                             jnp.where(second, prev[_SUBLANES - 1:_SUBLANES], rolled))
        assert k == -1
        return jnp.where(last, nxt[0:1], rolled)

    return shift


def _shifted(ext, k, rows, halo=_SUBLANES):
    if k == 0:
        return ext[halo:halo + rows]
    return pltpu.roll(ext, k % ext.shape[0], axis=0)[halo:halo + rows]


def _time_specs(width, rows, nt, t_len, reverse, col=0, bb=None):
    per = rows // _SUBLANES
    last = t_len // _SUBLANES - 1

    def tidx(i):
        return (nt - 1 - i) if reverse else i

    main = pl.BlockSpec((bb, rows, width), lambda b, i: (b, tidx(i), col))
    prev = pl.BlockSpec((bb, _SUBLANES, width),
                        lambda b, i: (b, jnp.maximum(tidx(i) * per - 1, 0), col))
    nxt = pl.BlockSpec((bb, _SUBLANES, width),
                       lambda b, i: (b, jnp.minimum((tidx(i) + 1) * per, last), col))
    return main, prev, nxt


def _const_spec(shape):
    nd = len(shape)
    return pl.BlockSpec(shape, lambda *_: (0,) * nd)


def _resident_spec(shape):
    nd = len(shape)
    return pl.BlockSpec(shape, lambda *_: (0,) * nd, pipeline_mode=pl.Buffered(1))


def _norm_proj_kernel(x_ref, g_ref, w_ref, *o_refs):
    h = _rms(x_ref[...], g_ref[...])
    res = _dot(h, w_ref[...])
    off = 0
    for o in o_refs:
        wdt = o.shape[-1]
        o[...] = res[:, off:off + wdt].astype(o.dtype)
        off += wdt


def _norm_proj(x, g, w, splits, dtypes, tm):
    b, t, d = x.shape
    n = w.shape[1]
    assert sum(splits) == n
    nt = t // tm
    row = lambda wd: pl.BlockSpec((None, tm, wd), lambda bi, i: (bi, i, 0))
    return pl.pallas_call(
        _norm_proj_kernel,
        grid=(b, nt),
        in_specs=[row(d), _const_spec((1, d)), _resident_spec((d, n))],
        out_specs=[row(s) for s in splits],
        out_shape=[jax.ShapeDtypeStruct((b, t, s), dt) for s, dt in zip(splits, dtypes)],
        compiler_params=_cparams(2),
        name="norm_proj",
    )(x, g, w)


def _lru_kernel(*refs, reverse, finalize, tt, nt):
    if finalize:
        xc_ref, wg_ref, bg_ref, lam_ref, hf_ref, gate_ref, o_ref, a_s, u_s, h_s, carry = refs
    else:
        (x_ref, xp_ref, xn_ref, cw_ref, cb_ref, wg_ref, bg_ref, lam_ref, o_ref, xc_ref,
         a_s, u_s, h_s, carry) = refs
    i = pl.program_id(1)
    ti = (nt - 1 - i) if reverse else i
    nseq, _, width = o_ref.shape
    nslab = width // _LANES
    pitch = a_s.shape[1] // nseq

    neg_c_sp = (-_LRU_C) * _softplus(-lam_ref[...])
    xcs = []
    for b in range(nseq):
        if finalize:
            xcs.append(xc_ref[b])
            continue
        cw = cw_ref[...]
        xb = x_ref[b]
        shift = _halo_shifter(xb, xp_ref[b], xn_ref[b], ti, nt)
        xc = cb_ref[...] + shift(2) * cw[0:1]
        xc = xc + shift(1) * cw[1:2]
        xc = xc + xb * cw[2:3]
        xc = xc + shift(-1) * cw[3:4]
        xc_ref[b] = xc
        xcs.append(xc)
    gates_all = _dot(jnp.concatenate(xcs, axis=0), wg_ref[...]) + bg_ref[...]
    for b in range(nseq):
        gates = gates_all[b * tt:(b + 1) * tt]
        r = _sigmoid(gates[:, :width])
        ig = _sigmoid(gates[:, width:])
        log_a = r * neg_c_sp
        th = jnp.tanh(log_a)
        a = jnp.exp(log_a)
        u = jnp.sqrt(-2.0 * th) * lax.rsqrt(1.0 - th) * (ig * xcs[b])
        for j in range(nslab):
            a_s[j, pl.ds(b * pitch, tt), :] = a[:, j * _LANES:(j + 1) * _LANES]
            u_s[j, pl.ds(b * pitch, tt), :] = u[:, j * _LANES:(j + 1) * _LANES]

    @pl.when(i == 0)
    def _():
        carry[...] = jnp.zeros_like(carry)

    def body(s, hs):
        t = (tt - 1 - s) if reverse else s
        new = []
        for j in range(nslab):
            rows = pl.ds(t, nseq, stride=pitch)
            h = a_s[j, rows, :] * hs[j] + u_s[j, rows, :]
            h_s[j, rows, :] = h
            new.append(h)
        return tuple(new)

    hs = lax.fori_loop(0, tt, body, tuple(carry[j] for j in range(nslab)), unroll=_SUBLANES)
    for j in range(nslab):
        carry[j] = hs[j]

    for b in range(nseq):
        h = jnp.concatenate([h_s[j, pl.ds(b * pitch, tt), :] for j in range(nslab)], axis=1)
        if finalize:
            h = (hf_ref[b] + h) * _gelu_tanh(gate_ref[b])
        o_ref[b] = h


def _lru_pass(p_lru, conv_w, conv_b, wg, bg, lam, first, reverse, tt):
    b, t, w2 = p_lru.shape
    w = w2 // 2
    nseq = max(n for n in (8, 4, 2, 1) if b % n == 0)
    tt = _tile(t, max(tt // nseq, _LANES))
    nt = t // tt
    pitch = tt + _SUBLANES if (tt // _SUBLANES) % 2 == 0 else tt
    nslab = w // _LANES
    finalize = first is not None
    main, prev, nxt = _time_specs(w, tt, nt, t, reverse, bb=nseq)
    tidx = (lambda i: nt - 1 - i) if reverse else (lambda i: i)
    tile = lambda col: pl.BlockSpec((nseq, tt, w), lambda bi, i: (bi, tidx(i), col))
    gate_consts = [wg, bg, lam]
    out_sds = jax.ShapeDtypeStruct((b, t, w), _F32)
    if finalize:
        hf, xc = first
        in_specs = [tile(0)] + [_const_spec(z.shape) for z in gate_consts] + [tile(0), tile(1)]
        args = [xc] + gate_consts + [hf, p_lru]
        out_specs, out_shape = tile(0), out_sds
    else:
        in_specs = ([main, prev, nxt, _const_spec(conv_w.shape), _const_spec(conv_b.shape)]
                    + [_const_spec(z.shape) for z in gate_consts])
        args = [p_lru, p_lru, p_lru, conv_w, conv_b] + gate_consts
        out_specs, out_shape = [tile(0), tile(0)], [out_sds, out_sds]
    scratch = [pltpu.VMEM((nslab, nseq * pitch, _LANES), _F32)] * 3
    scratch += [pltpu.VMEM((nslab, nseq, _LANES), _F32)]
    return pl.pallas_call(
        functools.partial(_lru_kernel, reverse=reverse, finalize=finalize, tt=tt, nt=nt),
        grid=(b // nseq, nt),
        in_specs=in_specs,
        out_specs=out_specs,
        out_shape=out_shape,
        scratch_shapes=scratch,
        compiler_params=_cparams(2),
        name="lru_bwd" if reverse else "lru_fwd",
    )(*args)


def _rwkv_kernel(*refs, reverse, finalize, tt, nt):
    (rw_ref, rwp_ref, rwn_ref, mu_ref, kk_ref, ka_ref, w0_ref, a0_ref, wup_ref, aup_ref,
     onesr_ref, onese_ref) = refs[:12]
    if finalize:
        yf_ref, rk_ref, lnw_ref, lnb_ref, gup_ref, o_ref, ht_s = refs[12:]
    else:
        o_ref, ht_s = refs[12:]
    i = pl.program_id(1)
    ti = (nt - 1 - i) if reverse else i
    nseq, ngrp, gw, _ = ht_s.shape
    c = ngrp * gw
    lc = _RW_CHUNK
    hpg = gw // _RW_HEAD
    ones_r = onesr_ref[...]
    ones_e = onese_ref[...]
    seqs = range(nseq)

    def segsum(x):
        return _dot01_right(_dot01_right(x, ones_r), ones_e)

    tok = []
    for b in seqs:
        x = rw_ref[b]
        tshift = _halo_shifter(x, rwp_ref[b], rwn_ref[b], ti, nt)
        rws = x + mu_ref[...] * (0.5 * (tshift(1) + tshift(-1)) - x)
        r = rws[:, 0:c]
        k = rws[:, c:2 * c]
        v = rws[:, 2 * c:3 * c]
        wad = rws[:, 3 * c:3 * c + _LANES]
        kk = k * kk_ref[...]
        kk = kk * jnp.minimum(lax.rsqrt(segsum(kk * kk)), 1e12)
        lw = (-_RW_DECAY_SCALE) * _sigmoid(w0_ref[...] + _dot(jnp.tanh(wad), wup_ref[...]))
        a = _sigmoid(a0_ref[...] + _dot(wad, aup_ref[...]))
        tok.append(dict(rws=rws, r=r, k=k, v=v, kk=kk, lw=lw, kt=k * (1.0 + (a - 1.0) * ka_ref[...]),
                        bvec=a * kk))

    shift = _RW_HEAD.bit_length() - 1
    row = lax.broadcasted_iota(jnp.int32, (gw, gw), 0)
    col = lax.broadcasted_iota(jnp.int32, (gw, gw), 1)
    same = (row >> shift) == (col >> shift)
    trow = lax.broadcasted_iota(jnp.int32, (lc, gw), 0)
    tcol = lax.broadcasted_iota(jnp.int32, (lc, gw), 1) & (lc - 1)
    m_incl = (tcol >= trow) if reverse else (tcol <= trow)
    m_strict = (tcol > trow) if reverse else (tcol < trow)
    eye_cat = jnp.where(tcol == trow, 1.0, 0.0)
    srow = lax.broadcasted_iota(jnp.int32, (lc, lc), 0)
    scol = lax.broadcasted_iota(jnp.int32, (lc, lc), 1)
    tri01 = ((scol >= srow) if reverse else (scol <= srow)).astype(_BF16)

    def stack(xv):
        xb = xv.astype(_BF16)
        return jnp.where(same, jnp.concatenate([xb] * hpg, axis=0), jnp.zeros((), _BF16))

    @pl.when(i == 0)
    def _():
        ht_s[...] = jnp.zeros_like(ht_s)

    nchunk = tt // lc
    chunks = list(range(nchunk - 1, -1, -1) if reverse else range(nchunk))
    groups = range(ngrp)
    lanes = [slice(g * gw, (g + 1) * gw) for g in groups]
    units = [(b, ci, g) for ci in chunks for b in seqs for g in groups]

    pre = {}
    for ci in chunks:
        sl = slice(ci * lc, (ci + 1) * lc)
        for b in seqs:
            tb = tok[b]
            r_c, lw_c, k_c, v_c, kk_c, b_c = (tb[n][sl] for n in ("r", "lw", "kt", "v", "kk", "bvec"))
            cum = _dot01_left(tri01, lw_c)
            clast = cum[0:1] if reverse else cum[lc - 1:lc]
            einv = jnp.exp(-cum)
            eend = jnp.exp(clast - cum)
            bf = lambda z: z.astype(_BF16)
            pre[b, ci] = dict(gamma=jnp.exp(clast), rg=bf(r_c * jnp.exp(cum)), kkg=bf(kk_c * jnp.exp(cum - lw_c)),
                              kd=bf(k_c * einv), bd=bf(b_c * einv), kend=bf(k_c * eend), bend=bf(b_c * eend),
                              v=bf(v_c))

    st = {}
    for b, ci, g in units:
        ls = lanes[g]
        pc = pre[b, ci]
        lhs = jnp.concatenate([pc["kkg"][:, ls], pc["rg"][:, ls]], axis=0)
        rhs = jnp.concatenate([stack(pc["kd"][:, ls]), stack(pc["bd"][:, ls])], axis=0)
        amat = _dot_nt(lhs, rhs)
        nj = -jnp.where(m_strict, amat[:lc, gw:], 0.0)
        st[b, ci, g] = dict(a_kk_k=jnp.where(m_strict, amat[:lc, :gw], 0.0).astype(_BF16),
                            a_r_k=jnp.where(m_incl, amat[lc:, :gw], 0.0).astype(_BF16),
                            a_r_b=jnp.where(m_incl, amat[lc:, gw:], 0.0).astype(_BF16),
                            nj=nj.astype(_BF16), tinv=eye_cat + nj)
    for key in units:
        s = st[key]
        s["nj"] = _dot(s["nj"], stack(s["nj"])).astype(_BF16)
    for _ in range(lc.bit_length() - 3):
        for key in units:
            s = st[key]
            pn = _dot(jnp.concatenate([s["tinv"].astype(_BF16), s["nj"]], axis=0), stack(s["nj"]))
            s["tinv"] = s["tinv"] + pn[:lc]
            s["nj"] = pn[lc:].astype(_BF16)
    for key in units:
        s = st[key]
        s["tinv"] = s["tinv"] + _dot(s["tinv"], stack(s["nj"]))
    for b, ci, g in units:
        s = st[b, ci, g]
        s["av"] = _dot(jnp.concatenate([s["a_kk_k"], s["a_r_k"]], axis=0), stack(pre[b, ci]["v"][:, lanes[g]]))
    for b, ci, g in units:
        s = st[b, ci, g]
        tk = _dot(s["tinv"], jnp.concatenate([stack(pre[b, ci]["kkg"][:, lanes[g]]), stack(s["av"][:lc])],
                                             axis=1))
        s["kkgp"], s["uv"] = tk[:, :gw].astype(_BF16), tk[:, gw:]
    for b, ci, g in units:
        ls = lanes[g]
        s = st[b, ci, g]
        pc = pre[b, ci]
        s["kb"] = jnp.where(same, _dot_tn(s["kkgp"], pc["bend"][:, ls]), 0.0).astype(_BF16)
        s["gmat"] = jnp.where(same, _dot_tn(jnp.concatenate([pc["v"][:, ls], (-s["uv"]).astype(_BF16)], axis=0),
                                            jnp.concatenate([pc["kend"][:, ls], pc["bend"][:, ls]], axis=0)),
                              0.0)

    chains = [(b, g) for b in seqs for g in groups]
    hts = {key: ht_s[key[0], key[1]] for key in chains}
    ys = {}
    for ci in chunks:
        cur = dict(hts)
        for b, g in chains:
            s = st[b, ci, g]
            hts[b, g] = cur[b, g] * pre[b, ci]["gamma"][:, lanes[g]] - _dot(cur[b, g], s["kb"]) + s["gmat"]
        pus = {(b, g): _dot_nt(jnp.concatenate([st[b, ci, g]["kkgp"], pre[b, ci]["rg"][:, lanes[g]]], axis=0),
                               cur[b, g]) for b, g in chains}
        us = {key: pus[key][:lc] + st[key[0], ci, key[1]]["uv"] for key in chains}
        for b in seqs:
            ys[b, ci] = jnp.concatenate(
                [pus[b, g][lc:] + st[b, ci, g]["av"][lc:] - _dot(st[b, ci, g]["a_r_b"], stack(us[b, g]))
                 for g in groups], axis=1)
    for b, g in chains:
        ht_s[b, g] = hts[b, g]

    for b in seqs:
        y = jnp.concatenate([ys[b, ci] for ci in range(nchunk)], axis=0)
        if finalize:
            tb = tok[b]
            y = yf_ref[b] + y
            inv_n = 1.0 / _RW_HEAD
            mean = segsum(y) * inv_n
            d = y - mean
            var = segsum(d * d) * inv_n
            yn = d * lax.rsqrt(var + _RW_GN_EPS) * lnw_ref[...] + lnb_ref[...]
            bonus = segsum(tb["r"] * tb["k"] * rk_ref[...]) * tb["v"]
            gd = tb["rws"][:, 3 * c + _LANES:3 * c + 2 * _LANES]
            y = (yn + bonus) * _dot(_sigmoid(gd), gup_ref[...])
        o_ref[b] = y


def _rwkv_pass(p_rw, prm, d, yf, reverse, tt):
    b, t, cols = p_rw.shape
    c = prm["k_k"].shape[-1]
    nt = t // tt
    nseq = _RW_SEQS if b % _RW_SEQS == 0 else 1
    finalize = yf is not None
    main, prev, nxt = _time_specs(cols, tt, nt, t, reverse, bb=nseq)
    tidx = (lambda i: nt - 1 - i) if reverse else (lambda i: i)
    tile = pl.BlockSpec((nseq, tt, c), lambda bi, i: (bi, tidx(i), 0))
    consts = [prm["mu"], prm["k_k"], prm["k_a"], prm["w0"][d], prm["a0"][d], prm["w_up"][d],
              prm["a_up"][d], prm["ones_r"], prm["ones_e"]]
    args = [p_rw, p_rw, p_rw] + consts
    in_specs = [main, prev, nxt] + [_const_spec(z.shape) for z in consts]
    if finalize:
        extra = [prm["r_k"], prm["ln_w"], prm["ln_b"], prm["g_up"]]
        args += [yf] + extra
        in_specs += [tile] + [_const_spec(z.shape) for z in extra]
    return pl.pallas_call(
        functools.partial(_rwkv_kernel, reverse=reverse, finalize=finalize, tt=tt, nt=nt),
        grid=(b // nseq, nt),
        in_specs=in_specs,
        out_specs=tile,
        out_shape=jax.ShapeDtypeStruct((b, t, c), _F32),
        scratch_shapes=[pltpu.VMEM((nseq, c // _RW_GROUP, _RW_GROUP, _RW_GROUP), _F32)],
        compiler_params=_cparams(2),
        name="rwkv_bwd" if reverse else "rwkv_fwd",
    )(*args)


def _out_proj2_kernel(x_ref, a_ref, b_ref, wa_ref, wb_ref, o_ref):
    o_ref[...] = x_ref[...] + _dot(a_ref[...], wa_ref[...]) + _dot(b_ref[...], wb_ref[...])


def _out_proj2(x, a, bm, wa, wb, tm):
    b, t, d = x.shape
    nt = t // tm
    row = lambda wd: pl.BlockSpec((None, tm, wd), lambda bi, i: (bi, i, 0))
    return pl.pallas_call(
        _out_proj2_kernel,
        grid=(b, nt),
        in_specs=[row(d), row(a.shape[-1]), row(bm.shape[-1]), _const_spec(wa.shape),
                  _const_spec(wb.shape)],
        out_specs=row(d),
        out_shape=jax.ShapeDtypeStruct((b, t, d), _F32),
        compiler_params=_cparams(2),
        name="even_out_proj",
    )(x, a, bm, wa, wb)


def _ffn_kernel(*refs, tm, nt, ck, final_norm):
    x_ref, xp_ref, xn_ref, g_ref, wup_ref, cw_ref, cb_ref, wd_ref = refs[:8]
    if final_norm:
        gf_ref, o_ref, h_s, act_s = refs[8:]
    else:
        o_ref, h_s, act_s = refs[8:]
    i = pl.program_id(1)
    dff = wd_ref.shape[0]
    halo = 2 * _SUBLANES
    g = g_ref[...]
    zeros = jnp.zeros((_SUBLANES, x_ref.shape[-1]), _F32)
    prev = jnp.where(i == 0, 0.0, _rms(xp_ref[...], g))
    nxt = jnp.where(i == nt - 1, 0.0, _rms(xn_ref[...], g))
    h_s[...] = jnp.concatenate([zeros, prev, _rms(x_ref[...], g), nxt, zeros], axis=0).astype(_BF16)
    hext = h_s[...]
    hmain = h_s[pl.ds(halo, tm), :]
    cw = cw_ref[...]
    cb = cb_ref[...]
    for c in range(dff // ck):
        cs = slice(c * ck, (c + 1) * ck)
        ug = jnp.dot(hext, wup_ref[:, dff + c * ck:dff + (c + 1) * ck], preferred_element_type=_F32)
        gate = cb[:, cs] + _shifted(ug, 1, tm, halo) * cw[0:1, cs]
        gate = gate + _shifted(ug, 0, tm, halo) * cw[1:2, cs]
        gate = gate + _shifted(ug, -1, tm, halo) * cw[2:3, cs]
        val = jnp.dot(hmain, wup_ref[:, cs], preferred_element_type=_F32)
        act_s[:, cs] = (_gelu_tanh(gate) * val).astype(_BF16)
    y = x_ref[...] + jnp.dot(act_s[...], wd_ref[...], preferred_element_type=_F32)
    if final_norm:
        y = _rms(y, gf_ref[...])
    o_ref[...] = y


def _ffn(x, g, w_up, conv_w, conv_b, w_down, g_final, tm, ck):
    b, t, d = x.shape
    dff = w_down.shape[0]
    nt = t // tm
    final_norm = g_final is not None
    main, prev, nxt = _time_specs(d, tm, nt, t, False)
    consts = [g, w_up, conv_w, conv_b, w_down] + ([g_final] if final_norm else [])
    return pl.pallas_call(
        functools.partial(_ffn_kernel, tm=tm, nt=nt, ck=ck, final_norm=final_norm),
        grid=(b, nt),
        in_specs=[main, prev, nxt] + [_resident_spec(z.shape) for z in consts],
        out_specs=main,
        out_shape=jax.ShapeDtypeStruct((b, t, d), _F32),
        scratch_shapes=[pltpu.VMEM((tm + 4 * _SUBLANES, d), _BF16), pltpu.VMEM((tm, dff), _BF16)],
        compiler_params=_cparams(2),
        name="conv_ffn",
    )(x, x, x, *consts)


def _mlstm_in_kernel(x_ref, xp_ref, xn_ref, g_ref, w_ref, cw_ref, cb_ref, wq_ref, wk_ref, wv_ref,
                     wg_ref, bg_ref, q_ref, k_ref, v_ref, xc_ref, z_ref, gt_ref, *, tm, nt, cols):
    i = pl.program_id(1)
    inner = z_ref.shape[-1]
    g = g_ref[...]
    hext = jnp.concatenate([_rms(xp_ref[...], g), _rms(x_ref[...], g), _rms(xn_ref[...], g)], axis=0)
    hext_b = hext.astype(_BF16)
    hmain_b = hext[_SUBLANES:_SUBLANES + tm].astype(_BF16)
    cw = cw_ref[...]
    cb = cb_ref[...]
    qscale = (inner // _ML_HEADS) ** -0.5
    nblk = inner // cols
    proj = lambda j: jnp.dot(hext_b, w_ref[:, j * cols:(j + 1) * cols], preferred_element_type=_F32)
    gates = bg_ref[...]
    xme_next = proj(0)
    for j in range(nblk):
        cs = slice(j * cols, (j + 1) * cols)
        xme = xme_next
        if j + 1 < nblk:
            xme_next = proj(j + 1)
        z_ref[:, cs] = jnp.dot(hmain_b, w_ref[:, inner + j * cols:inner + (j + 1) * cols],
                               preferred_element_type=_F32).astype(z_ref.dtype)
        prev = jnp.where(i == 0, 0.0, xme[:_SUBLANES])
        nxt = jnp.where(i == nt - 1, 0.0, xme[_SUBLANES + tm:])
        xm = xme[_SUBLANES:_SUBLANES + tm]
        shift = _halo_shifter(xm, prev, nxt)
        zc = cb[:, cs] + shift(2) * cw[0:1, cs]
        zc = zc + shift(1) * cw[1:2, cs]
        zc = zc + xm * cw[2:3, cs]
        zc = zc + shift(-1) * cw[3:4, cs]
        xc = zc * _sigmoid(zc)
        xc_ref[:, cs] = xc.astype(xc_ref.dtype)
        for jb in range(cols // _ML_BLK):
            blk = j * (cols // _ML_BLK) + jb
            sub = slice(jb * _ML_BLK, (jb + 1) * _ML_BLK)
            sl = slice(blk * _ML_BLK, (blk + 1) * _ML_BLK)
            q = _dot(xc[:, sub], wq_ref[blk])
            k = _dot(xc[:, sub], wk_ref[blk])
            v = _dot(xm[:, sub], wv_ref[blk])
            gates = (gates + _dot(q, wg_ref[sl])
                     + _dot(k, wg_ref[inner + blk * _ML_BLK:inner + (blk + 1) * _ML_BLK])
                     + _dot(v, wg_ref[2 * inner + blk * _ML_BLK:2 * inner + (blk + 1) * _ML_BLK]))
            q_ref[:, sl] = (q * qscale).astype(q_ref.dtype)
            k_ref[:, sl] = k.astype(k_ref.dtype)
            v_ref[:, sl] = v.astype(v_ref.dtype)
    gt_ref[...] = gates


def _mlstm_in(x, g, w_in, prm, tm):
    b, t, d = x.shape
    inner = w_in.shape[1] // 2
    nt = t // tm
    cols = _ML_IN_COLS if inner % _ML_IN_COLS == 0 else inner
    main, prev, nxt = _time_specs(d, tm, nt, t, False)
    consts = [g, w_in, prm["conv_w"], prm["conv_b"], prm["wq"], prm["wk"], prm["wv"], prm["wg"], prm["bg"]]
    row = lambda wd: pl.BlockSpec((None, tm, wd), lambda bi, i: (bi, i, 0))
    big = jax.ShapeDtypeStruct((b, t, inner), _BF16)
    return pl.pallas_call(
        functools.partial(_mlstm_in_kernel, tm=tm, nt=nt, cols=cols),
        grid=(b, nt),
        in_specs=[main, prev, nxt] + [_resident_spec(z.shape) for z in consts],
        out_specs=[row(inner)] * 5 + [row(_LANES)],
        out_shape=[big] * 5 + [jax.ShapeDtypeStruct((b, t, _LANES), _F32)],
        compiler_params=_cparams(2),
        name="mlstm_in",
    )(x, x, x, *consts)


def _mlstm_kernel(*refs, lc, dirs):
    nd = len(dirs)
    ins, outs = refs[:4 * nd], refs[4 * nd:5 * nd]
    c_s, n_s, m_s = refs[5 * nd:]
    i = pl.program_id(1)
    dh = c_s.shape[-1]

    @pl.when(i == 0)
    def _():
        c_s[...] = jnp.zeros_like(c_s)
        n_s[...] = jnp.zeros_like(n_s)
        m_s[...] = jnp.zeros_like(m_s)

    lane = lax.broadcasted_iota(jnp.int32, (lc, _LANES), 1)
    row = lax.broadcasted_iota(jnp.int32, (lc, lc), 0)
    col = lax.broadcasted_iota(jnp.int32, (lc, lc), 1)

    def lane_col(mat, j):
        return jnp.sum(jnp.where(lane == j, mat, 0.0), axis=-1, keepdims=True)

    units = [(x, h) for x in range(nd) for h in range(_ML_HEADS)]
    tri, gates, bcum_all = [], [], []
    for x, (d, reverse) in enumerate(dirs):
        tri.append((col >= row) if reverse else (col <= row))
        gates.append(ins[4 * x + 3][...])
        bcum_all.append(_dot01_left(tri[x].astype(_BF16), -_softplus(-gates[x])))
    sls = [slice(h * dh, (h + 1) * dh) for h in range(_ML_HEADS)]
    li, bcum, m_prev, src_row, dmat, inter, m_t, g, wlog, m_new, wexp, wexp_row, decay = ({} for _ in range(13))
    for u in units:
        x, h = u
        d, reverse = dirs[x]
        li[u] = lane_col(gates[x], d * _ML_HEADS + h)
        bcum[u] = lane_col(bcum_all[x], 2 * _ML_HEADS + d * _ML_HEADS + h)
        m_prev[u] = m_s[x * _ML_HEADS + h][:, 0:1]
        src_row[u] = jnp.broadcast_to(li[u] - bcum[u], (lc, lc)).T
        dmat[u] = jnp.where(tri[x], bcum[u] + src_row[u], -jnp.inf)
        inter[u] = bcum[u] + m_prev[u]
        m_t[u] = jnp.maximum(inter[u], jnp.max(dmat[u], axis=-1, keepdims=True))
        g[u] = bcum[u][0:1] if reverse else bcum[u][lc - 1:lc]
        wlog[u] = g[u] - bcum[u] + li[u]
        m_new[u] = jnp.maximum(g[u] + m_prev[u], jnp.max(wlog[u], axis=0, keepdims=True))
        wexp[u] = jnp.exp(wlog[u] - m_new[u])
        wexp_row[u] = jnp.exp(g[u] + src_row[u][0:2 * _SUBLANES] - m_new[u])
        decay[u] = jnp.exp(g[u] + m_prev[u] - m_new[u])
    q = {(x, h): ins[4 * x][:, sls[h]] for x, h in units}
    k = {(x, h): ins[4 * x + 1][:, sls[h]] for x, h in units}
    v = {(x, h): ins[4 * x + 2][:, sls[h]] for x, h in units}
    cmat = {(x, h): c_s[x * _ML_HEADS + h] for x, h in units}
    nvec = {(x, h): n_s[x * _ML_HEADS + h] for x, h in units}
    s_raw = {u: _dot_nt(q[u], k[u]) for u in units}
    cq = {u: _dot_nt(q[u], cmat[u]) for u in units}
    qn = {u: _dot_nt(q[u], jnp.broadcast_to(nvec[u], (_LANES, dh)))[:, 0:1] for u in units}
    upd = {u: _dot_tn(v[u] * wexp[u].astype(v[u].dtype), k[u]) for u in units}
    nupd = {u: _dot(wexp_row[u], k[u])[0:1] for u in units}
    s = {u: s_raw[u] * jnp.exp(dmat[u] - m_t[u]) for u in units}
    pv = {u: _dot(s[u], v[u]) for u in units}
    for u in units:
        x, h = u
        scale = jnp.exp(inter[u] - m_t[u])
        num = pv[u] + scale * cq[u]
        den = jnp.sum(s[u], axis=-1, keepdims=True) + scale * qn[u]
        outs[x][:, sls[h]] = (num / jnp.maximum(jnp.abs(den), jnp.exp(-m_t[u]))).astype(outs[x].dtype)
        c_s[x * _ML_HEADS + h] = decay[u] * cmat[u] + upd[u]
        n_s[x * _ML_HEADS + h] = decay[u] * nvec[u] + nupd[u]
        m_s[x * _ML_HEADS + h] = jnp.broadcast_to(m_new[u], m_s.shape[1:])


def _mlstm_pass(q, k, v, gt, lc, dirs):
    b, t, inner = q.shape
    nc = t // lc
    dh = inner // _ML_HEADS
    nd = len(dirs)
    in_specs, args = [], []
    out_specs = []
    for _, reverse in dirs:
        tidx = (lambda i: nc - 1 - i) if reverse else (lambda i: i)
        row = lambda wd, tidx=tidx: pl.BlockSpec((None, lc, wd), lambda bi, i: (bi, tidx(i), 0))
        in_specs += [row(inner)] * 3 + [row(_LANES)]
        args += [q, k, v, gt]
        out_specs.append(row(inner))
    return pl.pallas_call(
        functools.partial(_mlstm_kernel, lc=lc, dirs=tuple(dirs)),
        grid=(b, nc),
        in_specs=in_specs,
        out_specs=out_specs,
        out_shape=[jax.ShapeDtypeStruct((b, t, inner), _BF16)] * nd,
        scratch_shapes=[pltpu.VMEM((nd * _ML_HEADS, dh, dh), _F32), pltpu.VMEM((nd * _ML_HEADS, 1, dh), _F32),
                        pltpu.VMEM((nd * _ML_HEADS, 1, _LANES), _F32)],
        compiler_params=_cparams(2),
        name="mlstm_scan",
    )(*args)


def _mlstm_out_kernel(x_ref, hf_ref, hb_ref, xc_ref, z_ref, nw_ref, skip_ref, wo_ref, o_ref):
    inner = hf_ref.shape[-1]
    dh = inner // _ML_HEADS
    acc = x_ref[...]
    for h in range(_ML_HEADS):
        sl = slice(h * dh, (h + 1) * dh)
        hh = hf_ref[:, sl].astype(_F32) + hb_ref[:, sl].astype(_F32)
        mean = jnp.mean(hh, axis=-1, keepdims=True)
        dlt = hh - mean
        var = jnp.mean(dlt * dlt, axis=-1, keepdims=True)
        hn = dlt * lax.rsqrt(var + _ML_LN_EPS) * nw_ref[:, sl]
        z = z_ref[:, sl].astype(_F32)
        out = (hn + skip_ref[:, sl] * xc_ref[:, sl].astype(_F32)) * (z * _sigmoid(z))
        acc = acc + _dot(out, wo_ref[sl, :])
    o_ref[...] = acc


def _mlstm_out(x, hf, hb, xc, z, norm_w, skip, w_out, tm):
    b, t, d = x.shape
    inner = hf.shape[-1]
    nt = t // tm
    row = lambda wd: pl.BlockSpec((None, tm, wd), lambda bi, i: (bi, i, 0))
    return pl.pallas_call(
        _mlstm_out_kernel,
        grid=(b, nt),
        in_specs=[row(d)] + [row(inner)] * 4 + [_const_spec(norm_w.shape), _const_spec(skip.shape),
                                                _const_spec(w_out.shape)],
        out_specs=row(d),
        out_shape=jax.ShapeDtypeStruct((b, t, d), _F32),
        compiler_params=_cparams(2),
        name="mlstm_out_proj",
    )(x, hf, hb, xc, z, norm_w, skip, w_out)


def _block_diag_dense(w):
    nb, bi, bo = w.shape
    eye = jnp.eye(nb, dtype=w.dtype)
    return jnp.einsum("nio,nm->nimo", w, eye).reshape(nb * bi, nb * bo)


def _block_diag_blocks(w, blk):
    nb, bi, bo = w.shape
    per = blk // bi
    return jax.vmap(_block_diag_dense)(w.reshape(nb // per, per, bi, bo))


def _row(v):
    return v.reshape(1, -1).astype(_F32)


def _tile(t_len, want):
    return min(want, t_len)


def _even_layer(x, li, p):
    (w_in, conv_w, conv_b, wa, ba, wx, bx, lam, mu, w0, w_up, a0, a_up, g_up, k_k, k_a, r_k,
     ln_w, ln_b, w_out) = [z[li] for z in p["even"]]
    t = x.shape[1]
    width = lam.shape[-1]
    c = k_k.shape[-1]
    p_lru, p_rw = _norm_proj(x, _row(p["norm_mix"][2 * li]), w_in.astype(_BF16),
                             (2 * width, w_in.shape[1] - 2 * width), (_F32, _F32),
                             _tile(t, _NORM_TILE))

    tt = _tile(t, _LRU_TILE)
    h_dir = None
    for d, rev in enumerate((False, True)):
        wg = jnp.concatenate([_block_diag_dense(wa[d]), _block_diag_dense(wx[d])], axis=1).astype(_BF16)
        bg = jnp.concatenate([ba[d], bx[d]]).reshape(1, -1)
        h_dir = _lru_pass(p_lru, conv_w, _row(conv_b), wg, bg, _row(lam[d]), h_dir, rev, tt)
    lru_out = h_dir

    zeros = jnp.zeros((_RW_HEAD, c), _F32)
    head = jnp.arange(c) // _RW_HEAD
    prm = {
        "mu": _row(mu), "k_k": _row(k_k), "k_a": _row(k_a),
        "w0": [_row(w0[d]) for d in range(2)], "a0": [_row(a0[d]) for d in range(2)],
        "w_up": [jnp.concatenate([w_up[d], zeros], axis=0).astype(_BF16) for d in range(2)],
        "a_up": [jnp.concatenate([zeros, a_up[d]], axis=0).astype(_BF16) for d in range(2)],
        "ones_r": (head[:, None] == jnp.arange(_LANES)[None, :]).astype(_BF16),
        "ones_e": (jnp.arange(_LANES)[:, None] == head[None, :]).astype(_BF16),
        "r_k": _row(r_k), "ln_w": _row(ln_w), "ln_b": _row(ln_b), "g_up": g_up.astype(_BF16),
    }
    tt = _tile(t, _RW_TILE)
    y_dir = None
    for d, rev in enumerate((False, True)):
        y_dir = _rwkv_pass(p_rw, prm, d, y_dir, rev, tt)
    rw_out = y_dir

    wo = w_out.astype(_BF16)
    return _out_proj2(x, lru_out, rw_out, wo[:width], wo[width:], _tile(t, _ROW_TILE))


def _odd_layer(x, li, p):
    (w_in, conv_w, conv_b, w_q, w_k, w_v, w_ig, b_ig, w_fg, b_fg, skip, norm_w,
     w_out) = [z[li] for z in p["odd"]]
    t = x.shape[1]
    inner = skip.shape[-1]
    nh = b_ig.shape[-1]
    wg = jnp.concatenate([w_ig[0], w_ig[1], w_fg[0], w_fg[1],
                          jnp.zeros((3 * inner, _LANES - 4 * nh), _F32)], axis=1).astype(_BF16)
    bg = jnp.concatenate([b_ig[0], b_ig[1], b_fg[0], b_fg[1], jnp.zeros((_LANES - 4 * nh,), _F32)])
    prm = {
        "conv_w": conv_w, "conv_b": _row(conv_b),
        "wq": _block_diag_blocks(w_q, _ML_BLK).astype(_BF16),
        "wk": _block_diag_blocks(w_k, _ML_BLK).astype(_BF16),
        "wv": _block_diag_blocks(w_v, _ML_BLK).astype(_BF16),
        "wg": wg, "bg": _row(bg),
    }
    q, k, v, xc, z, gt = _mlstm_in(x, _row(p["norm_mix"][2 * li + 1]), w_in.astype(_BF16), prm,
                                   _tile(t, _ML_ROW_TILE))
    lc = _tile(t, _ML_TILE)
    (hf,) = _mlstm_pass(q, k, v, gt, lc, ((0, False),))
    (hb,) = _mlstm_pass(q, k, v, gt, lc, ((1, True),))
    return _mlstm_out(x, hf, hb, xc, z, _row(norm_w), _row(skip), w_out.astype(_BF16),
                      _tile(t, _ML_OUT_TILE))


def _trunk(x, p):
    depth = p["norm_mix"].shape[0]
    t = x.shape[1]
    for layer in range(depth):
        if layer % 2 == 0:
            x = _even_layer(x, layer // 2, p)
        else:
            x = _odd_layer(x, layer // 2, p)
        w_up, conv_w, conv_b, w_down = [z[layer] for z in p["ffn"]]
        dff = w_down.shape[0]
        ck = _FFN_COLS if dff % _FFN_COLS == 0 else dff
        g_final = _row(p["norm_final"]) if layer == depth - 1 else None
        x = _ffn(x, _row(p["norm_ffn"][layer]), w_up.astype(_BF16), conv_w, _row(conv_b),
                 w_down.astype(_BF16), g_final, _tile(t, _ROW_TILE), ck)
    return x


def kernel(x_prompt, x_sample, ev_w_in, ev_lru_conv_w, ev_lru_conv_b, ev_lru_wa, ev_lru_ba, ev_lru_wx, ev_lru_bx, ev_lru_lambda, ev_rw_mu, ev_rw_w0, ev_rw_w_up, ev_rw_a0, ev_rw_a_up, ev_rw_g_up, ev_rw_k_k, ev_rw_k_a, ev_rw_r_k, ev_rw_ln_w, ev_rw_ln_b, ev_w_out, od_w_in, od_conv_w, od_conv_b, od_w_q, od_w_k, od_w_v, od_w_ig, od_b_ig, od_w_fg, od_b_fg, od_skip, od_norm_w, od_w_out, ffn_w_up, ffn_conv_w, ffn_conv_b, ffn_w_down, norm_mix, norm_ffn, norm_final):
    p = {
        "even": (ev_w_in, ev_lru_conv_w, ev_lru_conv_b, ev_lru_wa, ev_lru_ba, ev_lru_wx, ev_lru_bx,
                 ev_lru_lambda, ev_rw_mu, ev_rw_w0, ev_rw_w_up, ev_rw_a0, ev_rw_a_up, ev_rw_g_up,
                 ev_rw_k_k, ev_rw_k_a, ev_rw_r_k, ev_rw_ln_w, ev_rw_ln_b, ev_w_out),
        "odd": (od_w_in, od_conv_w, od_conv_b, od_w_q, od_w_k, od_w_v, od_w_ig, od_b_ig, od_w_fg,
                od_b_fg, od_skip, od_norm_w, od_w_out),
        "ffn": (ffn_w_up, ffn_conv_w, ffn_conv_b, ffn_w_down),
        "norm_mix": norm_mix, "norm_ffn": norm_ffn, "norm_final": norm_final,
    }
    return (_trunk(x_prompt, p), _trunk(x_sample, p))
```

```python
import functools
import math

import jax
import jax.numpy as jnp
from jax import lax
from jax.experimental import pallas as pl
from jax.experimental.pallas import tpu as pltpu

_F32 = jnp.float32
_BF16 = jnp.bfloat16

_RMS_EPS = 1e-6
_LRU_C = 8.0
_RW_HEAD = 64
_RW_DECAY_SCALE = math.exp(-0.5)
_RW_GN_EPS = 64e-5
_ML_HEADS = 4
_ML_LN_EPS = 1e-5
_ML_BLK = 256

_SUBLANES = 8
_LANES = 128
_VMEM_LIMIT = 56 * 1024 * 1024
_ROW_TILE = 1024
_NORM_TILE = 1024
_LRU_TILE = 1024
_RW_TILE = 256
_RW_CHUNK = 64
_RW_GROUP = 256
_RW_SEQS = 2
_ML_ROW_TILE = 512
_ML_IN_COLS = 1024
_ML_OUT_TILE = 512
_ML_TILE = 256
_FFN_COLS = 256


def _cparams(n_axes):
    return pltpu.CompilerParams(dimension_semantics=("arbitrary",) * n_axes,
                                vmem_limit_bytes=_VMEM_LIMIT)


def _dot(a, b):
    return jnp.dot(a.astype(_BF16), b.astype(_BF16), preferred_element_type=_F32)


def _dot_nt(a, b):
    return lax.dot_general(a.astype(_BF16), b.astype(_BF16), (((1,), (1,)), ((), ())),
                           preferred_element_type=_F32)


def _dot_tn(a, b):
    return lax.dot_general(a.astype(_BF16), b.astype(_BF16), (((0,), (0,)), ((), ())),
                           preferred_element_type=_F32)


def _split3(x):
    hi = x.astype(_BF16)
    r1 = x - hi.astype(_F32)
    mid = r1.astype(_BF16)
    lo = (r1 - mid.astype(_F32)).astype(_BF16)
    return hi, mid, lo


def _dot01_left(m01, x):
    hi, mid, lo = _split3(x)
    f = lambda p: jnp.dot(m01, p, preferred_element_type=_F32)
    return (f(lo) + f(mid)) + f(hi)


def _dot01_right(x, m01):
    hi = x.astype(_BF16)
    lo = (x - hi.astype(_F32)).astype(_BF16)
    f = lambda p: jnp.dot(p, m01, preferred_element_type=_F32)
    return f(lo) + f(hi)


def _sigmoid(x):
    return 0.5 * jnp.tanh(0.5 * x) + 0.5


def _softplus(x):
    return jnp.maximum(x, 0.0) + jnp.log1p(jnp.exp(-jnp.abs(x)))


def _gelu_tanh(x):
    c = math.sqrt(2.0 / math.pi)
    return x * (0.5 * (1.0 + jnp.tanh(c * (x + 0.044715 * (x * x * x)))))


def _rms(x, g):
    ms = jnp.mean(x * x, axis=-1, keepdims=True)
    return x * lax.rsqrt(ms + _RMS_EPS) * g


def _halo_shifter(main, prev, nxt, ti=None, nt=None):
    rows = main.shape[0]
    if ti is not None:
        prev = jnp.where(ti == 0, 0.0, prev)
        nxt = jnp.where(ti == nt - 1, 0.0, nxt)
    ridx = lax.broadcasted_iota(jnp.int32, main.shape, 0)
    first, second, last = ridx == 0, ridx == 1, ridx == rows - 1

    def shift(k):
        rolled = pltpu.roll(main, k % rows, axis=0)
        if k == 1:
            return jnp.where(first, prev[_SUBLANES - 1:_SUBLANES], rolled)
        if k == 2:
            return jnp.where(first, prev[_SUBLANES - 2:_SUBLANES - 1],
                             jnp.where(second, prev[_SUBLANES - 1:_SUBLANES], rolled))
        assert k == -1
        return jnp.where(last, nxt[0:1], rolled)

    return shift


def _shifted(ext, k, rows, halo=_SUBLANES):
    if k == 0:
        return ext[halo:halo + rows]
    return pltpu.roll(ext, k % ext.shape[0], axis=0)[halo:halo + rows]


def _time_specs(width, rows, nt, t_len, reverse, col=0, bb=None):
    per = rows // _SUBLANES
    last = t_len // _SUBLANES - 1

    def tidx(i):
        return (nt - 1 - i) if reverse else i

    main = pl.BlockSpec((bb, rows, width), lambda b, i: (b, tidx(i), col))
    prev = pl.BlockSpec((bb, _SUBLANES, width),
                        lambda b, i: (b, jnp.maximum(tidx(i) * per - 1, 0), col))
    nxt = pl.BlockSpec((bb, _SUBLANES, width),
                       lambda b, i: (b, jnp.minimum((tidx(i) + 1) * per, last), col))
    return main, prev, nxt


def _const_spec(shape):
    nd = len(shape)
    return pl.BlockSpec(shape, lambda *_: (0,) * nd)


def _resident_spec(shape):
    nd = len(shape)
    return pl.BlockSpec(shape, lambda *_: (0,) * nd, pipeline_mode=pl.Buffered(1))


def _norm_proj_kernel(x_ref, g_ref, w_ref, *o_refs):
    h = _rms(x_ref[...], g_ref[...])
    res = _dot(h, w_ref[...])
    off = 0
    for o in o_refs:
        wdt = o.shape[-1]
        o[...] = res[:, off:off + wdt].astype(o.dtype)
        off += wdt


def _norm_proj(x, g, w, splits, dtypes, tm):
    b, t, d = x.shape
    n = w.shape[1]
    assert sum(splits) == n
    nt = t // tm
    row = lambda wd: pl.BlockSpec((None, tm, wd), lambda bi, i: (bi, i, 0))
    return pl.pallas_call(
        _norm_proj_kernel,
        grid=(b, nt),
        in_specs=[row(d), _const_spec((1, d)), _resident_spec((d, n))],
        out_specs=[row(s) for s in splits],
        out_shape=[jax.ShapeDtypeStruct((b, t, s), dt) for s, dt in zip(splits, dtypes)],
        compiler_params=_cparams(2),
        name="norm_proj",
    )(x, g, w)


def _lru_kernel(*refs, reverse, finalize, tt, nt):
    if finalize:
        xc_ref, wg_ref, bg_ref, lam_ref, hf_ref, gate_ref, o_ref, a_s, u_s, h_s, carry = refs
    else:
        (x_ref, xp_ref, xn_ref, cw_ref, cb_ref, wg_ref, bg_ref, lam_ref, o_ref, xc_ref,
         a_s, u_s, h_s, carry) = refs
    i = pl.program_id(1)
    ti = (nt - 1 - i) if reverse else i
    nseq, _, width = o_ref.shape
    nslab = width // _LANES
    pitch = a_s.shape[1] // nseq

    neg_c_sp = (-_LRU_C) * _softplus(-lam_ref[...])
    xcs = []
    for b in range(nseq):
        if finalize:
            xcs.append(xc_ref[b])
            continue
        cw = cw_ref[...]
        xb = x_ref[b]
        shift = _halo_shifter(xb, xp_ref[b], xn_ref[b], ti, nt)
        xc = cb_ref[...] + shift(2) * cw[0:1]
        xc = xc + shift(1) * cw[1:2]
        xc = xc + xb * cw[2:3]
        xc = xc + shift(-1) * cw[3:4]
        xc_ref[b] = xc
        xcs.append(xc)
    gates_all = _dot(jnp.concatenate(xcs, axis=0), wg_ref[...]) + bg_ref[...]
    for b in range(nseq):
        gates = gates_all[b * tt:(b + 1) * tt]
        r = _sigmoid(gates[:, :width])
        ig = _sigmoid(gates[:, width:])
        log_a = r * neg_c_sp
        th = jnp.tanh(log_a)
        a = jnp.exp(log_a)
        u = jnp.sqrt(-2.0 * th) * lax.rsqrt(1.0 - th) * (ig * xcs[b])
        for j in range(nslab):
            a_s[j, pl.ds(b * pitch, tt), :] = a[:, j * _LANES:(j + 1) * _LANES]
            u_s[j, pl.ds(b * pitch, tt), :] = u[:, j * _LANES:(j + 1) * _LANES]

    @pl.when(i == 0)
    def _():
        carry[...] = jnp.zeros_like(carry)

    def body(s, hs):
        t = (tt - 1 - s) if reverse else s
        new = []
        for j in range(nslab):
            rows = pl.ds(t, nseq, stride=pitch)
            h = a_s[j, rows, :] * hs[j] + u_s[j, rows, :]
            h_s[j, rows, :] = h
            new.append(h)
        return tuple(new)

    hs = lax.fori_loop(0, tt, body, tuple(carry[j] for j in range(nslab)), unroll=_SUBLANES)
    for j in range(nslab):
        carry[j] = hs[j]

    for b in range(nseq):
        h = jnp.concatenate([h_s[j, pl.ds(b * pitch, tt), :] for j in range(nslab)], axis=1)
        if finalize:
            h = (hf_ref[b] + h) * _gelu_tanh(gate_ref[b])
        o_ref[b] = h


def _lru_pass(p_lru, conv_w, conv_b, wg, bg, lam, first, reverse, tt):
    b, t, w2 = p_lru.shape
    w = w2 // 2
    nseq = max(n for n in (8, 4, 2, 1) if b % n == 0)
    tt = _tile(t, max(tt // nseq, _LANES))
    nt = t // tt
    pitch = tt + _SUBLANES if (tt // _SUBLANES) % 2 == 0 else tt
    nslab = w // _LANES
    finalize = first is not None
    main, prev, nxt = _time_specs(w, tt, nt, t, reverse, bb=nseq)
    tidx = (lambda i: nt - 1 - i) if reverse else (lambda i: i)
    tile = lambda col: pl.BlockSpec((nseq, tt, w), lambda bi, i: (bi, tidx(i), col))
    gate_consts = [wg, bg, lam]
    out_sds = jax.ShapeDtypeStruct((b, t, w), _F32)
    if finalize:
        hf, xc = first
        in_specs = [tile(0)] + [_const_spec(z.shape) for z in gate_consts] + [tile(0), tile(1)]
        args = [xc] + gate_consts + [hf, p_lru]
        out_specs, out_shape = tile(0), out_sds
    else:
        in_specs = ([main, prev, nxt, _const_spec(conv_w.shape), _const_spec(conv_b.shape)]
                    + [_const_spec(z.shape) for z in gate_consts])
        args = [p_lru, p_lru, p_lru, conv_w, conv_b] + gate_consts
        out_specs, out_shape = [tile(0), tile(0)], [out_sds, out_sds]
    scratch = [pltpu.VMEM((nslab, nseq * pitch, _LANES), _F32)] * 3
    scratch += [pltpu.VMEM((nslab, nseq, _LANES), _F32)]
    return pl.pallas_call(
        functools.partial(_lru_kernel, reverse=reverse, finalize=finalize, tt=tt, nt=nt),
        grid=(b // nseq, nt),
        in_specs=in_specs,
        out_specs=out_specs,
        out_shape=out_shape,
        scratch_shapes=scratch,
        compiler_params=_cparams(2),
        name="lru_bwd" if reverse else "lru_fwd",
    )(*args)


def _rwkv_kernel(*refs, reverse, finalize, tt, nt):
    if finalize:
        (rws_ref, kkn_ref, ka_ref, w0_ref, a0_ref, wup_ref, aup_ref, onesr_ref, onese_ref,
         yf_ref, rk_ref, lnw_ref, lnb_ref, gup_ref, o_ref, ht_s) = refs
    else:
        (rw_ref, rwp_ref, rwn_ref, mu_ref, kk_ref, ka_ref, w0_ref, a0_ref, wup_ref, aup_ref,
         onesr_ref, onese_ref, o_ref, rwso_ref, kko_ref, ht_s) = refs
    i = pl.program_id(1)
    ti = (nt - 1 - i) if reverse else i
    nseq, ngrp, gw, _ = ht_s.shape
    c = ngrp * gw
    lc = _RW_CHUNK
    hpg = gw // _RW_HEAD
    ones_r = onesr_ref[...]
    ones_e = onese_ref[...]
    seqs = range(nseq)

    def segsum(x):
        return _dot01_right(_dot01_right(x, ones_r), ones_e)

    tok = []
    for b in seqs:
        if finalize:
            rws = rws_ref[b]
        else:
            x = rw_ref[b]
            tshift = _halo_shifter(x, rwp_ref[b], rwn_ref[b], ti, nt)
            rws = x + mu_ref[...] * (0.5 * (tshift(1) + tshift(-1)) - x)
            rwso_ref[b] = rws
        r = rws[:, 0:c]
        k = rws[:, c:2 * c]
        v = rws[:, 2 * c:3 * c]
        wad = rws[:, 3 * c:3 * c + _LANES]
        if finalize:
            kk = kkn_ref[b]
        else:
            kk = k * kk_ref[...]
            kk = kk * jnp.minimum(lax.rsqrt(segsum(kk * kk)), 1e12)
            kko_ref[b] = kk
        lw = (-_RW_DECAY_SCALE) * _sigmoid(w0_ref[...] + _dot(jnp.tanh(wad), wup_ref[...]))
        a = _sigmoid(a0_ref[...] + _dot(wad, aup_ref[...]))
        tok.append(dict(rws=rws, r=r, k=k, v=v, kk=kk, lw=lw, kt=k * (1.0 + (a - 1.0) * ka_ref[...]),
                        bvec=a * kk))

    shift = _RW_HEAD.bit_length() - 1
    row = lax.broadcasted_iota(jnp.int32, (gw, gw), 0)
    col = lax.broadcasted_iota(jnp.int32, (gw, gw), 1)
    same = (row >> shift) == (col >> shift)
    trow = lax.broadcasted_iota(jnp.int32, (lc, gw), 0)
    tcol = lax.broadcasted_iota(jnp.int32, (lc, gw), 1) & (lc - 1)
    m_incl = (tcol >= trow) if reverse else (tcol <= trow)
    m_strict = (tcol > trow) if reverse else (tcol < trow)
    eye_cat = jnp.where(tcol == trow, 1.0, 0.0)
    srow = lax.broadcasted_iota(jnp.int32, (lc, lc), 0)
    scol = lax.broadcasted_iota(jnp.int32, (lc, lc), 1)
    tri01 = ((scol >= srow) if reverse else (scol <= srow)).astype(_BF16)

    def stack(xv):
        xb = xv.astype(_BF16)
        return jnp.where(same, jnp.concatenate([xb] * hpg, axis=0), jnp.zeros((), _BF16))

    @pl.when(i == 0)
    def _():
        ht_s[...] = jnp.zeros_like(ht_s)

    nchunk = tt // lc
    chunks = list(range(nchunk - 1, -1, -1) if reverse else range(nchunk))
    groups = range(ngrp)
    lanes = [slice(g * gw, (g + 1) * gw) for g in groups]
    units = [(b, ci, g) for ci in chunks for b in seqs for g in groups]

    pre = {}
    for ci in chunks:
        sl = slice(ci * lc, (ci + 1) * lc)
        for b in seqs:
            tb = tok[b]
            r_c, lw_c, k_c, v_c, kk_c, b_c = (tb[n][sl] for n in ("r", "lw", "kt", "v", "kk", "bvec"))
            cum = _dot01_left(tri01, lw_c)
            clast = cum[0:1] if reverse else cum[lc - 1:lc]
            einv = jnp.exp(-cum)
            eend = jnp.exp(clast - cum)
            bf = lambda z: z.astype(_BF16)
            pre[b, ci] = dict(gamma=jnp.exp(clast), rg=bf(r_c * jnp.exp(cum)), kkg=bf(kk_c * jnp.exp(cum - lw_c)),
                              kd=bf(k_c * einv), bd=bf(b_c * einv), kend=bf(k_c * eend), bend=bf(b_c * eend),
                              v=bf(v_c))

    st = {}
    for b, ci, g in units:
        ls = lanes[g]
        pc = pre[b, ci]
        lhs = jnp.concatenate([pc["kkg"][:, ls], pc["rg"][:, ls]], axis=0)
        rhs = jnp.concatenate([stack(pc["kd"][:, ls]), stack(pc["bd"][:, ls])], axis=0)
        amat = _dot_nt(lhs, rhs)
        nj = -jnp.where(m_strict, amat[:lc, gw:], 0.0)
        st[b, ci, g] = dict(a_kk_k=jnp.where(m_strict, amat[:lc, :gw], 0.0).astype(_BF16),
                            a_r_k=jnp.where(m_incl, amat[lc:, :gw], 0.0).astype(_BF16),
                            a_r_b=jnp.where(m_incl, amat[lc:, gw:], 0.0).astype(_BF16),
                            nj=nj.astype(_BF16), tinv=eye_cat + nj)
    for key in units:
        s = st[key]
        s["nj"] = _dot(s["nj"], stack(s["nj"])).astype(_BF16)
    for _ in range(lc.bit_length() - 3):
        for key in units:
            s = st[key]
            pn = _dot(jnp.concatenate([s["tinv"].astype(_BF16), s["nj"]], axis=0), stack(s["nj"]))
            s["tinv"] = s["tinv"] + pn[:lc]
            s["nj"] = pn[lc:].astype(_BF16)
    for key in units:
        s = st[key]
        s["tinv"] = s["tinv"] + _dot(s["tinv"], stack(s["nj"]))
    for b, ci, g in units:
        s = st[b, ci, g]
        s["av"] = _dot(jnp.concatenate([s["a_kk_k"], s["a_r_k"]], axis=0), stack(pre[b, ci]["v"][:, lanes[g]]))
    for b, ci, g in units:
        s = st[b, ci, g]
        tk = _dot(s["tinv"], jnp.concatenate([stack(pre[b, ci]["kkg"][:, lanes[g]]), stack(s["av"][:lc])],
                                             axis=1))
        s["kkgp"], s["uv"] = tk[:, :gw].astype(_BF16), tk[:, gw:]
    for b, ci, g in units:
        ls = lanes[g]
        s = st[b, ci, g]
        pc = pre[b, ci]
        s["kb"] = jnp.where(same, _dot_tn(s["kkgp"], pc["bend"][:, ls]), 0.0).astype(_BF16)
        s["gmat"] = jnp.where(same, _dot_tn(jnp.concatenate([pc["v"][:, ls], (-s["uv"]).astype(_BF16)], axis=0),
                                            jnp.concatenate([pc["kend"][:, ls], pc["bend"][:, ls]], axis=0)),
                              0.0)

    chains = [(b, g) for b in seqs for g in groups]
    hts = {key: ht_s[key[0], key[1]] for key in chains}
    ys = {}
    for ci in chunks:
        cur = dict(hts)
        for b, g in chains:
            s = st[b, ci, g]
            hts[b, g] = cur[b, g] * pre[b, ci]["gamma"][:, lanes[g]] - _dot(cur[b, g], s["kb"]) + s["gmat"]
        pus = {(b, g): _dot_nt(jnp.concatenate([st[b, ci, g]["kkgp"], pre[b, ci]["rg"][:, lanes[g]]], axis=0),
                               cur[b, g]) for b, g in chains}
        us = {key: pus[key][:lc] + st[key[0], ci, key[1]]["uv"] for key in chains}
        for b in seqs:
            ys[b, ci] = jnp.concatenate(
                [pus[b, g][lc:] + st[b, ci, g]["av"][lc:] - _dot(st[b, ci, g]["a_r_b"], stack(us[b, g]))
                 for g in groups], axis=1)
    for b, g in chains:
        ht_s[b, g] = hts[b, g]

    for b in seqs:
        y = jnp.concatenate([ys[b, ci] for ci in range(nchunk)], axis=0)
        if finalize:
            tb = tok[b]
            y = yf_ref[b] + y
            inv_n = 1.0 / _RW_HEAD
            mean = segsum(y) * inv_n
            d = y - mean
            var = segsum(d * d) * inv_n
            yn = d * lax.rsqrt(var + _RW_GN_EPS) * lnw_ref[...] + lnb_ref[...]
            bonus = segsum(tb["r"] * tb["k"] * rk_ref[...]) * tb["v"]
            gd = tb["rws"][:, 3 * c + _LANES:3 * c + 2 * _LANES]
            y = (yn + bonus) * _dot(_sigmoid(gd), gup_ref[...])
        o_ref[b] = y


def _rwkv_pass(p_rw, prm, d, first, reverse, tt):
    b, t, cols = p_rw.shape
    c = prm["k_k"].shape[-1]
    nt = t // tt
    nseq = _RW_SEQS if b % _RW_SEQS == 0 else 1
    finalize = first is not None
    main, prev, nxt = _time_specs(cols, tt, nt, t, reverse, bb=nseq)
    tidx = (lambda i: nt - 1 - i) if reverse else (lambda i: i)
    tile = lambda wd: pl.BlockSpec((nseq, tt, wd), lambda bi, i: (bi, tidx(i), 0))
    consts = [prm["k_a"], prm["w0"][d], prm["a0"][d], prm["w_up"][d], prm["a_up"][d], prm["ones_r"],
              prm["ones_e"]]
    y_sds = jax.ShapeDtypeStruct((b, t, c), _F32)
    if finalize:
        yf, rws, kkn = first
        extra = [prm["r_k"], prm["ln_w"], prm["ln_b"], prm["g_up"]]
        args = [rws, kkn] + consts + [yf] + extra
        in_specs = ([tile(cols), tile(c)] + [_const_spec(z.shape) for z in consts] + [tile(c)]
                    + [_const_spec(z.shape) for z in extra])
        out_specs, out_shape = tile(c), y_sds
    else:
        consts = [prm["mu"], prm["k_k"]] + consts
        args = [p_rw, p_rw, p_rw] + consts
        in_specs = [main, prev, nxt] + [_const_spec(z.shape) for z in consts]
        out_specs = [tile(c), tile(cols), tile(c)]
        out_shape = [y_sds, jax.ShapeDtypeStruct((b, t, cols), _F32), y_sds]
    return pl.pallas_call(
        functools.partial(_rwkv_kernel, reverse=reverse, finalize=finalize, tt=tt, nt=nt),
        grid=(b // nseq, nt),
        in_specs=in_specs,
        out_specs=out_specs,
        out_shape=out_shape,
        scratch_shapes=[pltpu.VMEM((nseq, c // _RW_GROUP, _RW_GROUP, _RW_GROUP), _F32)],
        compiler_params=_cparams(2),
        name="rwkv_bwd" if reverse else "rwkv_fwd",
    )(*args)


def _out_proj2_kernel(x_ref, a_ref, b_ref, wa_ref, wb_ref, o_ref):
    o_ref[...] = x_ref[...] + _dot(a_ref[...], wa_ref[...]) + _dot(b_ref[...], wb_ref[...])


def _out_proj2(x, a, bm, wa, wb, tm):
    b, t, d = x.shape
    nt = t // tm
    row = lambda wd: pl.BlockSpec((None, tm, wd), lambda bi, i: (bi, i, 0))
    return pl.pallas_call(
        _out_proj2_kernel,
        grid=(b, nt),
        in_specs=[row(d), row(a.shape[-1]), row(bm.shape[-1]), _const_spec(wa.shape),
                  _const_spec(wb.shape)],
        out_specs=row(d),
        out_shape=jax.ShapeDtypeStruct((b, t, d), _F32),
        compiler_params=_cparams(2),
        name="even_out_proj",
    )(x, a, bm, wa, wb)


def _ffn_kernel(*refs, tm, nt, ck, final_norm):
    x_ref, xp_ref, xn_ref, g_ref, wup_ref, cw_ref, cb_ref, wd_ref = refs[:8]
    if final_norm:
        gf_ref, o_ref, h_s, act_s = refs[8:]
    else:
        o_ref, h_s, act_s = refs[8:]
    i = pl.program_id(1)
    dff = wd_ref.shape[0]
    halo = 2 * _SUBLANES
    g = g_ref[...]
    zeros = jnp.zeros((_SUBLANES, x_ref.shape[-1]), _F32)
    prev = jnp.where(i == 0, 0.0, _rms(xp_ref[...], g))
    nxt = jnp.where(i == nt - 1, 0.0, _rms(xn_ref[...], g))
    h_s[...] = jnp.concatenate([zeros, prev, _rms(x_ref[...], g), nxt, zeros], axis=0).astype(_BF16)
    hext = h_s[...]
    hmain = h_s[pl.ds(halo, tm), :]
    cw = cw_ref[...]
    cb = cb_ref[...]
    for c in range(dff // ck):
        cs = slice(c * ck, (c + 1) * ck)
        ug = jnp.dot(hext, wup_ref[:, dff + c * ck:dff + (c + 1) * ck], preferred_element_type=_F32)
        gate = cb[:, cs] + _shifted(ug, 1, tm, halo) * cw[0:1, cs]
        gate = gate + _shifted(ug, 0, tm, halo) * cw[1:2, cs]
        gate = gate + _shifted(ug, -1, tm, halo) * cw[2:3, cs]
        val = jnp.dot(hmain, wup_ref[:, cs], preferred_element_type=_F32)
        act_s[:, cs] = (_gelu_tanh(gate) * val).astype(_BF16)
    y = x_ref[...] + jnp.dot(act_s[...], wd_ref[...], preferred_element_type=_F32)
    if final_norm:
        y = _rms(y, gf_ref[...])
    o_ref[...] = y


def _ffn(x, g, w_up, conv_w, conv_b, w_down, g_final, tm, ck):
    b, t, d = x.shape
    dff = w_down.shape[0]
    nt = t // tm
    final_norm = g_final is not None
    main, prev, nxt = _time_specs(d, tm, nt, t, False)
    consts = [g, w_up, conv_w, conv_b, w_down] + ([g_final] if final_norm else [])
    return pl.pallas_call(
        functools.partial(_ffn_kernel, tm=tm, nt=nt, ck=ck, final_norm=final_norm),
        grid=(b, nt),
        in_specs=[main, prev, nxt] + [_resident_spec(z.shape) for z in consts],
        out_specs=main,
        out_shape=jax.ShapeDtypeStruct((b, t, d), _F32),
        scratch_shapes=[pltpu.VMEM((tm + 4 * _SUBLANES, d), _BF16), pltpu.VMEM((tm, dff), _BF16)],
        compiler_params=_cparams(2),
        name="conv_ffn",
    )(x, x, x, *consts)


def _mlstm_in_kernel(x_ref, xp_ref, xn_ref, g_ref, w_ref, cw_ref, cb_ref, wq_ref, wk_ref, wv_ref,
                     wg_ref, bg_ref, q_ref, k_ref, v_ref, xc_ref, z_ref, gt_ref, *, tm, nt, cols):
    i = pl.program_id(1)
    inner = z_ref.shape[-1]
    g = g_ref[...]
    hext = jnp.concatenate([_rms(xp_ref[...], g), _rms(x_ref[...], g), _rms(xn_ref[...], g)], axis=0)
    hext_b = hext.astype(_BF16)
    hmain_b = hext[_SUBLANES:_SUBLANES + tm].astype(_BF16)
    cw = cw_ref[...]
    cb = cb_ref[...]
    qscale = (inner // _ML_HEADS) ** -0.5
    nblk = inner // cols
    proj = lambda j: jnp.dot(hext_b, w_ref[:, j * cols:(j + 1) * cols], preferred_element_type=_F32)
    gates = bg_ref[...]
    xme_next = proj(0)
    for j in range(nblk):
        cs = slice(j * cols, (j + 1) * cols)
        xme = xme_next
        if j + 1 < nblk:
            xme_next = proj(j + 1)
        z_ref[:, cs] = jnp.dot(hmain_b, w_ref[:, inner + j * cols:inner + (j + 1) * cols],
                               preferred_element_type=_F32).astype(z_ref.dtype)
        prev = jnp.where(i == 0, 0.0, xme[:_SUBLANES])
        nxt = jnp.where(i == nt - 1, 0.0, xme[_SUBLANES + tm:])
        xm = xme[_SUBLANES:_SUBLANES + tm]
        shift = _halo_shifter(xm, prev, nxt)
        zc = cb[:, cs] + shift(2) * cw[0:1, cs]
        zc = zc + shift(1) * cw[1:2, cs]
        zc = zc + xm * cw[2:3, cs]
        zc = zc + shift(-1) * cw[3:4, cs]
        xc = zc * _sigmoid(zc)
        xc_ref[:, cs] = xc.astype(xc_ref.dtype)
        for jb in range(cols // _ML_BLK):
            blk = j * (cols // _ML_BLK) + jb
            sub = slice(jb * _ML_BLK, (jb + 1) * _ML_BLK)
            sl = slice(blk * _ML_BLK, (blk + 1) * _ML_BLK)
            q = _dot(xc[:, sub], wq_ref[blk])
            k = _dot(xc[:, sub], wk_ref[blk])
            v = _dot(xm[:, sub], wv_ref[blk])
            gates = (gates + _dot(q, wg_ref[sl])
                     + _dot(k, wg_ref[inner + blk * _ML_BLK:inner + (blk + 1) * _ML_BLK])
                     + _dot(v, wg_ref[2 * inner + blk * _ML_BLK:2 * inner + (blk + 1) * _ML_BLK]))
            q_ref[:, sl] = (q * qscale).astype(q_ref.dtype)
            k_ref[:, sl] = k.astype(k_ref.dtype)
            v_ref[:, sl] = v.astype(v_ref.dtype)
    gt_ref[...] = gates


def _mlstm_in(x, g, w_in, prm, tm):
    b, t, d = x.shape
    inner = w_in.shape[1] // 2
    nt = t // tm
    cols = _ML_IN_COLS if inner % _ML_IN_COLS == 0 else inner
    main, prev, nxt = _time_specs(d, tm, nt, t, False)
    consts = [g, w_in, prm["conv_w"], prm["conv_b"], prm["wq"], prm["wk"], prm["wv"], prm["wg"], prm["bg"]]
    row = lambda wd: pl.BlockSpec((None, tm, wd), lambda bi, i: (bi, i, 0))
    big = jax.ShapeDtypeStruct((b, t, inner), _BF16)
    return pl.pallas_call(
        functools.partial(_mlstm_in_kernel, tm=tm, nt=nt, cols=cols),
        grid=(b, nt),
        in_specs=[main, prev, nxt] + [_resident_spec(z.shape) for z in consts],
        out_specs=[row(inner)] * 5 + [row(_LANES)],
        out_shape=[big] * 5 + [jax.ShapeDtypeStruct((b, t, _LANES), _F32)],
        compiler_params=_cparams(2),
        name="mlstm_in",
    )(x, x, x, *consts)


def _mlstm_kernel(*refs, lc, dirs):
    nd = len(dirs)
    ins, outs = refs[:4 * nd], refs[4 * nd:5 * nd]
    c_s, n_s, m_s = refs[5 * nd:]
    i = pl.program_id(1)
    dh = c_s.shape[-1]

    @pl.when(i == 0)
    def _():
        c_s[...] = jnp.zeros_like(c_s)
        n_s[...] = jnp.zeros_like(n_s)
        m_s[...] = jnp.zeros_like(m_s)

    lane = lax.broadcasted_iota(jnp.int32, (lc, _LANES), 1)
    row = lax.broadcasted_iota(jnp.int32, (lc, lc), 0)
    col = lax.broadcasted_iota(jnp.int32, (lc, lc), 1)

    def lane_col(mat, j):
        return jnp.sum(jnp.where(lane == j, mat, 0.0), axis=-1, keepdims=True)

    units = [(x, h) for x in range(nd) for h in range(_ML_HEADS)]
    tri, gates, bcum_all = [], [], []
    for x, (d, reverse) in enumerate(dirs):
        tri.append((col >= row) if reverse else (col <= row))
        gates.append(ins[4 * x + 3][...])
        bcum_all.append(_dot01_left(tri[x].astype(_BF16), -_softplus(-gates[x])))
    sls = [slice(h * dh, (h + 1) * dh) for h in range(_ML_HEADS)]
    li, bcum, m_prev, src_row, dmat, inter, m_t, g, wlog, m_new, wexp, wexp_row, decay = ({} for _ in range(13))
    for u in units:
        x, h = u
        d, reverse = dirs[x]
        li[u] = lane_col(gates[x], d * _ML_HEADS + h)
        bcum[u] = lane_col(bcum_all[x], 2 * _ML_HEADS + d * _ML_HEADS + h)
        m_prev[u] = m_s[x * _ML_HEADS + h][:, 0:1]
        src_row[u] = jnp.broadcast_to(li[u] - bcum[u], (lc, lc)).T
        dmat[u] = jnp.where(tri[x], bcum[u] + src_row[u], -jnp.inf)
        inter[u] = bcum[u] + m_prev[u]
        m_t[u] = jnp.maximum(inter[u], jnp.max(dmat[u], axis=-1, keepdims=True))
        g[u] = bcum[u][0:1] if reverse else bcum[u][lc - 1:lc]
        wlog[u] = g[u] - bcum[u] + li[u]
        m_new[u] = jnp.maximum(g[u] + m_prev[u], jnp.max(wlog[u], axis=0, keepdims=True))
        wexp[u] = jnp.exp(wlog[u] - m_new[u])
        wexp_row[u] = jnp.exp(g[u] + src_row[u][0:2 * _SUBLANES] - m_new[u])
        decay[u] = jnp.exp(g[u] + m_prev[u] - m_new[u])
    q = {(x, h): ins[4 * x][:, sls[h]] for x, h in units}
    k = {(x, h): ins[4 * x + 1][:, sls[h]] for x, h in units}
    v = {(x, h): ins[4 * x + 2][:, sls[h]] for x, h in units}
    cmat = {(x, h): c_s[x * _ML_HEADS + h] for x, h in units}
    nvec = {(x, h): n_s[x * _ML_HEADS + h] for x, h in units}
    s_raw = {u: _dot_nt(q[u], k[u]) for u in units}
    cq = {u: _dot_nt(q[u], cmat[u]) for u in units}
    qn = {u: _dot_nt(q[u], jnp.broadcast_to(nvec[u], (_LANES, dh)))[:, 0:1] for u in units}
    upd = {u: _dot_tn(v[u] * wexp[u].astype(v[u].dtype), k[u]) for u in units}
    nupd = {u: _dot(wexp_row[u], k[u])[0:1] for u in units}
    s = {u: s_raw[u] * jnp.exp(dmat[u] - m_t[u]) for u in units}
    pv = {u: _dot(s[u], v[u]) for u in units}
    for u in units:
        x, h = u
        scale = jnp.exp(inter[u] - m_t[u])
        num = pv[u] + scale * cq[u]
        den = jnp.sum(s[u], axis=-1, keepdims=True) + scale * qn[u]
        outs[x][:, sls[h]] = (num / jnp.maximum(jnp.abs(den), jnp.exp(-m_t[u]))).astype(outs[x].dtype)
        c_s[x * _ML_HEADS + h] = decay[u] * cmat[u] + upd[u]
        n_s[x * _ML_HEADS + h] = decay[u] * nvec[u] + nupd[u]
        m_s[x * _ML_HEADS + h] = jnp.broadcast_to(m_new[u], m_s.shape[1:])


def _mlstm_pass(q, k, v, gt, lc, dirs):
    b, t, inner = q.shape
    nc = t // lc
    dh = inner // _ML_HEADS
    nd = len(dirs)
    in_specs, args = [], []
    out_specs = []
    for _, reverse in dirs:
        tidx = (lambda i: nc - 1 - i) if reverse else (lambda i: i)
        row = lambda wd, tidx=tidx: pl.BlockSpec((None, lc, wd), lambda bi, i: (bi, tidx(i), 0))
        in_specs += [row(inner)] * 3 + [row(_LANES)]
        args += [q, k, v, gt]
        out_specs.append(row(inner))
    return pl.pallas_call(
        functools.partial(_mlstm_kernel, lc=lc, dirs=tuple(dirs)),
        grid=(b, nc),
        in_specs=in_specs,
        out_specs=out_specs,
        out_shape=[jax.ShapeDtypeStruct((b, t, inner), _BF16)] * nd,
        scratch_shapes=[pltpu.VMEM((nd * _ML_HEADS, dh, dh), _F32), pltpu.VMEM((nd * _ML_HEADS, 1, dh), _F32),
                        pltpu.VMEM((nd * _ML_HEADS, 1, _LANES), _F32)],
        compiler_params=_cparams(2),
        name="mlstm_scan",
    )(*args)


def _mlstm_out_kernel(x_ref, hf_ref, hb_ref, xc_ref, z_ref, nw_ref, skip_ref, wo_ref, o_ref):
    inner = hf_ref.shape[-1]
    dh = inner // _ML_HEADS
    acc = x_ref[...]
    for h in range(_ML_HEADS):
        sl = slice(h * dh, (h + 1) * dh)
        hh = hf_ref[:, sl].astype(_F32) + hb_ref[:, sl].astype(_F32)
        mean = jnp.mean(hh, axis=-1, keepdims=True)
        dlt = hh - mean
        var = jnp.mean(dlt * dlt, axis=-1, keepdims=True)
        hn = dlt * lax.rsqrt(var + _ML_LN_EPS) * nw_ref[:, sl]
        z = z_ref[:, sl].astype(_F32)
        out = (hn + skip_ref[:, sl] * xc_ref[:, sl].astype(_F32)) * (z * _sigmoid(z))
        acc = acc + _dot(out, wo_ref[sl, :])
    o_ref[...] = acc


def _mlstm_out(x, hf, hb, xc, z, norm_w, skip, w_out, tm):
    b, t, d = x.shape
    inner = hf.shape[-1]
    nt = t // tm
    row = lambda wd: pl.BlockSpec((None, tm, wd), lambda bi, i: (bi, i, 0))
    return pl.pallas_call(
        _mlstm_out_kernel,
        grid=(b, nt),
        in_specs=[row(d)] + [row(inner)] * 4 + [_const_spec(norm_w.shape), _const_spec(skip.shape),
                                                _const_spec(w_out.shape)],
        out_specs=row(d),
        out_shape=jax.ShapeDtypeStruct((b, t, d), _F32),
        compiler_params=_cparams(2),
        name="mlstm_out_proj",
    )(x, hf, hb, xc, z, norm_w, skip, w_out)


def _block_diag_dense(w):
    nb, bi, bo = w.shape
    eye = jnp.eye(nb, dtype=w.dtype)
    return jnp.einsum("nio,nm->nimo", w, eye).reshape(nb * bi, nb * bo)


def _block_diag_blocks(w, blk):
    nb, bi, bo = w.shape
    per = blk // bi
    return jax.vmap(_block_diag_dense)(w.reshape(nb // per, per, bi, bo))


def _row(v):
    return v.reshape(1, -1).astype(_F32)


def _tile(t_len, want):
    return min(want, t_len)


def _even_layer(x, li, p):
    (w_in, conv_w, conv_b, wa, ba, wx, bx, lam, mu, w0, w_up, a0, a_up, g_up, k_k, k_a, r_k,
     ln_w, ln_b, w_out) = [z[li] for z in p["even"]]
    t = x.shape[1]
    width = lam.shape[-1]
    c = k_k.shape[-1]
    p_lru, p_rw = _norm_proj(x, _row(p["norm_mix"][2 * li]), w_in.astype(_BF16),
                             (2 * width, w_in.shape[1] - 2 * width), (_F32, _F32),
                             _tile(t, _NORM_TILE))

    tt = _tile(t, _LRU_TILE)
    h_dir = None
    for d, rev in enumerate((False, True)):
        wg = jnp.concatenate([_block_diag_dense(wa[d]), _block_diag_dense(wx[d])], axis=1).astype(_BF16)
        bg = jnp.concatenate([ba[d], bx[d]]).reshape(1, -1)
        h_dir = _lru_pass(p_lru, conv_w, _row(conv_b), wg, bg, _row(lam[d]), h_dir, rev, tt)
    lru_out = h_dir

    zeros = jnp.zeros((_RW_HEAD, c), _F32)
    head = jnp.arange(c) // _RW_HEAD
    prm = {
        "mu": _row(mu), "k_k": _row(k_k), "k_a": _row(k_a),
        "w0": [_row(w0[d]) for d in range(2)], "a0": [_row(a0[d]) for d in range(2)],
        "w_up": [jnp.concatenate([w_up[d], zeros], axis=0).astype(_BF16) for d in range(2)],
        "a_up": [jnp.concatenate([zeros, a_up[d]], axis=0).astype(_BF16) for d in range(2)],
        "ones_r": (head[:, None] == jnp.arange(_LANES)[None, :]).astype(_BF16),
        "ones_e": (jnp.arange(_LANES)[:, None] == head[None, :]).astype(_BF16),
        "r_k": _row(r_k), "ln_w": _row(ln_w), "ln_b": _row(ln_b), "g_up": g_up.astype(_BF16),
    }
    tt = _tile(t, _RW_TILE)
    y_dir = None
    for d, rev in enumerate((False, True)):
        y_dir = _rwkv_pass(p_rw, prm, d, y_dir, rev, tt)
    rw_out = y_dir

    wo = w_out.astype(_BF16)
    return _out_proj2(x, lru_out, rw_out, wo[:width], wo[width:], _tile(t, _ROW_TILE))


def _odd_layer(x, li, p):
    (w_in, conv_w, conv_b, w_q, w_k, w_v, w_ig, b_ig, w_fg, b_fg, skip, norm_w,
     w_out) = [z[li] for z in p["odd"]]
    t = x.shape[1]
    inner = skip.shape[-1]
    nh = b_ig.shape[-1]
    wg = jnp.concatenate([w_ig[0], w_ig[1], w_fg[0], w_fg[1],
                          jnp.zeros((3 * inner, _LANES - 4 * nh), _F32)], axis=1).astype(_BF16)
    bg = jnp.concatenate([b_ig[0], b_ig[1], b_fg[0], b_fg[1], jnp.zeros((_LANES - 4 * nh,), _F32)])
    prm = {
        "conv_w": conv_w, "conv_b": _row(conv_b),
        "wq": _block_diag_blocks(w_q, _ML_BLK).astype(_BF16),
        "wk": _block_diag_blocks(w_k, _ML_BLK).astype(_BF16),
        "wv": _block_diag_blocks(w_v, _ML_BLK).astype(_BF16),
        "wg": wg, "bg": _row(bg),
    }
    q, k, v, xc, z, gt = _mlstm_in(x, _row(p["norm_mix"][2 * li + 1]), w_in.astype(_BF16), prm,
                                   _tile(t, _ML_ROW_TILE))
    lc = _tile(t, _ML_TILE)
    (hf,) = _mlstm_pass(q, k, v, gt, lc, ((0, False),))
    (hb,) = _mlstm_pass(q, k, v, gt, lc, ((1, True),))
    return _mlstm_out(x, hf, hb, xc, z, _row(norm_w), _row(skip), w_out.astype(_BF16),
                      _tile(t, _ML_OUT_TILE))


def _trunk(x, p):
    depth = p["norm_mix"].shape[0]
    t = x.shape[1]
    for layer in range(depth):
        if layer % 2 == 0:
            x = _even_layer(x, layer // 2, p)
        else:
            x = _odd_layer(x, layer // 2, p)
        w_up, conv_w, conv_b, w_down = [z[layer] for z in p["ffn"]]
        dff = w_down.shape[0]
        ck = _FFN_COLS if dff % _FFN_COLS == 0 else dff
        g_final = _row(p["norm_final"]) if layer == depth - 1 else None
        x = _ffn(x, _row(p["norm_ffn"][layer]), w_up.astype(_BF16), conv_w, _row(conv_b),
                 w_down.astype(_BF16), g_final, _tile(t, _ROW_TILE), ck)
    return x


def kernel(x_prompt, x_sample, ev_w_in, ev_lru_conv_w, ev_lru_conv_b, ev_lru_wa, ev_lru_ba, ev_lru_wx, ev_lru_bx, ev_lru_lambda, ev_rw_mu, ev_rw_w0, ev_rw_w_up, ev_rw_a0, ev_rw_a_up, ev_rw_g_up, ev_rw_k_k, ev_rw_k_a, ev_rw_r_k, ev_rw_ln_w, ev_rw_ln_b, ev_w_out, od_w_in, od_conv_w, od_conv_b, od_w_q, od_w_k, od_w_v, od_w_ig, od_b_ig, od_w_fg, od_b_fg, od_skip, od_norm_w, od_w_out, ffn_w_up, ffn_conv_w, ffn_conv_b, ffn_w_down, norm_mix, norm_ffn, norm_final):
    p = {
        "even": (ev_w_in, ev_lru_conv_w, ev_lru_conv_b, ev_lru_wa, ev_lru_ba, ev_lru_wx, ev_lru_bx,
                 ev_lru_lambda, ev_rw_mu, ev_rw_w0, ev_rw_w_up, ev_rw_a0, ev_rw_a_up, ev_rw_g_up,
                 ev_rw_k_k, ev_rw_k_a, ev_rw_r_k, ev_rw_ln_w, ev_rw_ln_b, ev_w_out),
        "odd": (od_w_in, od_conv_w, od_conv_b, od_w_q, od_w_k, od_w_v, od_w_ig, od_b_ig, od_w_fg,
                od_b_fg, od_skip, od_norm_w, od_w_out),
        "ffn": (ffn_w_up, ffn_conv_w, ffn_conv_b, ffn_w_down),
        "norm_mix": norm_mix, "norm_ffn": norm_ffn, "norm_final": norm_final,
    }
    return (_trunk(x_prompt, p), _trunk(x_sample, p))
```
